```python
import jax
import jax.numpy as jnp
from jax import lax

D_MODEL = 1024
BATCH = 8
SEQ = 2048
DEPTH = 2
DEC_BATCH = 32
DEC_SEQ = 4
PAST_LEN = 16384
PAGE_SIZE = 128

HEAD_DIM = 64
D_MIX = D_MODEL
N_MIX_HEADS = D_MIX // HEAD_DIM
GDN_HEADS = 3 * N_MIX_HEADS // 8
FOX_HEADS = 3 * N_MIX_HEADS // 8
CHUNK_GROUPS = N_MIX_HEADS - GDN_HEADS - FOX_HEADS
W_A = GDN_HEADS * HEAD_DIM
W_B = FOX_HEADS * HEAD_DIM
W_C = CHUNK_GROUPS * HEAD_DIM
D_IN = 4 * W_A + 2 * GDN_HEADS + 3 * W_B + FOX_HEADS + 2 * W_C
GDN_CONV = 4
GDN_CHUNK = 64
FOX_BLOCK = 128
CHUNK_LEN = 128
FFN_CONV = 3
D_FF = ((8 * D_MODEL // 3 + 127) // 128) * 128
FOX_F_BIAS_MIN = 2.0
FOX_F_BIAS_MAX = 10.0
EPS = 1e-6

kernel_name = 'hybrid_gdn_fox_chunkmlp_step'


def rms_norm(x, w):
    xf = x.astype(jnp.float32)
    y = xf * lax.rsqrt(jnp.mean(xf * xf, axis=-1, keepdims=True) + EPS)
    return (y * w.astype(jnp.float32)).astype(x.dtype)


def l2_normalize(x):
    xf = x.astype(jnp.float32)
    return xf * lax.rsqrt(jnp.sum(xf * xf, axis=-1, keepdims=True) + EPS)


def causal_dwconv(x, buf, w):
    k = w.shape[0]
    L = x.shape[1]
    xp = jnp.concatenate([buf.astype(x.dtype), x], axis=1)
    y = xp[:, 0:L] * w[0]
    for i in range(1, k):
        y = y + xp[:, i:i + L] * w[i]
    return y, xp[:, L:]


def split_input_projection(proj):
    sizes = [3 * W_A, W_A, GDN_HEADS, GDN_HEADS, 3 * W_B, FOX_HEADS, 2 * W_C]
    idx = []
    acc = 0
    for s in sizes[:-1]:
        acc += s
        idx.append(acc)
    return jnp.split(proj, idx, axis=-1)


def gated_delta_rule(q, k, v, g, beta, s0):
    f32 = jnp.float32
    b, L, h, dk = q.shape
    dv = v.shape[-1]
    c = min(GDN_CHUNK, L)
    pad = (-L) % c
    n = (L + pad) // c

    def prep(t):
        t = t.astype(f32)
        t = jnp.pad(t, [(0, 0), (0, pad)] + [(0, 0)] * (t.ndim - 2))
        t = t.reshape((b, n, c) + t.shape[2:])
        return jnp.transpose(t, (1, 0, 3, 2) + tuple(range(4, t.ndim)))

    qc = prep(q) * dk ** -0.5
    kc = prep(k)
    vc = prep(v)
    bc = prep(beta)
    gcum = jnp.cumsum(prep(g), axis=-1)
    tri = jnp.tril(jnp.ones((c, c), bool))
    strict = jnp.tril(jnp.ones((c, c), bool), -1)
    diff = gcum[..., :, None] - gcum[..., None, :]
    decay = jnp.where(tri, jnp.exp(jnp.where(tri, diff, 0.0)), 0.0)
    kb = kc * bc[..., None]
    lmat = jnp.where(strict, jnp.einsum('nbhid,nbhjd->nbhij', kb, kc) * decay, 0.0)
    rhs = jnp.concatenate([vc * bc[..., None], kb * jnp.exp(gcum)[..., None]], axis=-1)
    sol = lax.linalg.triangular_solve(lmat + jnp.eye(c, dtype=f32), rhs,
                                      left_side=True, lower=True, unit_diagonal=True)
    u, w = sol[..., :dv], sol[..., dv:]
    attn = jnp.where(tri, jnp.einsum('nbhid,nbhjd->nbhij', qc, kc) * decay, 0.0)

    def step(S, xs):
        q_i, k_i, u_i, w_i, g_i, a_i = xs
        v_new = u_i - jnp.einsum('bhid,bhde->bhie', w_i, S)
        o = (jnp.einsum('bhid,bhde->bhie', q_i * jnp.exp(g_i)[..., None], S)
             + jnp.einsum('bhij,bhje->bhie', a_i, v_new))
        g_last = g_i[..., -1:]
        S = (S * jnp.exp(g_last)[..., None]
             + jnp.einsum('bhid,bhie->bhde', k_i * jnp.exp(g_last - g_i)[..., None], v_new))
        return S, o

    s_fin, o = lax.scan(step, s0.astype(f32), (qc, kc, u, w, gcum, attn))
    o = jnp.transpose(o, (1, 0, 3, 2, 4)).reshape(b, n * c, h, dv)[:, :L]
    return o, s_fin


def fox_prompt(q, k, v, logf):
    b, L, h, d = q.shape
    scale = d ** -0.5
    cum = jnp.transpose(lax.cumsum(logf.astype(jnp.float32), axis=1), (0, 2, 1))
    outs = []
    for i in range(L // FOX_BLOCK):
        q0, q1 = i * FOX_BLOCK, (i + 1) * FOX_BLOCK
        s = jnp.einsum('bqhd,bkhd->bhqk', q[:, q0:q1], k[:, :q1]).astype(jnp.float32) * scale
        s = s + cum[:, :, q0:q1, None] - cum[:, :, None, :q1]
        causal = jnp.arange(q0, q1)[:, None] >= jnp.arange(q1)[None, :]
        p = jax.nn.softmax(jnp.where(causal, s, -jnp.inf), axis=-1)
        outs.append(jnp.einsum('bhqk,bkhd->bqhd', p.astype(v.dtype), v[:, :q1]))
    return jnp.concatenate(outs, axis=1)


def fox_sample(q, k, v, logf, k_past, v_past, logf_past):
    b, T, h, d = q.shape
    P = k_past.shape[1]
    scale = d ** -0.5
    cn = jnp.transpose(lax.cumsum(logf.astype(jnp.float32), axis=1), (0, 2, 1))
    lp = logf_past.astype(jnp.float32)
    suffix = jnp.transpose(lax.cumsum(lp, axis=1, reverse=True) - lp, (0, 2, 1))
    s_past = (jnp.einsum('bqhd,bkhd->bhqk', q, k_past).astype(jnp.float32) * scale
              + cn[..., None] + suffix[:, :, None, :])
    s_new = (jnp.einsum('bqhd,bkhd->bhqk', q, k).astype(jnp.float32) * scale
             + cn[..., :, None] - cn[..., None, :])
    causal = jnp.tril(jnp.ones((T, T), bool))
    s_new = jnp.where(causal, s_new, -jnp.inf)
    p = jax.nn.softmax(jnp.concatenate([s_past, s_new], axis=-1), axis=-1)
    return (jnp.einsum('bhqk,bkhd->bqhd', p[..., :P].astype(v.dtype), v_past)
            + jnp.einsum('bhqk,bkhd->bqhd', p[..., P:].astype(v.dtype), v))


def chunk_mlp(u, vv, w_s, b_s):
    b, L, G, d = u.shape
    c = min(L, CHUNK_LEN)
    n = L // c
    tri = jnp.tril(jnp.ones((c, c), bool))
    wm = jnp.where(tri, w_s[:, :c, :c], 0.0).astype(vv.dtype)
    vc = vv.reshape(b, n, c, G, d)
    mix = jnp.einsum('gts,bnsgd->bntgd', wm, vc) + jnp.transpose(b_s[:, :c])[None, None, :, :, None]
    return u * mix.reshape(b, L, G, d)


def hybrid_layer(x, p, gdn_state, gdn_buf, ffn_buf, fox_past):
    f32 = jnp.float32
    b, L, _ = x.shape
    h = rms_norm(x, p['norm_mix'])
    a_qkv, a_z, a_beta, a_alpha, b_qkv, b_f, c_uv = split_input_projection(h @ p['w_in'])

    a_conv, gdn_buf_new = causal_dwconv(a_qkv, gdn_buf, p['gdn_conv_w'])
    a4 = jax.nn.silu(a_conv).reshape(b, L, 3, GDN_HEADS, HEAD_DIM)
    beta = jax.nn.sigmoid(a_beta.astype(f32))
    g = -jnp.exp(p['gdn_a_log'].astype(f32)) * jax.nn.softplus(a_alpha.astype(f32) + p['gdn_dt_bias'].astype(f32))
    o_a, gdn_state_new = gated_delta_rule(l2_normalize(a4[:, :, 0]), l2_normalize(a4[:, :, 1]), a4[:, :, 2],
                                          g, beta, gdn_state)
    o_a = rms_norm(o_a, p['gdn_norm_w']) * jax.nn.silu(a_z.reshape(b, L, GDN_HEADS, HEAD_DIM).astype(f32))

    b4 = b_qkv.reshape(b, L, 3, FOX_HEADS, HEAD_DIM)
    bq, bk, bv = b4[:, :, 0], b4[:, :, 1], b4[:, :, 2]
    logf = jax.nn.log_sigmoid(b_f.astype(f32) + p['fox_f_bias'].astype(f32))
    if fox_past is None:
        o_b = fox_prompt(bq, bk, bv, logf)
    else:
        o_b = fox_sample(bq, bk, bv, logf, fox_past[0], fox_past[1], fox_past[2])
    o_b = rms_norm(o_b, p['fox_out_norm'])

    cu, cv = jnp.split(jax.nn.gelu(c_uv), 2, axis=-1)
    cu = cu.reshape(b, L, CHUNK_GROUPS, HEAD_DIM)
    cv = rms_norm(cv.reshape(b, L, CHUNK_GROUPS, HEAD_DIM), p['chunk_v_norm'])
    o_c = chunk_mlp(cu, cv, p['chunk_ws'], p['chunk_bs'])

    mix = jnp.concatenate([o_a.reshape(b, L, W_A).astype(x.dtype),
                           o_b.reshape(b, L, W_B).astype(x.dtype),
                           o_c.reshape(b, L, W_C).astype(x.dtype)], axis=-1)
    x = x + mix @ p['w_out']

    h2 = rms_norm(x, p['norm_ffn'])
    gate, ffn_buf_new = causal_dwconv(h2 @ p['ffn_w_gate'], ffn_buf, p['ffn_conv_w'])
    x = x + (jax.nn.silu(gate) * (h2 @ p['ffn_w_up'])) @ p['ffn_w_down']
    return x, (bk, bv, logf, gdn_state_new, gdn_buf_new, ffn_buf_new, cv.reshape(b, L, W_C))


def setup_inputs(seed: int = 0) -> dict:
    key = jax.random.key(seed)
    ks = jax.random.split(key, 32)
    f32 = jnp.float32
    n_pages = PAST_LEN // PAGE_SIZE
    n_phys = (5 * DEC_BATCH * n_pages) // 4

    def nrm(k, shape, scale=1.0):
        return jax.random.normal(k, shape, f32) * scale

    def gain(k, shape):
        return 1.0 + 0.02 * jax.random.normal(k, shape, f32)

    fox_f_bias = (jnp.linspace(FOX_F_BIAS_MIN, FOX_F_BIAS_MAX, FOX_HEADS, dtype=f32)[None, :]
                  + 0.1 * nrm(ks[15], (DEPTH, FOX_HEADS)))
    page_table = jax.random.permutation(ks[5], n_phys)[:DEC_BATCH * n_pages].reshape(DEC_BATCH, n_pages).astype(jnp.int32)
    return {
        'x_prompt': nrm(ks[0], (BATCH, SEQ, D_MODEL)),
        'x_sample': nrm(ks[1], (DEC_BATCH, DEC_SEQ, D_MODEL)),
        'cache_fox_k': nrm(ks[2], (DEPTH, n_phys, PAGE_SIZE, FOX_HEADS, HEAD_DIM)),
        'cache_fox_v': nrm(ks[3], (DEPTH, n_phys, PAGE_SIZE, FOX_HEADS, HEAD_DIM)),
        'cache_fox_logf': jax.nn.log_sigmoid(fox_f_bias[:, None, None, :]
                                             + nrm(ks[4], (DEPTH, n_phys, PAGE_SIZE, FOX_HEADS))),
        'page_table': page_table,
        'state_gdn': nrm(ks[6], (DEPTH, DEC_BATCH, GDN_HEADS, HEAD_DIM, HEAD_DIM), 0.5),
        'state_gdn_conv': nrm(ks[7], (DEPTH, DEC_BATCH, GDN_CONV - 1, 3 * W_A)),
        'state_ffn_conv': nrm(ks[8], (DEPTH, DEC_BATCH, FFN_CONV - 1, D_FF)),
        'norm_mix': gain(ks[9], (DEPTH, D_MODEL)),
        'w_in': nrm(ks[10], (DEPTH, D_MODEL, D_IN), D_MODEL ** -0.5),
        'gdn_conv_w': nrm(ks[11], (DEPTH, GDN_CONV, 3 * W_A), GDN_CONV ** -0.5),
        'gdn_a_log': jnp.log(jax.random.uniform(ks[12], (DEPTH, GDN_HEADS), f32, 1.0, 16.0)),
        'gdn_dt_bias': jnp.log(jnp.expm1(jax.random.uniform(ks[13], (DEPTH, GDN_HEADS), f32, 0.001, 0.1))),
        'gdn_norm_w': gain(ks[14], (DEPTH, HEAD_DIM)),
        'fox_f_bias': fox_f_bias,
        'fox_out_norm': gain(ks[16], (DEPTH, HEAD_DIM)),
        'chunk_v_norm': gain(ks[17], (DEPTH, HEAD_DIM)),
        'chunk_ws': nrm(ks[18], (DEPTH, CHUNK_GROUPS, CHUNK_LEN, CHUNK_LEN), CHUNK_LEN ** -0.5),
        'chunk_bs': 1.0 + 0.1 * nrm(ks[19], (DEPTH, CHUNK_GROUPS, CHUNK_LEN)),
        'w_out': nrm(ks[20], (DEPTH, D_MIX, D_MODEL), D_MIX ** -0.5),
        'norm_ffn': gain(ks[21], (DEPTH, D_MODEL)),
        'ffn_w_gate': nrm(ks[22], (DEPTH, D_MODEL, D_FF), D_MODEL ** -0.5),
        'ffn_w_up': nrm(ks[23], (DEPTH, D_MODEL, D_FF), D_MODEL ** -0.5),
        'ffn_conv_w': nrm(ks[24], (DEPTH, FFN_CONV, D_FF), FFN_CONV ** -0.5),
        'ffn_w_down': nrm(ks[25], (DEPTH, D_FF, D_MODEL), D_FF ** -0.5),
        'norm_final': gain(ks[26], (D_MODEL,)),
    }


def reference(x_prompt, x_sample, cache_fox_k, cache_fox_v, cache_fox_logf, page_table,
              state_gdn, state_gdn_conv, state_ffn_conv,
              norm_mix, w_in, gdn_conv_w, gdn_a_log, gdn_dt_bias, gdn_norm_w,
              fox_f_bias, fox_out_norm, chunk_v_norm, chunk_ws, chunk_bs, w_out,
              norm_ffn, ffn_w_gate, ffn_w_up, ffn_conv_w, ffn_w_down, norm_final):
    f32 = jnp.float32
    b_p = x_prompt.shape[0]
    b_s = x_sample.shape[0]
    past_len = page_table.shape[1] * PAGE_SIZE
    xp, xs = x_prompt, x_sample
    prompt_rows, sample_rows = [], []
    for l in range(DEPTH):
        p = {'norm_mix': norm_mix[l], 'w_in': w_in[l], 'gdn_conv_w': gdn_conv_w[l],
             'gdn_a_log': gdn_a_log[l], 'gdn_dt_bias': gdn_dt_bias[l], 'gdn_norm_w': gdn_norm_w[l],
             'fox_f_bias': fox_f_bias[l], 'fox_out_norm': fox_out_norm[l], 'chunk_v_norm': chunk_v_norm[l],
             'chunk_ws': chunk_ws[l], 'chunk_bs': chunk_bs[l], 'w_out': w_out[l], 'norm_ffn': norm_ffn[l],
             'ffn_w_gate': ffn_w_gate[l], 'ffn_w_up': ffn_w_up[l], 'ffn_conv_w': ffn_conv_w[l],
             'ffn_w_down': ffn_w_down[l]}
        xp, rows_p = hybrid_layer(xp, p,
                                  jnp.zeros((b_p, GDN_HEADS, HEAD_DIM, HEAD_DIM), f32),
                                  jnp.zeros((b_p, GDN_CONV - 1, 3 * W_A), xp.dtype),
                                  jnp.zeros((b_p, FFN_CONV - 1, D_FF), xp.dtype),
                                  None)
        fox_past = (cache_fox_k[l][page_table].reshape(b_s, past_len, FOX_HEADS, HEAD_DIM),
                    cache_fox_v[l][page_table].reshape(b_s, past_len, FOX_HEADS, HEAD_DIM),
                    cache_fox_logf[l][page_table].reshape(b_s, past_len, FOX_HEADS))
        xs, rows_s = hybrid_layer(xs, p, state_gdn[l], state_gdn_conv[l], state_ffn_conv[l], fox_past)
        prompt_rows.append(rows_p)
        sample_rows.append(rows_s)

    def stack(rows, i):
        return jnp.stack([r[i] for r in rows])

    y_prompt = rms_norm(xp, norm_final)
    y_sample = rms_norm(xs, norm_final)
    fox_k_prompt = stack(prompt_rows, 0)
    fox_v_prompt = stack(prompt_rows, 1)
    fox_logf_prompt = stack(prompt_rows, 2)
    gdn_state_prompt = stack(prompt_rows, 3)
    gdn_conv_prompt = stack(prompt_rows, 4)
    ffn_conv_prompt = stack(prompt_rows, 5)
    fox_k_sample = stack(sample_rows, 0)
    fox_v_sample = stack(sample_rows, 1)
    fox_logf_sample = stack(sample_rows, 2)
    gdn_state_sample = stack(sample_rows, 3)
    gdn_conv_sample = stack(sample_rows, 4)
    ffn_conv_sample = stack(sample_rows, 5)
    chunk_v_sample = stack(sample_rows, 6)
    return (y_prompt, y_sample, fox_k_prompt, fox_v_prompt, fox_logf_prompt, gdn_state_prompt,
            gdn_conv_prompt, ffn_conv_prompt, fox_k_sample, fox_v_sample, fox_logf_sample,
            gdn_state_sample, gdn_conv_sample, ffn_conv_sample, chunk_v_sample)
```

```python
import functools
import math

import jax
import jax.numpy as jnp
from jax import lax
from jax.experimental import pallas as pl
from jax.experimental.pallas import tpu as pltpu

F32 = jnp.float32
BF16 = jnp.bfloat16
EPS = 1e-6
LANES = 128
SUBLANES = 8
HEAD_DIM = 64
PAIR = 2 * HEAD_DIM
GDN_CHUNK = 64
FOX_PAGES_PER_STEP = 8
VMEM_LIMIT = 56 * 1024 * 1024
HIGHEST = lax.Precision.HIGHEST
NEG_INF = -1e30

SM_BETA, SM_G, SM_F = 0, 8, 16


def _cparams(sem):
    return pltpu.CompilerParams(dimension_semantics=sem, vmem_limit_bytes=VMEM_LIMIT)


def _dot(a, b):
    return jnp.dot(a.astype(BF16), b.astype(BF16), preferred_element_type=F32)


def _dot_nt(a, b):
    return lax.dot_general(a.astype(BF16), b.astype(BF16), (((1,), (1,)), ((), ())),
                           preferred_element_type=F32)


def _dot_tn(a, b):
    return lax.dot_general(a.astype(BF16), b.astype(BF16), (((0,), (0,)), ((), ())),
                           preferred_element_type=F32)


def _dot2(a, b01):
    hi = a.astype(BF16)
    lo = (a - hi.astype(F32)).astype(BF16)
    b = b01.astype(BF16)
    return (jnp.dot(hi, b, preferred_element_type=F32) + jnp.dot(lo, b, preferred_element_type=F32))


def _iota(shape, dim):
    return lax.broadcasted_iota(jnp.int32, shape, dim)


def _head_ones(n):
    return (_iota((n, n), 0) // HEAD_DIM == _iota((n, n), 1) // HEAD_DIM).astype(F32)


def _silu(x):
    return x * jax.nn.sigmoid(x)


def _inproj_kernel(x_ref, nw_ref, w_ref, par_ref,
                   aqkv_ref, az_ref, bq_ref, bk_ref, bv_ref, cuv_ref, small_ref, cumt_ref,
                   carry_ref, *, offs, seq_tiles, seq_len):
    i = pl.program_id(0)
    tm = x_ref.shape[0]
    x = x_ref[...]
    h = (x * lax.rsqrt(jnp.mean(x * x, axis=-1, keepdims=True) + EPS) * nw_ref[...]).astype(BF16)

    def mm(name):
        lo, hi = offs[name]
        return jnp.dot(h, w_ref[:, lo:hi], preferred_element_type=F32)

    aqkv_ref[...] = mm('a_qkv')
    az_ref[...] = mm('a_z')
    bq_ref[...] = mm('b_q')
    bk_ref[...] = mm('b_k')
    bv_ref[...] = mm('b_v')
    cuv_ref[...] = mm('c_uv')

    z = mm('small') + par_ref[0:1, :]
    lane = _iota(z.shape, 1)
    t = jnp.log1p(jnp.exp(-jnp.abs(z)))
    softplus = jnp.maximum(z, 0.0) + t
    log_sig = jnp.minimum(z, 0.0) - t
    sm = jnp.where(lane < SM_G, jax.nn.sigmoid(z),
                   jnp.where(lane < SM_F, -jnp.exp(par_ref[1:2, :]) * softplus, log_sig))
    small_ref[...] = sm

    lb = min(seq_len, tm)
    row, col = _iota((tm, tm), 0), _iota((tm, tm), 1)
    tri = jnp.where((col <= row) & (row // lb == col // lb), 1.0, 0.0).astype(F32)
    cum = jnp.dot(tri, sm, precision=HIGHEST, preferred_element_type=F32)
    if seq_tiles > 1:
        @pl.when(i % seq_tiles == 0)
        def _():
            carry_ref[...] = jnp.zeros_like(carry_ref)
        cum = cum + carry_ref[0:1, :]
        carry_ref[0:1, :] = cum[tm - 1:tm, :]
    cumt_ref[...] = cum.T[SM_F:SM_F + SUBLANES, :]


def _inproj(x, nw, w, par, offs, seq_len, tm):
    n, d = x.shape
    assert n % tm == 0 and (seq_len % tm == 0 or tm % seq_len == 0)
    seq_tiles = max(seq_len // tm, 1)
    widths = {k: hi - lo for k, (lo, hi) in offs.items()}
    names = ['a_qkv', 'a_z', 'b_q', 'b_k', 'b_v', 'c_uv', 'small']
    out_shape = [jax.ShapeDtypeStruct((n, widths[k]), F32) for k in names]
    out_specs = [pl.BlockSpec((tm, widths[k]), lambda i: (i, 0)) for k in names]
    out_shape.append(jax.ShapeDtypeStruct((SUBLANES, n), F32))
    out_specs.append(pl.BlockSpec((SUBLANES, tm), lambda i: (0, i)))
    return pl.pallas_call(
        functools.partial(_inproj_kernel, offs=offs, seq_tiles=seq_tiles, seq_len=seq_len),
        grid=(n // tm,),
        in_specs=[pl.BlockSpec((tm, d), lambda i: (i, 0)),
                  pl.BlockSpec((1, d), lambda i: (0, 0)),
                  pl.BlockSpec(w.shape, lambda i: (0, 0)),
                  pl.BlockSpec(par.shape, lambda i: (0, 0))],
        out_specs=out_specs,
        out_shape=out_shape,
        scratch_shapes=[pltpu.VMEM((SUBLANES, LANES), F32)],
        compiler_params=_cparams(("arbitrary",)),
        name="inproj",
    )(x, nw, w, par)


def _unit_lower_inverse(lmat):
    n = lmat.shape[0]
    row, col = _iota((n, n), 0), _iota((n, n), 1)
    eye = (row == col).astype(F32)
    base = SUBLANES
    ld = jnp.where(row // base == col // base, lmat, 0.0)
    x = eye - ld
    p = _dot(ld, ld)
    x = x + _dot(x, p)
    p = _dot(p, p)
    x = x + _dot(x, p)
    s = base
    while s < HEAD_DIM:
        off = jnp.where((row // (2 * s) == col // (2 * s)) & (row // s != col // s), lmat, 0.0)
        x = x - _dot(_dot(x, off), x)
        s *= 2
    return x


def _gdn_kernel(aqkv_ref, az_ref, small_ref, buf_ref, s0_ref, cw_ref, nw_ref,
                o_ref, sout_ref,
                s_ref, prev_ref, qs_ref, qg_ref, k_ref, kb_ref, kbg_ref, vb_ref, gc_ref, oacc_ref,
                *, n_pairs, valid_len):
    t = pl.program_id(1)
    nt = pl.num_programs(1)
    ts = aqkv_ref.shape[1]
    w_a = n_pairs * PAIR
    c = GDN_CHUNK

    @pl.when(t == 0)
    def _():
        for j in range(n_pairs):
            s_a = s0_ref[0, 2 * j]
            s_b = s0_ref[0, 2 * j + 1]
            top = jnp.concatenate([s_a, jnp.zeros_like(s_a)], axis=1)
            bot = jnp.concatenate([jnp.zeros_like(s_b), s_b], axis=1)
            s_ref[j] = jnp.concatenate([top, bot], axis=0)
        prev_ref[...] = buf_ref[0]

    x = aqkv_ref[0]
    xp = jnp.concatenate([prev_ref[...], x], axis=0)
    kw = cw_ref.shape[0]
    conv = cw_ref[0:1, :] * xp[SUBLANES - kw + 1:SUBLANES - kw + 1 + ts]
    for i in range(1, kw):
        conv = conv + cw_ref[i:i + 1, :] * xp[SUBLANES - kw + 1 + i:SUBLANES - kw + 1 + i + ts]
    prev_ref[...] = x[ts - SUBLANES:ts]
    a = _silu(conv)
    rowid = _iota((ts, 1), 0) + t * ts
    valid = (rowid < valid_len).astype(F32)
    ones_h = _head_ones(w_a)
    q = a[:, 0:w_a]
    k = a[:, w_a:2 * w_a]
    v = a[:, 2 * w_a:3 * w_a] * valid
    qn = q * lax.rsqrt(_dot2(q * q, ones_h) + EPS) * valid
    kn = k * lax.rsqrt(_dot2(k * k, ones_h) + EPS) * valid

    sm = small_ref[0] * valid
    row, col = _iota((ts, ts), 0), _iota((ts, ts), 1)
    tri = jnp.where((col <= row) & (row // c == col // c), 1.0, 0.0).astype(F32)
    gcum = jnp.dot(tri, sm, precision=HIGHEST, preferred_element_type=F32)
    gc_ref[...] = gcum
    lane, hrow = _iota((LANES, w_a), 1), _iota((LANES, w_a), 0)
    exp_b = (hrow == lane // HEAD_DIM + SM_BETA).astype(F32)
    exp_g = (hrow == lane // HEAD_DIM + SM_G).astype(F32)
    beta_e = _dot2(sm, exp_b)
    gcum_e = jnp.dot(gcum, exp_g, precision=HIGHEST, preferred_element_type=F32)
    eg = jnp.exp(gcum_e)
    scale = HEAD_DIM ** -0.5
    qs_ref[...] = qn * scale
    qg_ref[...] = qn * (scale * eg)
    k_ref[...] = kn
    kb = kn * beta_e
    kb_ref[...] = kb
    kbg_ref[...] = kb * eg
    vb_ref[...] = v * beta_e

    r2, c2 = _iota((PAIR, PAIR), 0), _iota((PAIR, PAIR), 1)
    same_head = (r2 // HEAD_DIM) == (c2 // HEAD_DIM)
    strict = same_head & (c2 < r2)
    rw, cw_ = _iota((c, PAIR), 0), _iota((c, PAIR), 1)
    incl_wide = (cw_ % HEAD_DIM) <= rw
    lane_lo = _iota((c, PAIR), 1) < HEAD_DIM

    def chunk(n, carry):
        r0 = pl.multiple_of(n * c, c)
        g_c = gc_ref[pl.ds(r0, c), :]
        g_t = g_c.T
        for j in range(n_pairs):
            ls = slice(j * PAIR, (j + 1) * PAIR)
            ha, hb = 2 * j, 2 * j + 1
            k_p = k_ref[pl.ds(r0, c), ls]
            kb_p = kb_ref[pl.ds(r0, c), ls]
            kbg_p = kbg_ref[pl.ds(r0, c), ls]
            vb_p = vb_ref[pl.ds(r0, c), ls]
            qs_p = qs_ref[pl.ds(r0, c), ls]
            qg_p = qg_ref[pl.ds(r0, c), ls]
            ga_col = g_c[:, SM_G + ha:SM_G + ha + 1]
            gb_col = g_c[:, SM_G + hb:SM_G + hb + 1]
            g_row = jnp.concatenate([g_t[SM_G + ha:SM_G + ha + 1, :], g_t[SM_G + hb:SM_G + hb + 1, :]], axis=1)
            g_col2 = jnp.concatenate([ga_col, gb_col], axis=0)
            g_wide = jnp.where(lane_lo, ga_col, gb_col)

            def stack(m):
                return jnp.concatenate([jnp.where(lane_lo, m, 0.0), jnp.where(lane_lo, 0.0, m)], axis=0)

            k2 = stack(k_p)
            kb2 = stack(kb_p)
            decay2 = jnp.exp(jnp.minimum(g_col2 - g_row, 0.0))
            lmat = jnp.where(strict, _dot_nt(kb2, k2) * decay2, 0.0)
            tinv = _unit_lower_inverse(lmat)
            rhs = jnp.concatenate([stack(vb_p), stack(kbg_p)], axis=1)
            sol = _dot(tinv, rhs)
            u2, w2 = sol[:, :PAIR], sol[:, PAIR:]
            decay_w = jnp.exp(jnp.minimum(g_wide - g_row, 0.0))
            attn_w = jnp.where(incl_wide, _dot_nt(qs_p, k2) * decay_w, 0.0)

            s2 = s_ref[j]
            vnew2 = u2 - _dot(w2, s2)
            o_p = _dot(qg_p, s2) + _dot(attn_w, vnew2)
            oacc_ref[pl.ds(r0, c), ls] = o_p
            g_last = jnp.where(lane_lo, ga_col[c - 1:c, :], gb_col[c - 1:c, :])
            kdec = k_p * jnp.exp(g_last - g_wide)
            vnat = vnew2[:c, :] + vnew2[c:, :]
            g_last_col = jnp.where(_iota((PAIR, 1), 0) < HEAD_DIM, ga_col[c - 1:c, :], gb_col[c - 1:c, :])
            s_new = s2 * jnp.exp(g_last_col) + jnp.where(same_head, _dot_tn(kdec, vnat), 0.0)
            s_ref[j] = s_new
        return carry

    lax.fori_loop(0, ts // c, chunk, 0)

    o = oacc_ref[...]
    on = o * lax.rsqrt(_dot2(o * o, ones_h) * (1.0 / HEAD_DIM) + EPS) * nw_ref[...]
    o_ref[0] = on * _silu(az_ref[0])

    @pl.when(t == nt - 1)
    def _():
        for j in range(n_pairs):
            s2 = s_ref[j]
            sout_ref[0, 2 * j] = s2[:HEAD_DIM, :HEAD_DIM]
            sout_ref[0, 2 * j + 1] = s2[HEAD_DIM:, HEAD_DIM:]


def _gdn(aqkv, az, small, bufpad, s0, conv_w, norm_w_tiled, valid_len, ts):
    b, l, w3 = aqkv.shape
    w_a = w3 // 3
    n_pairs = w_a // PAIR
    heads = s0.shape[1]
    assert l % ts == 0 and ts % GDN_CHUNK == 0 and heads == 2 * n_pairs
    scratch = [pltpu.VMEM((n_pairs, PAIR, PAIR), F32), pltpu.VMEM((SUBLANES, w3), F32)]
    scratch += [pltpu.VMEM((ts, w_a), F32) for _ in range(6)]
    scratch += [pltpu.VMEM((ts, LANES), F32), pltpu.VMEM((ts, w_a), F32)]
    return pl.pallas_call(
        functools.partial(_gdn_kernel, n_pairs=n_pairs, valid_len=valid_len),
        grid=(b, l // ts),
        in_specs=[pl.BlockSpec((1, ts, w3), lambda i, t: (i, t, 0)),
                  pl.BlockSpec((1, ts, w_a), lambda i, t: (i, t, 0)),
                  pl.BlockSpec((1, ts, LANES), lambda i, t: (i, t, 0)),
                  pl.BlockSpec((1, SUBLANES, w3), lambda i, t: (i, 0, 0)),
                  pl.BlockSpec((1, heads, HEAD_DIM, HEAD_DIM), lambda i, t: (i, 0, 0, 0)),
                  pl.BlockSpec(conv_w.shape, lambda i, t: (0, 0)),
                  pl.BlockSpec(norm_w_tiled.shape, lambda i, t: (0, 0))],
        out_specs=[pl.BlockSpec((1, ts, w_a), lambda i, t: (i, t, 0)),
                   pl.BlockSpec((1, heads, HEAD_DIM, HEAD_DIM), lambda i, t: (i, 0, 0, 0))],
        out_shape=[jax.ShapeDtypeStruct((b, l, w_a), F32),
                   jax.ShapeDtypeStruct((b, heads, HEAD_DIM, HEAD_DIM), F32)],
        scratch_shapes=scratch,
        compiler_params=_cparams(("arbitrary", "arbitrary")),
        name="gdn",
    )(aqkv, az, small, bufpad, s0, conv_w, norm_w_tiled)


def _fox_prompt_kernel(q_ref, k_ref, v_ref, cumt_ref, nw_ref, o_ref, *, n_pairs, tk):
    i = pl.program_id(1)
    tq = q_ref.shape[1]
    scale = HEAD_DIM ** -0.5
    lane_lo = _iota((tq, PAIR), 1) < HEAD_DIM
    qpos = i * tq + _iota((tq, tk), 0)
    kidx = _iota((tq, tk), 1)
    n_kb = ((i + 1) * tq + tk - 1) // tk
    q0 = pl.multiple_of(i * tq, tq)
    pairs = []
    for j in range(n_pairs):
        ls = slice(j * PAIR, (j + 1) * PAIR)
        q_p = q_ref[0, :, ls] * scale
        outs = []
        for e in range(2):
            h = 2 * j + e
            q_h = jnp.where(lane_lo if e == 0 else jnp.logical_not(lane_lo), q_p, 0.0).astype(BF16)
            c_ref = cumt_ref[h:h + 1, pl.ds(q0, LANES)][:, 0:1]

            def body(kb, carry, q_h=q_h, c_ref=c_ref, h=h, ls=ls):
                m, l, acc = carry
                k0 = pl.multiple_of(kb * tk, tk)
                k_blk = k_ref[0, pl.ds(k0, tk), ls].astype(BF16)
                v_blk = v_ref[0, pl.ds(k0, tk), ls].astype(BF16)
                s = lax.dot_general(q_h, k_blk, (((1,), (1,)), ((), ())), preferred_element_type=F32)
                s = s + (c_ref - cumt_ref[h:h + 1, pl.ds(k0, tk)])
                s = jnp.where(k0 + kidx <= qpos, s, NEG_INF)
                m_new = jnp.maximum(m, jnp.max(s, axis=-1, keepdims=True))
                p = jnp.exp(s - m_new)
                alpha = jnp.exp(m - m_new)
                l = alpha * l + jnp.sum(p, axis=-1, keepdims=True)
                acc = alpha * acc + jnp.dot(p.astype(BF16), v_blk, preferred_element_type=F32)
                return m_new, l, acc

            init = (jnp.full((tq, 1), NEG_INF, F32), jnp.zeros((tq, 1), F32), jnp.zeros((tq, PAIR), F32))
            m, l, acc = lax.fori_loop(0, n_kb, body, init)
            outs.append(acc / l)
        pairs.append(jnp.where(lane_lo, outs[0], outs[1]))
    o = jnp.concatenate(pairs, axis=1)
    ones_h = _head_ones(o.shape[1])
    o_ref[0] = o * lax.rsqrt(_dot2(o * o, ones_h) * (1.0 / HEAD_DIM) + EPS) * nw_ref[...]


def _fox_prompt(q, k, v, cumt, norm_w_tiled, tq, tk):
    b, l, w_b = q.shape
    assert l % tq == 0 and l % tk == 0 and tq % LANES == 0
    return pl.pallas_call(
        functools.partial(_fox_prompt_kernel, n_pairs=w_b // PAIR, tk=tk),
        grid=(b, l // tq),
        in_specs=[pl.BlockSpec((1, tq, w_b), lambda i, t: (i, t, 0)),
                  pl.BlockSpec((1, l, w_b), lambda i, t: (i, 0, 0)),
                  pl.BlockSpec((1, l, w_b), lambda i, t: (i, 0, 0)),
                  pl.BlockSpec((SUBLANES, l), lambda i, t: (0, i)),
                  pl.BlockSpec(norm_w_tiled.shape, lambda i, t: (0, 0))],
        out_specs=pl.BlockSpec((1, tq, w_b), lambda i, t: (i, t, 0)),
        out_shape=jax.ShapeDtypeStruct((b, l, w_b), F32),
        compiler_params=_cparams(("arbitrary", "arbitrary")),
        name="fox_prompt",
    )(q, k, v, cumt, norm_w_tiled)


def _page_suffix_kernel(x_ref, m_ref, o_ref):
    x = x_ref[...]
    x1 = x.astype(BF16)
    r1 = x - x1.astype(F32)
    x2 = r1.astype(BF16)
    x3 = (r1 - x2.astype(F32)).astype(BF16)
    m = m_ref[...]
    o_ref[...] = (jnp.dot(x1, m, preferred_element_type=F32) + jnp.dot(x2, m, preferred_element_type=F32)
                  + jnp.dot(x3, m, preferred_element_type=F32))


def _page_suffix(logf_pages, heads, tr):
    r, pw = logf_pages.shape
    page = pw // heads
    assert page == LANES and r % tr == 0
    src = jnp.arange(pw)
    dst = jnp.arange(SUBLANES * page)
    m = ((src[:, None] % heads == dst[None, :] // page) & (src[:, None] // heads >= dst[None, :] % page)).astype(BF16)
    return pl.pallas_call(
        _page_suffix_kernel,
        grid=(r // tr,),
        in_specs=[pl.BlockSpec((tr, pw), lambda i: (i, 0)),
                  pl.BlockSpec(m.shape, lambda i: (0, 0))],
        out_specs=pl.BlockSpec((tr, SUBLANES * page), lambda i: (i, 0)),
        out_shape=jax.ShapeDtypeStruct((r, SUBLANES * page), F32),
        compiler_params=_cparams(("arbitrary",)),
        name="page_suffix",
    )(logf_pages, m)


def _fox_decode_kernel(pt_ref, q_ref, kn_ref, vn_ref, cn_ref, nw_ref, *refs, n_heads, n_new, pps):
    k_refs = refs[0:pps]
    v_refs = refs[pps:2 * pps]
    s_refs = refs[2 * pps:3 * pps]
    o_ref = refs[3 * pps]
    qbd_ref, m_ref, l_ref, acc_ref, carry_ref = refs[3 * pps + 1:]
    del pt_ref
    j = pl.program_id(1)
    nj = pl.num_programs(1)
    n_tok = q_ref.shape[1]
    w_b = q_ref.shape[2]
    rows = n_tok * SUBLANES
    page = k_refs[0].shape[2]
    scale = HEAD_DIM ** -0.5

    @pl.when(j == 0)
    def _():
        q = q_ref[0] * scale
        headmask = (_iota((SUBLANES, w_b), 1) // HEAD_DIM == _iota((SUBLANES, w_b), 0)).astype(F32)
        qbd = jnp.concatenate([q[t:t + 1, :] * headmask for t in range(n_tok)], axis=0)
        qbd_ref[...] = qbd.astype(BF16)
        kn = kn_ref[0]
        s = _dot_nt(qbd, kn)
        cn = cn_ref[0]
        s = s - jnp.concatenate([cn] * n_tok, axis=0)
        kidx = _iota((rows, SUBLANES), 1)
        tidx = _iota((rows, SUBLANES), 0) // SUBLANES
        s = jnp.where((kidx <= tidx) & (kidx < n_new), s, NEG_INF)
        m = jnp.max(s, axis=-1, keepdims=True)
        p = jnp.exp(s - m)
        m_ref[...] = m
        l_ref[...] = jnp.sum(p, axis=-1, keepdims=True)
        acc_ref[...] = _dot(p, vn_ref[0])
        carry_ref[...] = jnp.zeros_like(carry_ref)

    lane = _iota((SUBLANES, page), 1)
    carry = carry_ref[...]
    biases = []
    for i in range(pps):
        incl = s_refs[i][0]
        excl = jnp.where(lane < page - 1, pltpu.roll(incl, page - 1, axis=1), 0.0)
        biases.append(excl + carry)
        carry = carry + incl[:, 0:1]
    carry_ref[...] = carry
    bias = jnp.concatenate(biases, axis=1)
    k_all = jnp.concatenate([k_refs[i][0, 0].astype(BF16) for i in range(pps)], axis=0)
    v_all = jnp.concatenate([v_refs[i][0, 0].astype(BF16) for i in range(pps)], axis=0)
    s = lax.dot_general(qbd_ref[...], k_all, (((1,), (1,)), ((), ())), preferred_element_type=F32)
    s = s + jnp.concatenate([bias] * n_tok, axis=0)
    m_old = m_ref[...]
    m_new = jnp.maximum(m_old, jnp.max(s, axis=-1, keepdims=True))
    p = jnp.exp(s - m_new)
    alpha = jnp.exp(m_old - m_new)
    m_ref[...] = m_new
    l_ref[...] = alpha * l_ref[...] + jnp.sum(p, axis=-1, keepdims=True)
    acc_ref[...] = alpha * acc_ref[...] + jnp.dot(p.astype(BF16), v_all, preferred_element_type=F32)

    @pl.when(j == nj - 1)
    def _():
        o = acc_ref[...] / l_ref[...]
        diag = (_iota((rows, w_b), 1) // HEAD_DIM) == (_iota((rows, w_b), 0) % SUBLANES)
        o = jnp.where(diag, o, 0.0)
        group = (_iota((SUBLANES, rows), 1) // SUBLANES == _iota((SUBLANES, rows), 0)).astype(F32)
        o_tok = jnp.dot(group, o, precision=HIGHEST, preferred_element_type=F32)
        ones_h = _head_ones(w_b)
        ss = jnp.dot(o_tok * o_tok, ones_h, precision=HIGHEST, preferred_element_type=F32)
        on = o_tok * lax.rsqrt(ss * (1.0 / HEAD_DIM) + EPS) * nw_ref[...]
        o_ref[0] = on[0:n_tok, :]
    del n_heads


def _fox_decode(layer, page_table, q, k_new8, v_new8, cn8, norm_w_tiled, cache_k, cache_v, suffix, n_new):
    b, n_tok, w_b = q.shape
    n_phys, page = cache_k.shape[1], cache_k.shape[2]
    n_pages = page_table.shape[1]
    pps = FOX_PAGES_PER_STEP
    assert n_pages % pps == 0
    rows = n_tok * SUBLANES

    def kv_map(i):
        return lambda bb, j, pt: (layer, pt[bb, n_pages - 1 - (j * pps + i)], 0, 0)

    def sfx_map(i):
        return lambda bb, j, pt: (layer * n_phys + pt[bb, n_pages - 1 - (j * pps + i)], 0, 0)

    in_specs = [pl.BlockSpec((1, n_tok, w_b), lambda bb, j, pt: (bb, 0, 0)),
                pl.BlockSpec((1, SUBLANES, w_b), lambda bb, j, pt: (bb, 0, 0)),
                pl.BlockSpec((1, SUBLANES, w_b), lambda bb, j, pt: (bb, 0, 0)),
                pl.BlockSpec((1, SUBLANES, SUBLANES), lambda bb, j, pt: (bb, 0, 0)),
                pl.BlockSpec(norm_w_tiled.shape, lambda bb, j, pt: (0, 0))]
    in_specs += [pl.BlockSpec((1, 1, page, w_b), kv_map(i)) for i in range(pps)]
    in_specs += [pl.BlockSpec((1, 1, page, w_b), kv_map(i)) for i in range(pps)]
    in_specs += [pl.BlockSpec((1, SUBLANES, page), sfx_map(i)) for i in range(pps)]
    grid_spec = pltpu.PrefetchScalarGridSpec(
        num_scalar_prefetch=1,
        grid=(b, n_pages // pps),
        in_specs=in_specs,
        out_specs=pl.BlockSpec((1, n_tok, w_b), lambda bb, j, pt: (bb, 0, 0)),
        scratch_shapes=[pltpu.VMEM((rows, w_b), BF16), pltpu.VMEM((rows, 1), F32), pltpu.VMEM((rows, 1), F32),
                        pltpu.VMEM((rows, w_b), F32), pltpu.VMEM((SUBLANES, page), F32)])
    return pl.pallas_call(
        functools.partial(_fox_decode_kernel, n_heads=w_b // HEAD_DIM, n_new=n_new, pps=pps),
        grid_spec=grid_spec,
        out_shape=jax.ShapeDtypeStruct((b, n_tok, w_b), F32),
        compiler_params=_cparams(("arbitrary", "arbitrary")),
        name="fox_decode",
    )(page_table, q, k_new8, v_new8, cn8, norm_w_tiled,
      *([cache_k] * pps), *([cache_v] * pps), *([suffix] * pps))


def _rms(x, w):
    return x * lax.rsqrt(jnp.mean(x * x, axis=-1, keepdims=True) + EPS) * w


def _gelu_tanh(x):
    return x * (0.5 * (1.0 + jnp.tanh(math.sqrt(2.0 / math.pi) * (x + 0.044715 * (x * x * x)))))


def _mix_ffn_kernel(x_ref, oa_ref, ob_ref, cuv_ref, cvw_ref, wm_ref, bse_ref, wo_ref, nf_ref,
                    wg_ref, wu_ref, cw_ref, wd_ref, pm_ref, nfin_ref,
                    y_ref, tail_ref, cv_ref, carry_ref,
                    *, seq_len, dff_chunk, final_norm, multi_seq):
    t = pl.program_id(1)
    tm = x_ref.shape[0]
    w_a, w_b = oa_ref.shape[1], ob_ref.shape[1]
    w_c = cuv_ref.shape[1] // 2
    d_ff = wg_ref.shape[1]
    tail_rows = tail_ref.shape[1]

    gel = _gelu_tanh(cuv_ref[...])
    cu, cv = gel[:, :w_c], gel[:, w_c:]
    ones_h = _head_ones(w_c)
    cvn = cv * lax.rsqrt(_dot2(cv * cv, ones_h) * (1.0 / HEAD_DIM) + EPS) * cvw_ref[...]
    cv_ref[...] = cvn
    cvn_b = cvn.astype(BF16)
    lane_lo = _iota((LANES, PAIR), 1) < HEAD_DIM
    row_blocks = []
    for n in range(tm // LANES):
        rs = slice(n * LANES, (n + 1) * LANES)
        pair_blocks = []
        for j in range(w_c // PAIR):
            ls = slice(j * PAIR, (j + 1) * PAIR)
            v_p = cvn_b[rs, ls]
            mix_a = jnp.dot(wm_ref[2 * j], v_p, preferred_element_type=F32)
            mix_b = jnp.dot(wm_ref[2 * j + 1], v_p, preferred_element_type=F32)
            pair_blocks.append(cu[rs, ls] * (jnp.where(lane_lo, mix_a, mix_b) + bse_ref[:, ls]))
        row_blocks.append(jnp.concatenate(pair_blocks, axis=1))
    o_c = jnp.concatenate(row_blocks, axis=0)

    x1 = (x_ref[...]
          + jnp.dot(oa_ref[...].astype(BF16), wo_ref[0:w_a, :], preferred_element_type=F32)
          + jnp.dot(ob_ref[...].astype(BF16), wo_ref[w_a:w_a + w_b, :], preferred_element_type=F32)
          + jnp.dot(o_c.astype(BF16), wo_ref[w_a + w_b:, :], preferred_element_type=F32))
    h2 = _rms(x1, nf_ref[...]).astype(BF16)

    row = _iota((tm, 1), 0)
    if multi_seq:
        pos = row % seq_len
    else:
        pos = row

        @pl.when(t == 0)
        def _():
            carry_ref[...] = pm_ref[0]

    acc = jnp.zeros((tm, x_ref.shape[1]), F32)
    for c0 in range(0, d_ff, dff_chunk):
        cs = slice(c0, c0 + dff_chunk)
        gp = jnp.dot(h2, wg_ref[:, cs], preferred_element_type=F32)
        up = jnp.dot(h2, wu_ref[:, cs], preferred_element_type=F32)
        if multi_seq:
            first1, first2 = pm_ref[0, :, cs], pm_ref[1, :, cs]
        else:
            c6, c7 = carry_ref[SUBLANES - 2:SUBLANES - 1, cs], carry_ref[SUBLANES - 1:SUBLANES, cs]
            first1 = c7
            first2 = jnp.where(row == 0, c6, c7)
            carry_ref[:, cs] = gp[tm - SUBLANES:tm, :]
        gm1 = jnp.where(pos >= 1, pltpu.roll(gp, 1, axis=0), first1)
        gm2 = jnp.where(pos >= 2, pltpu.roll(gp, 2, axis=0), first2)
        gate = cw_ref[0:1, cs] * gm2 + cw_ref[1:2, cs] * gm1 + cw_ref[2:3, cs] * gp
        act = (_silu(gate) * up).astype(BF16)
        acc = acc + jnp.dot(act, wd_ref[cs, :], preferred_element_type=F32)
        tail_ref[0, :, cs] = gp[tm - tail_rows:tm, :]
    x2 = x1 + acc
    if final_norm:
        x2 = _rms(x2, nfin_ref[...])
    y_ref[...] = x2


def _mix_ffn(x, oa, ob, cuv, cvw, wm, bse, wo, nf, wg, wu, cw, wd, pm, nfin, seq_len, tm, final_norm):
    n, d = x.shape
    multi_seq = seq_len < tm
    assert n % tm == 0 and (tm % seq_len == 0 if multi_seq else seq_len % tm == 0)
    assert tm % LANES == 0 and wm.shape[1] == LANES
    n_seq = 1 if multi_seq else n // seq_len
    seq_tiles = 1 if multi_seq else seq_len // tm
    n_tiles = n // tm
    d_ff = wg.shape[1]
    dff_chunk = 256
    assert d_ff % dff_chunk == 0
    tail_rows = tm if multi_seq else SUBLANES
    w_c = cuv.shape[1] // 2

    def rows(w):
        return pl.BlockSpec((tm, w), lambda s, t: (s * seq_tiles + t, 0))

    def const(a):
        nd = a.ndim
        return pl.BlockSpec(a.shape, lambda s, t: (0,) * nd, pipeline_mode=pl.Buffered(1))

    if multi_seq:
        pm_spec = pl.BlockSpec((2, tm, d_ff), lambda s, t: (0, s * seq_tiles + t, 0))
        grid = (n_tiles, 1)
    else:
        pm_spec = pl.BlockSpec((1, SUBLANES, d_ff), lambda s, t: (s, 0, 0))
        grid = (n_seq, seq_tiles)
    n_tail = n_tiles if multi_seq else n_seq
    return pl.pallas_call(
        functools.partial(_mix_ffn_kernel, seq_len=seq_len, dff_chunk=dff_chunk, final_norm=final_norm,
                          multi_seq=multi_seq),
        grid=grid,
        in_specs=[rows(d), rows(oa.shape[1]), rows(ob.shape[1]), rows(cuv.shape[1]),
                  const(cvw), const(wm), const(bse), const(wo), const(nf), const(wg), const(wu), const(cw),
                  const(wd), pm_spec, const(nfin)],
        out_specs=[rows(d),
                   pl.BlockSpec((1, tail_rows, d_ff), lambda s, t: (s, 0, 0)),
                   rows(w_c)],
        out_shape=[jax.ShapeDtypeStruct((n, d), F32),
                   jax.ShapeDtypeStruct((n_tail, tail_rows, d_ff), F32),
                   jax.ShapeDtypeStruct((n, w_c), F32)],
        scratch_shapes=[pltpu.VMEM((SUBLANES, d_ff), F32)],
        compiler_params=_cparams(("arbitrary", "arbitrary")),
        name="mix_ffn",
    )(x, oa, ob, cuv, cvw, wm, bse, wo, nf, wg, wu, cw, wd, pm, nfin)


def _pad_axis(a, axis, before, after):
    pads = [(0, 0)] * a.ndim
    pads[axis] = (before, after)
    return jnp.pad(a, pads)


def _lanes_block(parts):
    lead = parts[0][1].shape[:-1]
    out = jnp.zeros(lead + (LANES,), F32)
    for off, a in parts:
        out = lax.dynamic_update_slice_in_dim(out, a.astype(F32), off, axis=a.ndim - 1)
    return out


def kernel(x_prompt, x_sample, cache_fox_k, cache_fox_v, cache_fox_logf, page_table, state_gdn, state_gdn_conv,
           state_ffn_conv, norm_mix, w_in, gdn_conv_w, gdn_a_log, gdn_dt_bias, gdn_norm_w, fox_f_bias,
           fox_out_norm, chunk_v_norm, chunk_ws, chunk_bs, w_out, norm_ffn, ffn_w_gate, ffn_w_up, ffn_conv_w,
           ffn_w_down, norm_final):
    depth, d_model, _ = w_in.shape
    b_p, seq, _ = x_prompt.shape
    b_s, t_s, _ = x_sample.shape
    gh, fh, cg = gdn_a_log.shape[1], fox_f_bias.shape[1], chunk_ws.shape[1]
    assert gdn_norm_w.shape[1] == HEAD_DIM and gh % 2 == 0 and fh % 2 == 0 and cg % 2 == 0
    w_a, w_b, w_c = gh * HEAD_DIM, fh * HEAD_DIM, cg * HEAD_DIM
    d_ff = ffn_w_gate.shape[2]
    conv_k = gdn_conv_w.shape[1]
    ffn_k = ffn_conv_w.shape[1]
    chunk_len = chunk_ws.shape[2]
    n_phys, page = cache_fox_k.shape[1], cache_fox_k.shape[2]
    assert gh <= SUBLANES and fh <= SUBLANES and conv_k - 1 <= t_s <= GDN_CHUNK and ffn_k == 3
    assert chunk_len == LANES and seq % chunk_len == 0 and t_s <= chunk_len

    sizes = [3 * w_a, w_a, gh, gh, 3 * w_b, fh, 2 * w_c]
    starts = [0]
    for s in sizes:
        starts.append(starts[-1] + s)
    i_aqkv, i_az, i_beta, i_alpha, i_bqkv, i_bf, i_cuv = starts[:-1]
    offs, pos = {}, 0
    for name, width in [('a_qkv', 3 * w_a), ('a_z', w_a), ('b_q', w_b), ('b_k', w_b), ('b_v', w_b),
                        ('c_uv', 2 * w_c), ('small', LANES)]:
        offs[name] = (pos, pos + width)
        pos += width

    xp = x_prompt.reshape(b_p * seq, d_model)
    xs = x_sample.reshape(b_s * t_s, d_model)
    cache_k = cache_fox_k.reshape(depth, n_phys, page, w_b)
    cache_v = cache_fox_v.reshape(depth, n_phys, page, w_b)
    suffix = _page_suffix(cache_fox_logf.reshape(depth * n_phys, page * fh), fh, math.gcd(depth * n_phys, 1024))
    suffix = suffix.reshape(depth * n_phys, SUBLANES, page)

    tri = jnp.tril(jnp.ones((chunk_len, chunk_len), bool))
    tri_s = jnp.tril(jnp.ones((t_s, t_s), bool))
    n_rep = chunk_len // t_s
    assert (b_s * t_s) % chunk_len == 0 and chunk_len % t_s == 0

    prompt_rows, sample_rows = [], []
    for l in range(depth):
        w = w_in[l]
        small_w = _lanes_block([(SM_BETA, w[:, i_beta:i_beta + gh]), (SM_G, w[:, i_alpha:i_alpha + gh]),
                                (SM_F, w[:, i_bf:i_bf + fh])])
        w_r = jnp.concatenate([w[:, i_aqkv:i_aqkv + 3 * w_a], w[:, i_az:i_az + w_a],
                               w[:, i_bqkv:i_bqkv + 3 * w_b], w[:, i_cuv:i_cuv + 2 * w_c], small_w],
                              axis=1).astype(BF16)
        par = jnp.concatenate([
            _lanes_block([(SM_G, gdn_dt_bias[l][None, :]), (SM_F, fox_f_bias[l][None, :])]),
            _lanes_block([(SM_G, gdn_a_log[l][None, :])]),
            jnp.zeros((SUBLANES - 2, LANES), F32)], axis=0)
        nm = norm_mix[l][None, :]
        gnw = jnp.tile(gdn_norm_w[l], gh)[None, :]
        fnw = jnp.tile(fox_out_norm[l], fh)[None, :]
        cvw = jnp.tile(chunk_v_norm[l], cg)[None, :]
        wo = w_out[l].astype(BF16)
        wg, wu, wd = ffn_w_gate[l].astype(BF16), ffn_w_up[l].astype(BF16), ffn_w_down[l].astype(BF16)
        nf = norm_ffn[l][None, :]
        nfin = norm_final[None, :]
        last = l == depth - 1

        aqkv, az, bq, bk, bv, cuv, small, cumt = _inproj(xp, nm, w_r, par, offs, seq, 512)
        o_a, s_new = _gdn(aqkv.reshape(b_p, seq, 3 * w_a), az.reshape(b_p, seq, w_a),
                          small.reshape(b_p, seq, LANES), jnp.zeros((b_p, SUBLANES, 3 * w_a), F32),
                          jnp.zeros((b_p, gh, HEAD_DIM, HEAD_DIM), F32), gdn_conv_w[l], gnw, seq, 512)
        o_b = _fox_prompt(bq.reshape(b_p, seq, w_b), bk.reshape(b_p, seq, w_b), bv.reshape(b_p, seq, w_b),
                          cumt, fnw, 512, 512)
        wm = jnp.where(tri, chunk_ws[l], 0.0).astype(BF16)
        bse = jnp.repeat(chunk_bs[l].T, HEAD_DIM, axis=1)
        xp, tail, _ = _mix_ffn(xp, o_a.reshape(b_p * seq, w_a), o_b.reshape(b_p * seq, w_b), cuv, cvw, wm, bse,
                               wo, nf, wg, wu, ffn_conv_w[l], wd, jnp.zeros((b_p, SUBLANES, d_ff), F32), nfin,
                               seq, 512, last)
        prompt_rows.append((bk.reshape(b_p, seq, fh, HEAD_DIM), bv.reshape(b_p, seq, fh, HEAD_DIM),
                            small[:, SM_F:SM_F + fh].reshape(b_p, seq, fh), s_new,
                            aqkv.reshape(b_p, seq, 3 * w_a)[:, seq - (conv_k - 1):, :],
                            tail[:, SUBLANES - (ffn_k - 1):, :]))

        n_s = b_s * t_s
        aqkv, az, bq, bk, bv, cuv, small, cumt = _inproj(xs, nm, w_r, par, offs, t_s, n_s)
        pad_t = GDN_CHUNK - t_s
        o_a, s_new = _gdn(_pad_axis(aqkv.reshape(b_s, t_s, 3 * w_a), 1, 0, pad_t),
                          _pad_axis(az.reshape(b_s, t_s, w_a), 1, 0, pad_t),
                          _pad_axis(small.reshape(b_s, t_s, LANES), 1, 0, pad_t),
                          _pad_axis(state_gdn_conv[l], 1, SUBLANES - (conv_k - 1), 0),
                          state_gdn[l], gdn_conv_w[l], gnw, t_s, GDN_CHUNK)
        o_a = o_a[:, :t_s, :]
        cn8 = _pad_axis(_pad_axis(cumt.reshape(SUBLANES, b_s, t_s).transpose(1, 0, 2), 2, 0, SUBLANES - t_s),
                        1, 0, 0)
        o_b = _fox_decode(l, page_table, bq.reshape(b_s, t_s, w_b),
                          _pad_axis(bk.reshape(b_s, t_s, w_b), 1, 0, SUBLANES - t_s),
                          _pad_axis(bv.reshape(b_s, t_s, w_b), 1, 0, SUBLANES - t_s),
                          cn8, fnw, cache_k, cache_v, suffix, t_s)
        eye_rep = jnp.eye(n_rep, dtype=F32)
        wm_s = jnp.stack([jnp.kron(eye_rep, jnp.where(tri_s, chunk_ws[l, g, :t_s, :t_s], 0.0))
                          for g in range(cg)]).astype(BF16)
        bse_s = jnp.repeat(jnp.tile(chunk_bs[l, :, :t_s], (1, n_rep)).T, HEAD_DIM, axis=1)
        buf = state_ffn_conv[l]
        zrow = jnp.zeros((b_s, 1, d_ff), F32)
        pm1 = jnp.concatenate([buf[:, 1:2]] + [zrow] * (t_s - 1), axis=1).reshape(n_s, d_ff)
        pm2 = jnp.concatenate([buf[:, 0:1], buf[:, 1:2]] + [zrow] * (t_s - 2), axis=1).reshape(n_s, d_ff)
        xs, tail, cvn = _mix_ffn(xs, o_a.reshape(n_s, w_a), o_b.reshape(n_s, w_b), cuv, cvw, wm_s, bse_s,
                                 wo, nf, wg, wu, ffn_conv_w[l], wd, jnp.stack([pm1, pm2]), nfin,
                                 t_s, chunk_len, last)
        sample_rows.append((bk.reshape(b_s, t_s, fh, HEAD_DIM), bv.reshape(b_s, t_s, fh, HEAD_DIM),
                            small[:, SM_F:SM_F + fh].reshape(b_s, t_s, fh), s_new,
                            aqkv.reshape(b_s, t_s, 3 * w_a)[:, t_s - (conv_k - 1):, :],
                            tail.reshape(b_s, t_s, d_ff)[:, t_s - (ffn_k - 1):, :],
                            cvn.reshape(b_s, t_s, w_c)))

    def stack(rows_, i):
        return jnp.stack([r[i] for r in rows_])

    y_prompt = xp.reshape(b_p, seq, d_model)
    y_sample = xs.reshape(b_s, t_s, d_model)
    return (y_prompt, y_sample,
            stack(prompt_rows, 0), stack(prompt_rows, 1), stack(prompt_rows, 2), stack(prompt_rows, 3),
            stack(prompt_rows, 4), stack(prompt_rows, 5),
            stack(sample_rows, 0), stack(sample_rows, 1), stack(sample_rows, 2), stack(sample_rows, 3),
            stack(sample_rows, 4), stack(sample_rows, 5), stack(sample_rows, 6))
```

```python
import functools
import math

import jax
import jax.numpy as jnp
from jax import lax
from jax.experimental import pallas as pl
from jax.experimental.pallas import tpu as pltpu

F32 = jnp.float32
BF16 = jnp.bfloat16
EPS = 1e-6
LANES = 128
SUBLANES = 8
HEAD_DIM = 64
PAIR = 2 * HEAD_DIM
GDN_CHUNK = 64
FOX_PAGES_PER_STEP = 8
VMEM_LIMIT = 56 * 1024 * 1024
HIGHEST = lax.Precision.HIGHEST
NEG_INF = -1e30

SM_BETA, SM_G, SM_F = 0, 8, 16


def _cparams(sem):
    return pltpu.CompilerParams(dimension_semantics=sem, vmem_limit_bytes=VMEM_LIMIT)


def _dot(a, b):
    return jnp.dot(a.astype(BF16), b.astype(BF16), preferred_element_type=F32)


def _dot_nt(a, b):
    return lax.dot_general(a.astype(BF16), b.astype(BF16), (((1,), (1,)), ((), ())),
                           preferred_element_type=F32)


def _dot_tn(a, b):
    return lax.dot_general(a.astype(BF16), b.astype(BF16), (((0,), (0,)), ((), ())),
                           preferred_element_type=F32)


def _dot2(a, b01):
    hi = a.astype(BF16)
    lo = (a - hi.astype(F32)).astype(BF16)
    b = b01.astype(BF16)
    return (jnp.dot(hi, b, preferred_element_type=F32) + jnp.dot(lo, b, preferred_element_type=F32))


def _iota(shape, dim):
    return lax.broadcasted_iota(jnp.int32, shape, dim)


def _head_ones(n):
    return (_iota((n, n), 0) // HEAD_DIM == _iota((n, n), 1) // HEAD_DIM).astype(F32)


def _silu(x):
    return x * jax.nn.sigmoid(x)


def _inproj_kernel(x_ref, nw_ref, w_ref, par_ref,
                   aqkv_ref, az_ref, bq_ref, bk_ref, bv_ref, cuv_ref, small_ref, cumt_ref,
                   carry_ref, *, offs, seq_tiles, seq_len):
    i = pl.program_id(0)
    tm = x_ref.shape[0]
    x = x_ref[...]
    h = (x * lax.rsqrt(jnp.mean(x * x, axis=-1, keepdims=True) + EPS) * nw_ref[...]).astype(BF16)

    def mm(name):
        lo, hi = offs[name]
        return jnp.dot(h, w_ref[:, lo:hi], preferred_element_type=F32)

    aqkv_ref[...] = mm('a_qkv')
    az_ref[...] = mm('a_z')
    bq_ref[...] = mm('b_q')
    bk_ref[...] = mm('b_k')
    bv_ref[...] = mm('b_v')
    cuv_ref[...] = mm('c_uv')

    z = mm('small') + par_ref[0:1, :]
    lane = _iota(z.shape, 1)
    t = jnp.log1p(jnp.exp(-jnp.abs(z)))
    softplus = jnp.maximum(z, 0.0) + t
    log_sig = jnp.minimum(z, 0.0) - t
    sm = jnp.where(lane < SM_G, jax.nn.sigmoid(z),
                   jnp.where(lane < SM_F, -jnp.exp(par_ref[1:2, :]) * softplus, log_sig))
    small_ref[...] = sm

    lb = min(seq_len, tm)
    row, col = _iota((tm, tm), 0), _iota((tm, tm), 1)
    tri = jnp.where((col <= row) & (row // lb == col // lb), 1.0, 0.0).astype(F32)
    cum = jnp.dot(tri, sm, precision=HIGHEST, preferred_element_type=F32)
    if seq_tiles > 1:
        @pl.when(i % seq_tiles == 0)
        def _():
            carry_ref[...] = jnp.zeros_like(carry_ref)
        cum = cum + carry_ref[0:1, :]
        carry_ref[0:1, :] = cum[tm - 1:tm, :]
    cumt_ref[...] = cum.T[SM_F:SM_F + SUBLANES, :]


def _inproj(x, nw, w, par, offs, seq_len, tm):
    n, d = x.shape
    assert n % tm == 0 and (seq_len % tm == 0 or tm % seq_len == 0)
    seq_tiles = max(seq_len // tm, 1)
    widths = {k: hi - lo for k, (lo, hi) in offs.items()}
    names = ['a_qkv', 'a_z', 'b_q', 'b_k', 'b_v', 'c_uv', 'small']
    out_shape = [jax.ShapeDtypeStruct((n, widths[k]), F32) for k in names]
    out_specs = [pl.BlockSpec((tm, widths[k]), lambda i: (i, 0)) for k in names]
    out_shape.append(jax.ShapeDtypeStruct((SUBLANES, n), F32))
    out_specs.append(pl.BlockSpec((SUBLANES, tm), lambda i: (0, i)))
    return pl.pallas_call(
        functools.partial(_inproj_kernel, offs=offs, seq_tiles=seq_tiles, seq_len=seq_len),
        grid=(n // tm,),
        in_specs=[pl.BlockSpec((tm, d), lambda i: (i, 0)),
                  pl.BlockSpec((1, d), lambda i: (0, 0)),
                  pl.BlockSpec(w.shape, lambda i: (0, 0)),
                  pl.BlockSpec(par.shape, lambda i: (0, 0))],
        out_specs=out_specs,
        out_shape=out_shape,
        scratch_shapes=[pltpu.VMEM((SUBLANES, LANES), F32)],
        compiler_params=_cparams(("arbitrary",)),
        name="inproj",
    )(x, nw, w, par)


def _unit_lower_inverse(lmat):
    n = lmat.shape[0]
    row, col = _iota((n, n), 0), _iota((n, n), 1)
    eye = (row == col).astype(F32)
    base = SUBLANES
    ld = jnp.where(row // base == col // base, lmat, 0.0)
    x = eye - ld
    p = _dot(ld, ld)
    x = x + _dot(x, p)
    p = _dot(p, p)
    x = x + _dot(x, p)
    s = base
    while s < HEAD_DIM:
        off = jnp.where((row // (2 * s) == col // (2 * s)) & (row // s != col // s), lmat, 0.0)
        x = x - _dot(_dot(x, off), x)
        s *= 2
    return x


def _gdn_kernel(aqkv_ref, az_ref, small_ref, buf_ref, s0_ref, cw_ref, nw_ref,
                o_ref, sout_ref,
                s_ref, prev_ref, qs_ref, qg_ref, k_ref, kb_ref, kbg_ref, vb_ref, gc_ref, oacc_ref,
                *, n_pairs, valid_len):
    t = pl.program_id(1)
    nt = pl.num_programs(1)
    ts = aqkv_ref.shape[1]
    w_a = n_pairs * PAIR
    c = GDN_CHUNK

    @pl.when(t == 0)
    def _():
        for j in range(n_pairs):
            s_a = s0_ref[0, 2 * j]
            s_b = s0_ref[0, 2 * j + 1]
            top = jnp.concatenate([s_a, jnp.zeros_like(s_a)], axis=1)
            bot = jnp.concatenate([jnp.zeros_like(s_b), s_b], axis=1)
            s_ref[j] = jnp.concatenate([top, bot], axis=0)
        prev_ref[...] = buf_ref[0]

    x = aqkv_ref[0]
    xp = jnp.concatenate([prev_ref[...], x], axis=0)
    kw = cw_ref.shape[0]
    conv = cw_ref[0:1, :] * xp[SUBLANES - kw + 1:SUBLANES - kw + 1 + ts]
    for i in range(1, kw):
        conv = conv + cw_ref[i:i + 1, :] * xp[SUBLANES - kw + 1 + i:SUBLANES - kw + 1 + i + ts]
    prev_ref[...] = x[ts - SUBLANES:ts]
    a = _silu(conv)
    rowid = _iota((ts, 1), 0) + t * ts
    valid = (rowid < valid_len).astype(F32)
    ones_h = _head_ones(w_a)
    q = a[:, 0:w_a]
    k = a[:, w_a:2 * w_a]
    v = a[:, 2 * w_a:3 * w_a] * valid
    qn = q * lax.rsqrt(_dot2(q * q, ones_h) + EPS) * valid
    kn = k * lax.rsqrt(_dot2(k * k, ones_h) + EPS) * valid

    sm = small_ref[0] * valid
    row, col = _iota((ts, ts), 0), _iota((ts, ts), 1)
    tri = jnp.where((col <= row) & (row // c == col // c), 1.0, 0.0).astype(F32)
    gcum = jnp.dot(tri, sm, precision=HIGHEST, preferred_element_type=F32)
    gc_ref[...] = gcum
    lane, hrow = _iota((LANES, w_a), 1), _iota((LANES, w_a), 0)
    exp_b = (hrow == lane // HEAD_DIM + SM_BETA).astype(F32)
    exp_g = (hrow == lane // HEAD_DIM + SM_G).astype(F32)
    beta_e = _dot2(sm, exp_b)
    gcum_e = jnp.dot(gcum, exp_g, precision=HIGHEST, preferred_element_type=F32)
    eg = jnp.exp(gcum_e)
    scale = HEAD_DIM ** -0.5
    qs_ref[...] = qn * scale
    qg_ref[...] = qn * (scale * eg)
    k_ref[...] = kn
    kb = kn * beta_e
    kb_ref[...] = kb
    kbg_ref[...] = kb * eg
    vb_ref[...] = v * beta_e

    r2, c2 = _iota((PAIR, PAIR), 0), _iota((PAIR, PAIR), 1)
    same_head = (r2 // HEAD_DIM) == (c2 // HEAD_DIM)
    strict = same_head & (c2 < r2)
    rw, cw_ = _iota((c, PAIR), 0), _iota((c, PAIR), 1)
    incl_wide = (cw_ % HEAD_DIM) <= rw
    lane_lo = _iota((c, PAIR), 1) < HEAD_DIM

    def chunk(n, carry):
        r0 = pl.multiple_of(n * c, c)
        g_c = gc_ref[pl.ds(r0, c), :]
        g_t = g_c.T
        for j in range(n_pairs):
            ls = slice(j * PAIR, (j + 1) * PAIR)
            ha, hb = 2 * j, 2 * j + 1
            k_p = k_ref[pl.ds(r0, c), ls]
            kb_p = kb_ref[pl.ds(r0, c), ls]
            kbg_p = kbg_ref[pl.ds(r0, c), ls]
            vb_p = vb_ref[pl.ds(r0, c), ls]
            qs_p = qs_ref[pl.ds(r0, c), ls]
            qg_p = qg_ref[pl.ds(r0, c), ls]
            ga_col = g_c[:, SM_G + ha:SM_G + ha + 1]
            gb_col = g_c[:, SM_G + hb:SM_G + hb + 1]
            g_row = jnp.concatenate([g_t[SM_G + ha:SM_G + ha + 1, :], g_t[SM_G + hb:SM_G + hb + 1, :]], axis=1)
            g_col2 = jnp.concatenate([ga_col, gb_col], axis=0)
            g_wide = jnp.where(lane_lo, ga_col, gb_col)

            def stack(m):
                return jnp.concatenate([jnp.where(lane_lo, m, 0.0), jnp.where(lane_lo, 0.0, m)], axis=0)

            k2 = stack(k_p)
            kb2 = stack(kb_p)
            decay2 = jnp.exp(jnp.minimum(g_col2 - g_row, 0.0))
            lmat = jnp.where(strict, _dot_nt(kb2, k2) * decay2, 0.0)
            tinv = _unit_lower_inverse(lmat)
            rhs = jnp.concatenate([stack(vb_p), stack(kbg_p)], axis=1)
            sol = _dot(tinv, rhs)
            u2, w2 = sol[:, :PAIR], sol[:, PAIR:]
            decay_w = jnp.exp(jnp.minimum(g_wide - g_row, 0.0))
            attn_w = jnp.where(incl_wide, _dot_nt(qs_p, k2) * decay_w, 0.0)

            s2 = s_ref[j]
            vnew2 = u2 - _dot(w2, s2)
            o_p = _dot(qg_p, s2) + _dot(attn_w, vnew2)
            oacc_ref[pl.ds(r0, c), ls] = o_p
            g_last = jnp.where(lane_lo, ga_col[c - 1:c, :], gb_col[c - 1:c, :])
            kdec = k_p * jnp.exp(g_last - g_wide)
            vnat = vnew2[:c, :] + vnew2[c:, :]
            g_last_col = jnp.where(_iota((PAIR, 1), 0) < HEAD_DIM, ga_col[c - 1:c, :], gb_col[c - 1:c, :])
            s_new = s2 * jnp.exp(g_last_col) + jnp.where(same_head, _dot_tn(kdec, vnat), 0.0)
            s_ref[j] = s_new
        return carry

    lax.fori_loop(0, ts // c, chunk, 0)

    o = oacc_ref[...]
    on = o * lax.rsqrt(_dot2(o * o, ones_h) * (1.0 / HEAD_DIM) + EPS) * nw_ref[...]
    o_ref[0] = on * _silu(az_ref[0])

    @pl.when(t == nt - 1)
    def _():
        for j in range(n_pairs):
            s2 = s_ref[j]
            sout_ref[0, 2 * j] = s2[:HEAD_DIM, :HEAD_DIM]
            sout_ref[0, 2 * j + 1] = s2[HEAD_DIM:, HEAD_DIM:]


def _gdn(aqkv, az, small, bufpad, s0, conv_w, norm_w_tiled, valid_len, ts):
    b, l, w3 = aqkv.shape
    w_a = w3 // 3
    n_pairs = w_a // PAIR
    heads = s0.shape[1]
    assert l % ts == 0 and ts % GDN_CHUNK == 0 and heads == 2 * n_pairs
    scratch = [pltpu.VMEM((n_pairs, PAIR, PAIR), F32), pltpu.VMEM((SUBLANES, w3), F32)]
    scratch += [pltpu.VMEM((ts, w_a), F32) for _ in range(6)]
    scratch += [pltpu.VMEM((ts, LANES), F32), pltpu.VMEM((ts, w_a), F32)]
    return pl.pallas_call(
        functools.partial(_gdn_kernel, n_pairs=n_pairs, valid_len=valid_len),
        grid=(b, l // ts),
        in_specs=[pl.BlockSpec((1, ts, w3), lambda i, t: (i, t, 0)),
                  pl.BlockSpec((1, ts, w_a), lambda i, t: (i, t, 0)),
                  pl.BlockSpec((1, ts, LANES), lambda i, t: (i, t, 0)),
                  pl.BlockSpec((1, SUBLANES, w3), lambda i, t: (i, 0, 0)),
                  pl.BlockSpec((1, heads, HEAD_DIM, HEAD_DIM), lambda i, t: (i, 0, 0, 0)),
                  pl.BlockSpec(conv_w.shape, lambda i, t: (0, 0)),
                  pl.BlockSpec(norm_w_tiled.shape, lambda i, t: (0, 0))],
        out_specs=[pl.BlockSpec((1, ts, w_a), lambda i, t: (i, t, 0)),
                   pl.BlockSpec((1, heads, HEAD_DIM, HEAD_DIM), lambda i, t: (i, 0, 0, 0))],
        out_shape=[jax.ShapeDtypeStruct((b, l, w_a), F32),
                   jax.ShapeDtypeStruct((b, heads, HEAD_DIM, HEAD_DIM), F32)],
        scratch_shapes=scratch,
        compiler_params=_cparams(("arbitrary", "arbitrary")),
        name="gdn",
    )(aqkv, az, small, bufpad, s0, conv_w, norm_w_tiled)


def _fox_prompt_kernel(q_ref, k_ref, v_ref, cumt_ref, nw_ref, o_ref, *, n_pairs, tk):
    i = pl.program_id(1)
    tq = q_ref.shape[1]
    scale = HEAD_DIM ** -0.5
    lane_lo = _iota((tq, PAIR), 1) < HEAD_DIM
    qpos = i * tq + _iota((tq, tk), 0)
    kidx = _iota((tq, tk), 1)
    n_kb = ((i + 1) * tq + tk - 1) // tk
    q0 = pl.multiple_of(i * tq, tq)
    pairs = []
    for j in range(n_pairs):
        ls = slice(j * PAIR, (j + 1) * PAIR)
        q_p = q_ref[0, :, ls] * scale
        outs = []
        for e in range(2):
            h = 2 * j + e
            q_h = jnp.where(lane_lo if e == 0 else jnp.logical_not(lane_lo), q_p, 0.0).astype(BF16)
            c_ref = cumt_ref[h:h + 1, pl.ds(q0, LANES)][:, 0:1]

            def body(kb, carry, q_h=q_h, c_ref=c_ref, h=h, ls=ls):
                m, l, acc = carry
                k0 = pl.multiple_of(kb * tk, tk)
                k_blk = k_ref[0, pl.ds(k0, tk), ls].astype(BF16)
                v_blk = v_ref[0, pl.ds(k0, tk), ls].astype(BF16)
                s = lax.dot_general(q_h, k_blk, (((1,), (1,)), ((), ())), preferred_element_type=F32)
                s = s + (c_ref - cumt_ref[h:h + 1, pl.ds(k0, tk)])
                s = jnp.where(k0 + kidx <= qpos, s, NEG_INF)
                m_new = jnp.maximum(m, jnp.max(s, axis=-1, keepdims=True))
                p = jnp.exp(s - m_new)
                alpha = jnp.exp(m - m_new)
                l = alpha * l + jnp.sum(p, axis=-1, keepdims=True)
                acc = alpha * acc + jnp.dot(p.astype(BF16), v_blk, preferred_element_type=F32)
                return m_new, l, acc

            init = (jnp.full((tq, 1), NEG_INF, F32), jnp.zeros((tq, 1), F32), jnp.zeros((tq, PAIR), F32))
            m, l, acc = lax.fori_loop(0, n_kb, body, init)
            outs.append(acc / l)
        pairs.append(jnp.where(lane_lo, outs[0], outs[1]))
    o = jnp.concatenate(pairs, axis=1)
    ones_h = _head_ones(o.shape[1])
    o_ref[0] = o * lax.rsqrt(_dot2(o * o, ones_h) * (1.0 / HEAD_DIM) + EPS) * nw_ref[...]


def _fox_prompt(q, k, v, cumt, norm_w_tiled, tq, tk):
    b, l, w_b = q.shape
    assert l % tq == 0 and l % tk == 0 and tq % LANES == 0
    return pl.pallas_call(
        functools.partial(_fox_prompt_kernel, n_pairs=w_b // PAIR, tk=tk),
        grid=(b, l // tq),
        in_specs=[pl.BlockSpec((1, tq, w_b), lambda i, t: (i, t, 0)),
                  pl.BlockSpec((1, l, w_b), lambda i, t: (i, 0, 0)),
                  pl.BlockSpec((1, l, w_b), lambda i, t: (i, 0, 0)),
                  pl.BlockSpec((SUBLANES, l), lambda i, t: (0, i)),
                  pl.BlockSpec(norm_w_tiled.shape, lambda i, t: (0, 0))],
        out_specs=pl.BlockSpec((1, tq, w_b), lambda i, t: (i, t, 0)),
        out_shape=jax.ShapeDtypeStruct((b, l, w_b), F32),
        compiler_params=_cparams(("arbitrary", "arbitrary")),
        name="fox_prompt",
    )(q, k, v, cumt, norm_w_tiled)


def _page_suffix_kernel(x_ref, m_ref, o_ref):
    x = x_ref[...]
    x1 = x.astype(BF16)
    r1 = x - x1.astype(F32)
    x2 = r1.astype(BF16)
    x3 = (r1 - x2.astype(F32)).astype(BF16)
    m = m_ref[...]
    o_ref[...] = (jnp.dot(x1, m, preferred_element_type=F32) + jnp.dot(x2, m, preferred_element_type=F32)
                  + jnp.dot(x3, m, preferred_element_type=F32))


def _page_suffix(logf_rows, tr):
    r, page = logf_rows.shape
    assert page == LANES and r % tr == 0
    idx = jnp.arange(page)
    m = (idx[:, None] >= idx[None, :]).astype(BF16)
    return pl.pallas_call(
        _page_suffix_kernel,
        grid=(r // tr,),
        in_specs=[pl.BlockSpec((tr, page), lambda i: (i, 0)),
                  pl.BlockSpec(m.shape, lambda i: (0, 0))],
        out_specs=pl.BlockSpec((tr, page), lambda i: (i, 0)),
        out_shape=jax.ShapeDtypeStruct((r, page), F32),
        compiler_params=_cparams(("arbitrary",)),
        name="page_suffix",
    )(logf_rows, m)


def _fox_decode_kernel(pt_ref, q_ref, kn_ref, vn_ref, cn_ref, nw_ref, *refs, n_heads, n_new, pps):
    k_refs = refs[0:pps]
    v_refs = refs[pps:2 * pps]
    s_refs = refs[2 * pps:3 * pps]
    o_ref = refs[3 * pps]
    qbd_ref, m_ref, l_ref, acc_ref, carry_ref = refs[3 * pps + 1:]
    bb = pl.program_id(0)
    j = pl.program_id(1)
    nj = pl.num_programs(1)
    n_tok = q_ref.shape[1]
    w_b = q_ref.shape[2]
    rows = n_tok * SUBLANES
    page = k_refs[0].shape[3]
    n_pages = nj * pps
    scale = HEAD_DIM ** -0.5

    @pl.when(j == 0)
    def _():
        q = q_ref[0] * scale
        headmask = (_iota((SUBLANES, w_b), 1) // HEAD_DIM == _iota((SUBLANES, w_b), 0)).astype(F32)
        qbd = jnp.concatenate([q[t:t + 1, :] * headmask for t in range(n_tok)], axis=0)
        qbd_ref[...] = qbd.astype(BF16)
        kn = kn_ref[0]
        s = _dot_nt(qbd, kn)
        cn = cn_ref[0]
        s = s - jnp.concatenate([cn] * n_tok, axis=0)
        kidx = _iota((rows, SUBLANES), 1)
        tidx = _iota((rows, SUBLANES), 0) // SUBLANES
        s = jnp.where((kidx <= tidx) & (kidx < n_new), s, NEG_INF)
        m = jnp.max(s, axis=-1, keepdims=True)
        p = jnp.exp(s - m)
        m_ref[...] = m
        l_ref[...] = jnp.sum(p, axis=-1, keepdims=True)
        acc_ref[...] = _dot(p, vn_ref[0])
        carry_ref[...] = jnp.zeros_like(carry_ref)

    lane = _iota((SUBLANES, page), 1)
    carry = carry_ref[...]
    biases = []
    zrows = jnp.zeros((SUBLANES - n_heads, page), F32)
    for i in range(pps):
        sub = pt_ref[bb, n_pages - 1 - (j * pps + i)] % SUBLANES
        incl = jnp.concatenate([s_refs[i][0, h, pl.ds(sub, 1), :] for h in range(n_heads)] + [zrows], axis=0)
        excl = jnp.where(lane < page - 1, pltpu.roll(incl, page - 1, axis=1), 0.0)
        biases.append(excl + carry)
        carry = carry + incl[:, 0:1]
    carry_ref[...] = carry
    bias = jnp.concatenate(biases, axis=1)
    kt_all = jnp.concatenate([k_refs[i][0, 0].astype(BF16) for i in range(pps)], axis=1)
    vt_all = jnp.concatenate([v_refs[i][0, 0].astype(BF16) for i in range(pps)], axis=1)
    s = jnp.dot(qbd_ref[...], kt_all, preferred_element_type=F32)
    s = s + jnp.concatenate([bias] * n_tok, axis=0)
    m_old = m_ref[...]
    m_new = jnp.maximum(m_old, jnp.max(s, axis=-1, keepdims=True))
    p = jnp.exp(s - m_new)
    alpha = jnp.exp(m_old - m_new)
    m_ref[...] = m_new
    l_ref[...] = alpha * l_ref[...] + jnp.sum(p, axis=-1, keepdims=True)
    acc_ref[...] = alpha * acc_ref[...] + lax.dot_general(
        p.astype(BF16), vt_all, (((1,), (1,)), ((), ())), preferred_element_type=F32)

    @pl.when(j == nj - 1)
    def _():
        o = acc_ref[...] / l_ref[...]
        diag = (_iota((rows, w_b), 1) // HEAD_DIM) == (_iota((rows, w_b), 0) % SUBLANES)
        o = jnp.where(diag, o, 0.0)
        group = (_iota((SUBLANES, rows), 1) // SUBLANES == _iota((SUBLANES, rows), 0)).astype(F32)
        o_tok = jnp.dot(group, o, precision=HIGHEST, preferred_element_type=F32)
        ones_h = _head_ones(w_b)
        ss = jnp.dot(o_tok * o_tok, ones_h, precision=HIGHEST, preferred_element_type=F32)
        on = o_tok * lax.rsqrt(ss * (1.0 / HEAD_DIM) + EPS) * nw_ref[...]
        o_ref[0] = on[0:n_tok, :]


def _fox_decode(layer, page_table, q, k_new8, v_new8, cn8, norm_w_tiled, cache_kt, cache_vt, suffix, n_new):
    b, n_tok, w_b = q.shape
    n_phys, page = cache_kt.shape[1], cache_kt.shape[3]
    n_heads = suffix.shape[1]
    n_pages = page_table.shape[1]
    pps = FOX_PAGES_PER_STEP
    assert n_pages % pps == 0 and n_phys % SUBLANES == 0
    rows = n_tok * SUBLANES

    def kv_map(i):
        return lambda bb, j, pt: (layer, pt[bb, n_pages - 1 - (j * pps + i)], 0, 0)

    def sfx_map(i):
        return lambda bb, j, pt: (layer, 0, pt[bb, n_pages - 1 - (j * pps + i)] // SUBLANES, 0)

    in_specs = [pl.BlockSpec((1, n_tok, w_b), lambda bb, j, pt: (bb, 0, 0)),
                pl.BlockSpec((1, SUBLANES, w_b), lambda bb, j, pt: (bb, 0, 0)),
                pl.BlockSpec((1, SUBLANES, w_b), lambda bb, j, pt: (bb, 0, 0)),
                pl.BlockSpec((1, SUBLANES, SUBLANES), lambda bb, j, pt: (bb, 0, 0)),
                pl.BlockSpec(norm_w_tiled.shape, lambda bb, j, pt: (0, 0))]
    in_specs += [pl.BlockSpec((1, 1, w_b, page), kv_map(i)) for i in range(pps)]
    in_specs += [pl.BlockSpec((1, 1, w_b, page), kv_map(i)) for i in range(pps)]
    in_specs += [pl.BlockSpec((1, n_heads, SUBLANES, page), sfx_map(i)) for i in range(pps)]
    grid_spec = pltpu.PrefetchScalarGridSpec(
        num_scalar_prefetch=1,
        grid=(b, n_pages // pps),
        in_specs=in_specs,
        out_specs=pl.BlockSpec((1, n_tok, w_b), lambda bb, j, pt: (bb, 0, 0)),
        scratch_shapes=[pltpu.VMEM((rows, w_b), BF16), pltpu.VMEM((rows, 1), F32), pltpu.VMEM((rows, 1), F32),
                        pltpu.VMEM((rows, w_b), F32), pltpu.VMEM((SUBLANES, page), F32)])
    return pl.pallas_call(
        functools.partial(_fox_decode_kernel, n_heads=n_heads, n_new=n_new, pps=pps),
        grid_spec=grid_spec,
        out_shape=jax.ShapeDtypeStruct((b, n_tok, w_b), F32),
        compiler_params=_cparams(("arbitrary", "arbitrary")),
        name="fox_decode",
    )(page_table, q, k_new8, v_new8, cn8, norm_w_tiled,
      *([cache_kt] * pps), *([cache_vt] * pps), *([suffix] * pps))


def _rms(x, w):
    return x * lax.rsqrt(jnp.mean(x * x, axis=-1, keepdims=True) + EPS) * w


def _gelu_tanh(x):
    return x * (0.5 * (1.0 + jnp.tanh(math.sqrt(2.0 / math.pi) * (x + 0.044715 * (x * x * x)))))


def _mix_ffn_kernel(x_ref, oa_ref, ob_ref, cuv_ref, cvw_ref, wm_ref, bse_ref, wo_ref, nf_ref,
                    wg_ref, wu_ref, cw_ref, wd_ref, pm_ref, nfin_ref,
                    y_ref, tail_ref, cv_ref, carry_ref,
                    *, seq_len, dff_chunk, final_norm, multi_seq):
    t = pl.program_id(1)
    tm = x_ref.shape[0]
    w_a, w_b = oa_ref.shape[1], ob_ref.shape[1]
    w_c = cuv_ref.shape[1] // 2
    d_ff = wg_ref.shape[1]
    tail_rows = tail_ref.shape[1]

    gel = _gelu_tanh(cuv_ref[...])
    cu, cv = gel[:, :w_c], gel[:, w_c:]
    ones_h = _head_ones(w_c)
    cvn = cv * lax.rsqrt(_dot2(cv * cv, ones_h) * (1.0 / HEAD_DIM) + EPS) * cvw_ref[...]
    cv_ref[...] = cvn
    cvn_b = cvn.astype(BF16)
    lane_lo = _iota((LANES, PAIR), 1) < HEAD_DIM
    row_blocks = []
    for n in range(tm // LANES):
        rs = slice(n * LANES, (n + 1) * LANES)
        pair_blocks = []
        for j in range(w_c // PAIR):
            ls = slice(j * PAIR, (j + 1) * PAIR)
            v_p = cvn_b[rs, ls]
            mix_a = jnp.dot(wm_ref[2 * j], v_p, preferred_element_type=F32)
            mix_b = jnp.dot(wm_ref[2 * j + 1], v_p, preferred_element_type=F32)
            pair_blocks.append(cu[rs, ls] * (jnp.where(lane_lo, mix_a, mix_b) + bse_ref[:, ls]))
        row_blocks.append(jnp.concatenate(pair_blocks, axis=1))
    o_c = jnp.concatenate(row_blocks, axis=0)

    x1 = (x_ref[...]
          + jnp.dot(oa_ref[...].astype(BF16), wo_ref[0:w_a, :], preferred_element_type=F32)
          + jnp.dot(ob_ref[...].astype(BF16), wo_ref[w_a:w_a + w_b, :], preferred_element_type=F32)
          + jnp.dot(o_c.astype(BF16), wo_ref[w_a + w_b:, :], preferred_element_type=F32))
    h2 = _rms(x1, nf_ref[...]).astype(BF16)

    row = _iota((tm, 1), 0)
    if multi_seq:
        pos = row % seq_len
    else:
        pos = row

        @pl.when(t == 0)
        def _():
            carry_ref[...] = pm_ref[0]

    acc = jnp.zeros((tm, x_ref.shape[1]), F32)
    for c0 in range(0, d_ff, dff_chunk):
        cs = slice(c0, c0 + dff_chunk)
        gp = jnp.dot(h2, wg_ref[:, cs], preferred_element_type=F32)
        up = jnp.dot(h2, wu_ref[:, cs], preferred_element_type=F32)
        if multi_seq:
            first1, first2 = pm_ref[0, :, cs], pm_ref[1, :, cs]
        else:
            c6, c7 = carry_ref[SUBLANES - 2:SUBLANES - 1, cs], carry_ref[SUBLANES - 1:SUBLANES, cs]
            first1 = c7
            first2 = jnp.where(row == 0, c6, c7)
            carry_ref[:, cs] = gp[tm - SUBLANES:tm, :]
        gm1 = jnp.where(pos >= 1, pltpu.roll(gp, 1, axis=0), first1)
        gm2 = jnp.where(pos >= 2, pltpu.roll(gp, 2, axis=0), first2)
        gate = cw_ref[0:1, cs] * gm2 + cw_ref[1:2, cs] * gm1 + cw_ref[2:3, cs] * gp
        act = (_silu(gate) * up).astype(BF16)
        acc = acc + jnp.dot(act, wd_ref[cs, :], preferred_element_type=F32)
        tail_ref[0, :, cs] = gp[tm - tail_rows:tm, :]
    x2 = x1 + acc
    if final_norm:
        x2 = _rms(x2, nfin_ref[...])
    y_ref[...] = x2


def _mix_ffn(x, oa, ob, cuv, cvw, wm, bse, wo, nf, wg, wu, cw, wd, pm, nfin, seq_len, tm, final_norm):
    n, d = x.shape
    multi_seq = seq_len < tm
    assert n % tm == 0 and (tm % seq_len == 0 if multi_seq else seq_len % tm == 0)
    assert tm % LANES == 0 and wm.shape[1] == LANES
    n_seq = 1 if multi_seq else n // seq_len
    seq_tiles = 1 if multi_seq else seq_len // tm
    n_tiles = n // tm
    d_ff = wg.shape[1]
    dff_chunk = 256
    assert d_ff % dff_chunk == 0
    tail_rows = tm if multi_seq else SUBLANES
    w_c = cuv.shape[1] // 2

    def rows(w):
        return pl.BlockSpec((tm, w), lambda s, t: (s * seq_tiles + t, 0))

    def const(a):
        nd = a.ndim
        return pl.BlockSpec(a.shape, lambda s, t: (0,) * nd, pipeline_mode=pl.Buffered(1))

    if multi_seq:
        pm_spec = pl.BlockSpec((2, tm, d_ff), lambda s, t: (0, s * seq_tiles + t, 0))
        grid = (n_tiles, 1)
    else:
        pm_spec = pl.BlockSpec((1, SUBLANES, d_ff), lambda s, t: (s, 0, 0))
        grid = (n_seq, seq_tiles)
    n_tail = n_tiles if multi_seq else n_seq
    return pl.pallas_call(
        functools.partial(_mix_ffn_kernel, seq_len=seq_len, dff_chunk=dff_chunk, final_norm=final_norm,
                          multi_seq=multi_seq),
        grid=grid,
        in_specs=[rows(d), rows(oa.shape[1]), rows(ob.shape[1]), rows(cuv.shape[1]),
                  const(cvw), const(wm), const(bse), const(wo), const(nf), const(wg), const(wu), const(cw),
                  const(wd), pm_spec, const(nfin)],
        out_specs=[rows(d),
                   pl.BlockSpec((1, tail_rows, d_ff), lambda s, t: (s, 0, 0)),
                   rows(w_c)],
        out_shape=[jax.ShapeDtypeStruct((n, d), F32),
                   jax.ShapeDtypeStruct((n_tail, tail_rows, d_ff), F32),
                   jax.ShapeDtypeStruct((n, w_c), F32)],
        scratch_shapes=[pltpu.VMEM((SUBLANES, d_ff), F32)],
        compiler_params=_cparams(("arbitrary", "arbitrary")),
        name="mix_ffn",
    )(x, oa, ob, cuv, cvw, wm, bse, wo, nf, wg, wu, cw, wd, pm, nfin)


def _pad_axis(a, axis, before, after):
    pads = [(0, 0)] * a.ndim
    pads[axis] = (before, after)
    return jnp.pad(a, pads)


def _lanes_block(parts):
    lead = parts[0][1].shape[:-1]
    out = jnp.zeros(lead + (LANES,), F32)
    for off, a in parts:
        out = lax.dynamic_update_slice_in_dim(out, a.astype(F32), off, axis=a.ndim - 1)
    return out


def kernel(x_prompt, x_sample, cache_fox_k, cache_fox_v, cache_fox_logf, page_table, state_gdn, state_gdn_conv,
           state_ffn_conv, norm_mix, w_in, gdn_conv_w, gdn_a_log, gdn_dt_bias, gdn_norm_w, fox_f_bias,
           fox_out_norm, chunk_v_norm, chunk_ws, chunk_bs, w_out, norm_ffn, ffn_w_gate, ffn_w_up, ffn_conv_w,
           ffn_w_down, norm_final):
    depth, d_model, _ = w_in.shape
    b_p, seq, _ = x_prompt.shape
    b_s, t_s, _ = x_sample.shape
    gh, fh, cg = gdn_a_log.shape[1], fox_f_bias.shape[1], chunk_ws.shape[1]
    assert gdn_norm_w.shape[1] == HEAD_DIM and gh % 2 == 0 and fh % 2 == 0 and cg % 2 == 0
    w_a, w_b, w_c = gh * HEAD_DIM, fh * HEAD_DIM, cg * HEAD_DIM
    d_ff = ffn_w_gate.shape[2]
    conv_k = gdn_conv_w.shape[1]
    ffn_k = ffn_conv_w.shape[1]
    chunk_len = chunk_ws.shape[2]
    n_phys, page = cache_fox_k.shape[1], cache_fox_k.shape[2]
    assert gh <= SUBLANES and fh <= SUBLANES and conv_k - 1 <= t_s <= GDN_CHUNK and ffn_k == 3
    assert chunk_len == LANES and seq % chunk_len == 0 and t_s <= chunk_len

    sizes = [3 * w_a, w_a, gh, gh, 3 * w_b, fh, 2 * w_c]
    starts = [0]
    for s in sizes:
        starts.append(starts[-1] + s)
    i_aqkv, i_az, i_beta, i_alpha, i_bqkv, i_bf, i_cuv = starts[:-1]
    offs, pos = {}, 0
    for name, width in [('a_qkv', 3 * w_a), ('a_z', w_a), ('b_q', w_b), ('b_k', w_b), ('b_v', w_b),
                        ('c_uv', 2 * w_c), ('small', LANES)]:
        offs[name] = (pos, pos + width)
        pos += width

    xp = x_prompt.reshape(b_p * seq, d_model)
    xs = x_sample.reshape(b_s * t_s, d_model)
    cache_kt = cache_fox_k.transpose(0, 1, 3, 4, 2).reshape(depth, n_phys, w_b, page)
    cache_vt = cache_fox_v.transpose(0, 1, 3, 4, 2).reshape(depth, n_phys, w_b, page)
    logf_rows = cache_fox_logf.transpose(0, 3, 1, 2).reshape(depth * fh * n_phys, page)
    suffix = _page_suffix(logf_rows, math.gcd(depth * fh * n_phys, 1024)).reshape(depth, fh, n_phys, page)

    tri = jnp.tril(jnp.ones((chunk_len, chunk_len), bool))
    tri_s = jnp.tril(jnp.ones((t_s, t_s), bool))
    n_rep = chunk_len // t_s
    assert (b_s * t_s) % chunk_len == 0 and chunk_len % t_s == 0

    prompt_rows, sample_rows = [], []
    for l in range(depth):
        w = w_in[l]
        small_w = _lanes_block([(SM_BETA, w[:, i_beta:i_beta + gh]), (SM_G, w[:, i_alpha:i_alpha + gh]),
                                (SM_F, w[:, i_bf:i_bf + fh])])
        w_r = jnp.concatenate([w[:, i_aqkv:i_aqkv + 3 * w_a], w[:, i_az:i_az + w_a],
                               w[:, i_bqkv:i_bqkv + 3 * w_b], w[:, i_cuv:i_cuv + 2 * w_c], small_w],
                              axis=1).astype(BF16)
        par = jnp.concatenate([
            _lanes_block([(SM_G, gdn_dt_bias[l][None, :]), (SM_F, fox_f_bias[l][None, :])]),
            _lanes_block([(SM_G, gdn_a_log[l][None, :])]),
            jnp.zeros((SUBLANES - 2, LANES), F32)], axis=0)
        nm = norm_mix[l][None, :]
        gnw = jnp.tile(gdn_norm_w[l], gh)[None, :]
        fnw = jnp.tile(fox_out_norm[l], fh)[None, :]
        cvw = jnp.tile(chunk_v_norm[l], cg)[None, :]
        wo = w_out[l].astype(BF16)
        wg, wu, wd = ffn_w_gate[l].astype(BF16), ffn_w_up[l].astype(BF16), ffn_w_down[l].astype(BF16)
        nf = norm_ffn[l][None, :]
        nfin = norm_final[None, :]
        last = l == depth - 1

        aqkv, az, bq, bk, bv, cuv, small, cumt = _inproj(xp, nm, w_r, par, offs, seq, 512)
        o_a, s_new = _gdn(aqkv.reshape(b_p, seq, 3 * w_a), az.reshape(b_p, seq, w_a),
                          small.reshape(b_p, seq, LANES), jnp.zeros((b_p, SUBLANES, 3 * w_a), F32),
                          jnp.zeros((b_p, gh, HEAD_DIM, HEAD_DIM), F32), gdn_conv_w[l], gnw, seq, 512)
        o_b = _fox_prompt(bq.reshape(b_p, seq, w_b), bk.reshape(b_p, seq, w_b), bv.reshape(b_p, seq, w_b),
                          cumt, fnw, 512, 512)
        wm = jnp.where(tri, chunk_ws[l], 0.0).astype(BF16)
        bse = jnp.repeat(chunk_bs[l].T, HEAD_DIM, axis=1)
        xp, tail, _ = _mix_ffn(xp, o_a.reshape(b_p * seq, w_a), o_b.reshape(b_p * seq, w_b), cuv, cvw, wm, bse,
                               wo, nf, wg, wu, ffn_conv_w[l], wd, jnp.zeros((b_p, SUBLANES, d_ff), F32), nfin,
                               seq, 512, last)
        prompt_rows.append((bk.reshape(b_p, seq, fh, HEAD_DIM), bv.reshape(b_p, seq, fh, HEAD_DIM),
                            small[:, SM_F:SM_F + fh].reshape(b_p, seq, fh), s_new,
                            aqkv.reshape(b_p, seq, 3 * w_a)[:, seq - (conv_k - 1):, :],
                            tail[:, SUBLANES - (ffn_k - 1):, :]))

        n_s = b_s * t_s
        aqkv, az, bq, bk, bv, cuv, small, cumt = _inproj(xs, nm, w_r, par, offs, t_s, n_s)
        pad_t = GDN_CHUNK - t_s
        o_a, s_new = _gdn(_pad_axis(aqkv.reshape(b_s, t_s, 3 * w_a), 1, 0, pad_t),
                          _pad_axis(az.reshape(b_s, t_s, w_a), 1, 0, pad_t),
                          _pad_axis(small.reshape(b_s, t_s, LANES), 1, 0, pad_t),
                          _pad_axis(state_gdn_conv[l], 1, SUBLANES - (conv_k - 1), 0),
                          state_gdn[l], gdn_conv_w[l], gnw, t_s, GDN_CHUNK)
        o_a = o_a[:, :t_s, :]
        cn8 = _pad_axis(_pad_axis(cumt.reshape(SUBLANES, b_s, t_s).transpose(1, 0, 2), 2, 0, SUBLANES - t_s),
                        1, 0, 0)
        o_b = _fox_decode(l, page_table, bq.reshape(b_s, t_s, w_b),
                          _pad_axis(bk.reshape(b_s, t_s, w_b), 1, 0, SUBLANES - t_s),
                          _pad_axis(bv.reshape(b_s, t_s, w_b), 1, 0, SUBLANES - t_s),
                          cn8, fnw, cache_kt, cache_vt, suffix, t_s)
        eye_rep = jnp.eye(n_rep, dtype=F32)
        wm_s = jnp.stack([jnp.kron(eye_rep, jnp.where(tri_s, chunk_ws[l, g, :t_s, :t_s], 0.0))
                          for g in range(cg)]).astype(BF16)
        bse_s = jnp.repeat(jnp.tile(chunk_bs[l, :, :t_s], (1, n_rep)).T, HEAD_DIM, axis=1)
        buf = state_ffn_conv[l]
        zrow = jnp.zeros((b_s, 1, d_ff), F32)
        pm1 = jnp.concatenate([buf[:, 1:2]] + [zrow] * (t_s - 1), axis=1).reshape(n_s, d_ff)
        pm2 = jnp.concatenate([buf[:, 0:1], buf[:, 1:2]] + [zrow] * (t_s - 2), axis=1).reshape(n_s, d_ff)
        xs, tail, cvn = _mix_ffn(xs, o_a.reshape(n_s, w_a), o_b.reshape(n_s, w_b), cuv, cvw, wm_s, bse_s,
                                 wo, nf, wg, wu, ffn_conv_w[l], wd, jnp.stack([pm1, pm2]), nfin,
                                 t_s, chunk_len, last)
        sample_rows.append((bk.reshape(b_s, t_s, fh, HEAD_DIM), bv.reshape(b_s, t_s, fh, HEAD_DIM),
                            small[:, SM_F:SM_F + fh].reshape(b_s, t_s, fh), s_new,
                            aqkv.reshape(b_s, t_s, 3 * w_a)[:, t_s - (conv_k - 1):, :],
                            tail.reshape(b_s, t_s, d_ff)[:, t_s - (ffn_k - 1):, :],
                            cvn.reshape(b_s, t_s, w_c)))

    def stack(rows_, i):
        return jnp.stack([r[i] for r in rows_])

    y_prompt = xp.reshape(b_p, seq, d_model)
    y_sample = xs.reshape(b_s, t_s, d_model)
    return (y_prompt, y_sample,
            stack(prompt_rows, 0), stack(prompt_rows, 1), stack(prompt_rows, 2), stack(prompt_rows, 3),
            stack(prompt_rows, 4), stack(prompt_rows, 5),
            stack(sample_rows, 0), stack(sample_rows, 1), stack(sample_rows, 2), stack(sample_rows, 3),
            stack(sample_rows, 4), stack(sample_rows, 5), stack(sample_rows, 6))
```

```python
import functools
import math

import jax
import jax.numpy as jnp
from jax import lax
from jax.experimental import pallas as pl
from jax.experimental.pallas import tpu as pltpu

F32 = jnp.float32
BF16 = jnp.bfloat16
EPS = 1e-6
LANES = 128
SUBLANES = 8
HEAD_DIM = 64
PAIR = 2 * HEAD_DIM
GDN_CHUNK = 64
FOX_PAGES_PER_STEP = 16
FOX_PAGES_PER_SUBBLOCK = 4
VMEM_LIMIT = 56 * 1024 * 1024
HIGHEST = lax.Precision.HIGHEST
NEG_INF = -1e30

SM_BETA, SM_G, SM_F = 0, 8, 16


def _cparams(sem):
    return pltpu.CompilerParams(dimension_semantics=sem, vmem_limit_bytes=VMEM_LIMIT)


def _dot(a, b):
    return jnp.dot(a.astype(BF16), b.astype(BF16), preferred_element_type=F32)


def _dot_nt(a, b):
    return lax.dot_general(a.astype(BF16), b.astype(BF16), (((1,), (1,)), ((), ())),
                           preferred_element_type=F32)


def _dot_tn(a, b):
    return lax.dot_general(a.astype(BF16), b.astype(BF16), (((0,), (0,)), ((), ())),
                           preferred_element_type=F32)


def _dot2(a, b01):
    hi = a.astype(BF16)
    lo = (a - hi.astype(F32)).astype(BF16)
    b = b01.astype(BF16)
    return (jnp.dot(hi, b, preferred_element_type=F32) + jnp.dot(lo, b, preferred_element_type=F32))


def _iota(shape, dim):
    return lax.broadcasted_iota(jnp.int32, shape, dim)


def _head_ones(n):
    return (_iota((n, n), 0) // HEAD_DIM == _iota((n, n), 1) // HEAD_DIM).astype(F32)


def _silu(x):
    return x * jax.nn.sigmoid(x)


def _inproj_kernel(x_ref, nw_ref, wt_ref, par_ref, *refs, offs, seq_tiles, seq_len, kv_transposed, n_alias):
    (aqkv_ref, az_ref, bq_ref, bk_ref, bv_ref, cuv_ref, small_ref, cumt_ref, logft_ref,
     carry_ref) = refs[n_alias:]
    i = pl.program_id(0)
    tm = x_ref.shape[0]
    x = x_ref[...]
    h = (x * lax.rsqrt(jnp.mean(x * x, axis=-1, keepdims=True) + EPS) * nw_ref[...]).astype(BF16)

    def mm(name):
        lo, hi = offs[name]
        return lax.dot_general(h, wt_ref[lo:hi, :], (((1,), (1,)), ((), ())), preferred_element_type=F32)

    aqkv_ref[...] = mm('a_qkv')
    az_ref[...] = mm('a_z')
    bq_ref[...] = mm('b_q')
    if kv_transposed:
        bk_ref[0, 0] = mm('b_k').T
        bv_ref[0, 0] = mm('b_v').T
    else:
        bk_ref[...] = mm('b_k')
        bv_ref[...] = mm('b_v')
    cuv_ref[...] = mm('c_uv')

    z = mm('small') + par_ref[0:1, :]
    lane = _iota(z.shape, 1)
    t = jnp.log1p(jnp.exp(-jnp.abs(z)))
    softplus = jnp.maximum(z, 0.0) + t
    log_sig = jnp.minimum(z, 0.0) - t
    sm = jnp.where(lane < SM_G, jax.nn.sigmoid(z),
                   jnp.where(lane < SM_F, -jnp.exp(par_ref[1:2, :]) * softplus, log_sig))
    small_ref[...] = sm

    lb = min(seq_len, tm)
    row, col = _iota((tm, tm), 0), _iota((tm, tm), 1)
    tri = jnp.where((col <= row) & (row // lb == col // lb), 1.0, 0.0).astype(F32)
    cum = jnp.dot(tri, sm, precision=HIGHEST, preferred_element_type=F32)
    if seq_tiles > 1:
        @pl.when(i % seq_tiles == 0)
        def _():
            carry_ref[...] = jnp.zeros_like(carry_ref)
        cum = cum + carry_ref[0:1, :]
        carry_ref[0:1, :] = cum[tm - 1:tm, :]
    cumt_ref[...] = cum.T[SM_F:SM_F + SUBLANES, :]
    logft_ref[...] = sm.T[SM_F:SM_F + SUBLANES, :]


def _inproj(x, nw, wt, par, offs, seq_len, tm, kv_stack=None):
    n, d = x.shape
    assert n % tm == 0 and (seq_len % tm == 0 or tm % seq_len == 0)
    seq_tiles = max(seq_len // tm, 1)
    widths = {k: hi - lo for k, (lo, hi) in offs.items()}
    names = ['a_qkv', 'a_z', 'b_q', 'b_k', 'b_v', 'c_uv', 'small']
    out_shape = [jax.ShapeDtypeStruct((n, widths[k]), F32) for k in names]
    out_specs = [pl.BlockSpec((tm, widths[k]), lambda i: (i, 0)) for k in names]
    out_shape += [jax.ShapeDtypeStruct((SUBLANES, n), F32)] * 2
    out_specs += [pl.BlockSpec((SUBLANES, tm), lambda i: (0, i))] * 2
    args = [x, nw, wt, par]
    in_specs = [pl.BlockSpec((tm, d), lambda i: (i, 0)),
                pl.BlockSpec((1, d), lambda i: (0, 0)),
                pl.BlockSpec(wt.shape, lambda i: (0, 0)),
                pl.BlockSpec(par.shape, lambda i: (0, 0))]
    aliases = {}
    if kv_stack is not None:
        layer, depth, kt_prev, vt_prev = kv_stack
        assert seq_len % tm == 0
        w_b = widths['b_k']
        for pos in (3, 4):
            out_shape[pos] = jax.ShapeDtypeStruct((depth, n // seq_len, w_b, seq_len), F32)
            out_specs[pos] = pl.BlockSpec((1, 1, w_b, tm), lambda i: (layer, i // seq_tiles, 0, i % seq_tiles))
        if kt_prev is not None:
            args += [kt_prev, vt_prev]
            in_specs += [pl.BlockSpec(memory_space=pl.ANY)] * 2
            aliases = {4: 3, 5: 4}
    return pl.pallas_call(
        functools.partial(_inproj_kernel, offs=offs, seq_tiles=seq_tiles, seq_len=seq_len,
                          kv_transposed=kv_stack is not None, n_alias=len(aliases)),
        grid=(n // tm,),
        in_specs=in_specs,
        out_specs=out_specs,
        out_shape=out_shape,
        input_output_aliases=aliases,
        scratch_shapes=[pltpu.VMEM((SUBLANES, LANES), F32)],
        compiler_params=_cparams(("arbitrary",)),
        name="inproj",
    )(*args)


def _unit_lower_inverses(lmats):
    n = lmats[0].shape[0]
    row, col = _iota((n, n), 0), _iota((n, n), 1)
    eye = (row == col).astype(F32)
    base = SUBLANES
    in_base = row // base == col // base
    lds = [jnp.where(in_base, lm, 0.0) for lm in lmats]
    xs = [eye - ld for ld in lds]
    ps = [_dot(ld, ld) for ld in lds]
    xs = [x + _dot(x, p) for x, p in zip(xs, ps)]
    ps = [_dot(p, p) for p in ps]
    xs = [x + _dot(x, p) for x, p in zip(xs, ps)]
    s = base
    while s < HEAD_DIM:
        off_mask = (row // (2 * s) == col // (2 * s)) & (row // s != col // s)
        ts_ = [_dot(x, jnp.where(off_mask, lm, 0.0)) for x, lm in zip(xs, lmats)]
        xs = [x - _dot(t_, x) for x, t_ in zip(xs, ts_)]
        s *= 2
    return xs


def _gdn_kernel(aqkv_ref, az_ref, small_ref, buf_ref, s0_ref, cw_ref, nw_ref,
                o_ref, sout_ref,
                s_ref, prev_ref, qs_ref, qg_ref, k_ref, kb_ref, kbg_ref, vb_ref, gc_ref, oacc_ref,
                conv_ref, uw_ref, at_ref, *, n_pairs, valid_len, group):
    t = pl.program_id(1)
    nt = pl.num_programs(1)
    nb, ts, w3 = aqkv_ref.shape
    w_a = n_pairs * PAIR
    c = GDN_CHUNK
    rows_all = nb * ts
    cps = ts // c

    @pl.when(t == 0)
    def _():
        for bi in range(nb):
            for j in range(n_pairs):
                s_a = s0_ref[bi, 2 * j]
                s_b = s0_ref[bi, 2 * j + 1]
                top = jnp.concatenate([s_a, jnp.zeros_like(s_a)], axis=1)
                bot = jnp.concatenate([jnp.zeros_like(s_b), s_b], axis=1)
                s_ref[bi, j] = jnp.concatenate([top, bot], axis=0)
        prev_ref[...] = buf_ref[...]

    x = aqkv_ref[...].reshape(rows_all, w3)
    kw = cw_ref.shape[0]
    conv = cw_ref[kw - 1:kw, :] * x
    for i in range(1, kw):
        conv = conv + cw_ref[kw - 1 - i:kw - i, :] * pltpu.roll(x, i, axis=0)
    conv_ref[...] = conv
    for bi in range(nb):
        head = jnp.concatenate([prev_ref[bi], x[bi * ts:bi * ts + SUBLANES]], axis=0)
        fixed = cw_ref[0:1, :] * head[SUBLANES - kw + 1:2 * SUBLANES - kw + 1]
        for i in range(1, kw):
            fixed = fixed + cw_ref[i:i + 1, :] * head[SUBLANES - kw + 1 + i:2 * SUBLANES - kw + 1 + i]
        conv_ref[bi * ts:bi * ts + SUBLANES, :] = fixed
        prev_ref[bi] = x[(bi + 1) * ts - SUBLANES:(bi + 1) * ts]
    a = _silu(conv_ref[...])
    rowid = _iota((rows_all, 1), 0) % ts + t * ts
    valid = (rowid < valid_len).astype(F32)
    ones_h = _head_ones(w_a)
    q = a[:, 0:w_a]
    k = a[:, w_a:2 * w_a]
    v = a[:, 2 * w_a:3 * w_a] * valid
    qn = q * lax.rsqrt(_dot2(q * q, ones_h) + EPS) * valid
    kn = k * lax.rsqrt(_dot2(k * k, ones_h) + EPS) * valid

    sm = small_ref[...].reshape(rows_all, LANES) * valid
    row, col = _iota((rows_all, rows_all), 0), _iota((rows_all, rows_all), 1)
    tri = jnp.where((col <= row) & (row // c == col // c), 1.0, 0.0).astype(F32)
    gcum = jnp.dot(tri, sm, precision=HIGHEST, preferred_element_type=F32)
    gc_ref[...] = gcum
    lane, hrow = _iota((LANES, w_a), 1), _iota((LANES, w_a), 0)
    exp_b = (hrow == lane // HEAD_DIM + SM_BETA).astype(F32)
    exp_g = (hrow == lane // HEAD_DIM + SM_G).astype(F32)
    beta_e = _dot2(sm, exp_b)
    gcum_e = jnp.dot(gcum, exp_g, precision=HIGHEST, preferred_element_type=F32)
    eg = jnp.exp(gcum_e)
    scale = HEAD_DIM ** -0.5
    qs_ref[...] = qn * scale
    qg_ref[...] = qn * (scale * eg)
    k_ref[...] = kn
    kb = kn * beta_e
    kb_ref[...] = kb
    kbg_ref[...] = kb * eg
    vb_ref[...] = v * beta_e

    r2, c2 = _iota((PAIR, PAIR), 0), _iota((PAIR, PAIR), 1)
    same_head = (r2 // HEAD_DIM) == (c2 // HEAD_DIM)
    strict = same_head & (c2 < r2)
    rw, cw_ = _iota((c, PAIR), 0), _iota((c, PAIR), 1)
    incl_wide = (cw_ % HEAD_DIM) <= rw
    lane_lo = _iota((c, PAIR), 1) < HEAD_DIM

    def stack(m):
        return jnp.concatenate([jnp.where(lane_lo, m, 0.0), jnp.where(lane_lo, 0.0, m)], axis=0)

    def gate_cols(g_c, j):
        return g_c[:, SM_G + 2 * j:SM_G + 2 * j + 1], g_c[:, SM_G + 2 * j + 1:SM_G + 2 * j + 2]

    def prepare(gi, carry):
        items = []
        for i in range(group):
            ci = gi * group + i
            r0 = pl.multiple_of(ci * c, c)
            g_c = gc_ref[pl.ds(r0, c), :]
            g_t = g_c.T
            for j in range(n_pairs):
                ls = slice(j * PAIR, (j + 1) * PAIR)
                ga_col, gb_col = gate_cols(g_c, j)
                g_row = jnp.concatenate([g_t[SM_G + 2 * j:SM_G + 2 * j + 1, :],
                                         g_t[SM_G + 2 * j + 1:SM_G + 2 * j + 2, :]], axis=1)
                g_col2 = jnp.concatenate([ga_col, gb_col], axis=0)
                g_wide = jnp.where(lane_lo, ga_col, gb_col)
                items.append(dict(
                    ci=ci, j=j,
                    k2=stack(k_ref[pl.ds(r0, c), ls]),
                    kb2=stack(kb_ref[pl.ds(r0, c), ls]),
                    rhs=jnp.concatenate([stack(vb_ref[pl.ds(r0, c), ls]), stack(kbg_ref[pl.ds(r0, c), ls])],
                                        axis=1),
                    qs=qs_ref[pl.ds(r0, c), ls],
                    decay2=jnp.exp(jnp.minimum(g_col2 - g_row, 0.0)),
                    decay_w=jnp.exp(jnp.minimum(g_wide - g_row, 0.0))))
        kk = [_dot_nt(it['kb2'], it['k2']) for it in items]
        qk = [_dot_nt(it['qs'], it['k2']) for it in items]
        lmats = [jnp.where(strict, m * it['decay2'], 0.0) for m, it in zip(kk, items)]
        for m, it in zip(qk, items):
            at_ref[it['ci'], it['j']] = jnp.where(incl_wide, m * it['decay_w'], 0.0)
        tinvs = _unit_lower_inverses(lmats)
        sols = [_dot(tinv, it['rhs']) for tinv, it in zip(tinvs, items)]
        for sol, it in zip(sols, items):
            uw_ref[it['ci'], it['j']] = sol
        return carry

    lax.fori_loop(0, rows_all // c // group, prepare, 0)

    def recur(n, carry):
        items = []
        for bi in range(nb):
            ci = bi * cps + n
            r0 = pl.multiple_of(ci * c, c)
            g_c = gc_ref[pl.ds(r0, c), :]
            for j in range(n_pairs):
                ls = slice(j * PAIR, (j + 1) * PAIR)
                ga_col, gb_col = gate_cols(g_c, j)
                g_wide = jnp.where(lane_lo, ga_col, gb_col)
                g_last = jnp.where(lane_lo, ga_col[c - 1:c, :], gb_col[c - 1:c, :])
                g_last_col = jnp.where(_iota((PAIR, 1), 0) < HEAD_DIM, ga_col[c - 1:c, :], gb_col[c - 1:c, :])
                items.append(dict(
                    bi=bi, r0=r0, ls=ls, j=j,
                    u2=uw_ref[ci, j, :, 0:PAIR], w2=uw_ref[ci, j, :, PAIR:2 * PAIR], attn=at_ref[ci, j],
                    s2=s_ref[bi, j], qg=qg_ref[pl.ds(r0, c), ls],
                    kdec=k_ref[pl.ds(r0, c), ls] * jnp.exp(g_last - g_wide),
                    s_decay=jnp.exp(g_last_col)))
        ws = [_dot(it['w2'], it['s2']) for it in items]
        qss = [_dot(it['qg'], it['s2']) for it in items]
        vnew2 = [it['u2'] - w for it, w in zip(items, ws)]
        avs = [_dot(it['attn'], v2) for it, v2 in zip(items, vnew2)]
        kvs = [_dot_tn(it['kdec'], v2[:c, :] + v2[c:, :]) for it, v2 in zip(items, vnew2)]
        for it, q_s, av, kv in zip(items, qss, avs, kvs):
            oacc_ref[pl.ds(it['r0'], c), it['ls']] = q_s + av
            s_ref[it['bi'], it['j']] = it['s2'] * it['s_decay'] + jnp.where(same_head, kv, 0.0)
        return carry

    lax.fori_loop(0, cps, recur, 0)

    o = oacc_ref[...]
    on = o * lax.rsqrt(_dot2(o * o, ones_h) * (1.0 / HEAD_DIM) + EPS) * nw_ref[...]
    o_ref[...] = (on * _silu(az_ref[...].reshape(rows_all, w_a))).reshape(nb, ts, w_a)

    @pl.when(t == nt - 1)
    def _():
        for bi in range(nb):
            for j in range(n_pairs):
                s2 = s_ref[bi, j]
                sout_ref[bi, 2 * j] = s2[:HEAD_DIM, :HEAD_DIM]
                sout_ref[bi, 2 * j + 1] = s2[HEAD_DIM:, HEAD_DIM:]


def _gdn(aqkv, az, small, bufpad, s0, conv_w, norm_w_tiled, valid_len, ts, nb):
    b, l, w3 = aqkv.shape
    w_a = w3 // 3
    n_pairs = w_a // PAIR
    heads = s0.shape[1]
    n_chunks = nb * ts // GDN_CHUNK
    group = math.gcd(n_chunks, 4)
    assert l % ts == 0 and ts % GDN_CHUNK == 0 and heads == 2 * n_pairs and b % nb == 0
    rows_all = nb * ts
    scratch = [pltpu.VMEM((nb, n_pairs, PAIR, PAIR), F32), pltpu.VMEM((nb, SUBLANES, w3), F32)]
    scratch += [pltpu.VMEM((rows_all, w_a), F32) for _ in range(6)]
    scratch += [pltpu.VMEM((rows_all, LANES), F32), pltpu.VMEM((rows_all, w_a), F32),
                pltpu.VMEM((rows_all, w3), F32),
                pltpu.VMEM((n_chunks, n_pairs, PAIR, 2 * PAIR), F32),
                pltpu.VMEM((n_chunks, n_pairs, GDN_CHUNK, PAIR), F32)]
    return pl.pallas_call(
        functools.partial(_gdn_kernel, n_pairs=n_pairs, valid_len=valid_len, group=group),
        grid=(b // nb, l // ts),
        in_specs=[pl.BlockSpec((nb, ts, w3), lambda i, t: (i, t, 0)),
                  pl.BlockSpec((nb, ts, w_a), lambda i, t: (i, t, 0)),
                  pl.BlockSpec((nb, ts, LANES), lambda i, t: (i, t, 0)),
                  pl.BlockSpec((nb, SUBLANES, w3), lambda i, t: (i, 0, 0)),
                  pl.BlockSpec((nb, heads, HEAD_DIM, HEAD_DIM), lambda i, t: (i, 0, 0, 0)),
                  pl.BlockSpec(conv_w.shape, lambda i, t: (0, 0)),
                  pl.BlockSpec(norm_w_tiled.shape, lambda i, t: (0, 0))],
        out_specs=[pl.BlockSpec((nb, ts, w_a), lambda i, t: (i, t, 0)),
                   pl.BlockSpec((nb, heads, HEAD_DIM, HEAD_DIM), lambda i, t: (i, 0, 0, 0))],
        out_shape=[jax.ShapeDtypeStruct((b, l, w_a), F32),
                   jax.ShapeDtypeStruct((b, heads, HEAD_DIM, HEAD_DIM), F32)],
        scratch_shapes=scratch,
        compiler_params=_cparams(("arbitrary", "arbitrary")),
        name="gdn",
    )(aqkv, az, small, bufpad, s0, conv_w, norm_w_tiled)


def _fox_prompt_kernel(q_ref, kt_ref, vt_ref, cumt_ref, nw_ref, o_ref, *, n_pairs):
    i = pl.program_id(1)
    tq = q_ref.shape[1]
    tk = tq
    scale = HEAD_DIM ** -0.5
    lane_lo = _iota((tq, PAIR), 1) < HEAD_DIM
    causal = _iota((tq, tk), 1) <= _iota((tq, tk), 0)
    q0 = pl.multiple_of(i * tq, tq)
    pairs = []
    for j in range(n_pairs):
        ls = slice(j * PAIR, (j + 1) * PAIR)
        q_p = q_ref[0, :, ls] * scale
        heads = (2 * j, 2 * j + 1)
        q_hs = [jnp.where(lane_lo, q_p, 0.0).astype(BF16), jnp.where(lane_lo, 0.0, q_p).astype(BF16)]
        c_refs = [cumt_ref[h:h + 1, pl.ds(q0, LANES)][:, 0:1] for h in heads]

        def block(k0, carry, masked, q_hs=q_hs, c_refs=c_refs, heads=heads, ls=ls):
            k_blk = kt_ref[0, 0, ls, pl.ds(k0, tk)].astype(BF16)
            v_blk = vt_ref[0, 0, ls, pl.ds(k0, tk)].astype(BF16)
            ss = [jnp.dot(q_h, k_blk, preferred_element_type=F32) for q_h in q_hs]
            ss = [s + (c_ref - cumt_ref[h:h + 1, pl.ds(k0, tk)]) for s, c_ref, h in zip(ss, c_refs, heads)]
            if masked:
                ss = [jnp.where(causal, s, NEG_INF) for s in ss]
            m_news = [jnp.maximum(st[0], jnp.max(s, axis=-1, keepdims=True)) for st, s in zip(carry, ss)]
            ps = [jnp.exp(s - m_new) for s, m_new in zip(ss, m_news)]
            alphas = [jnp.exp(st[0] - m_new) for st, m_new in zip(carry, m_news)]
            pvs = [lax.dot_general(p.astype(BF16), v_blk, (((1,), (1,)), ((), ())), preferred_element_type=F32)
                   for p in ps]
            return tuple((m_new, a * st[1] + jnp.sum(p, axis=-1, keepdims=True), a * st[2] + pv)
                         for st, m_new, a, p, pv in zip(carry, m_news, alphas, ps, pvs))

        init = tuple((jnp.full((tq, 1), NEG_INF, F32), jnp.zeros((tq, 1), F32), jnp.zeros((tq, PAIR), F32))
                     for _ in heads)
        carry = lax.fori_loop(
            0, i, lambda kb, c_, block=block: block(pl.multiple_of(kb * tk, tk), c_, False), init)
        outs = [acc / l for _, l, acc in block(q0, carry, True)]
        pairs.append(jnp.where(lane_lo, outs[0], outs[1]))
    o = jnp.concatenate(pairs, axis=1)
    ones_h = _head_ones(o.shape[1])
    o_ref[0] = o * lax.rsqrt(_dot2(o * o, ones_h) * (1.0 / HEAD_DIM) + EPS) * nw_ref[...]


def _fox_prompt(layer, q, kt, vt, cumt, norm_w_tiled, tq):
    b, l, w_b = q.shape
    assert l % tq == 0 and tq % LANES == 0
    return pl.pallas_call(
        functools.partial(_fox_prompt_kernel, n_pairs=w_b // PAIR),
        grid=(b, l // tq),
        in_specs=[pl.BlockSpec((1, tq, w_b), lambda i, t: (i, t, 0)),
                  pl.BlockSpec((1, 1, w_b, l), lambda i, t: (layer, i, 0, 0)),
                  pl.BlockSpec((1, 1, w_b, l), lambda i, t: (layer, i, 0, 0)),
                  pl.BlockSpec((SUBLANES, l), lambda i, t: (0, i)),
                  pl.BlockSpec(norm_w_tiled.shape, lambda i, t: (0, 0))],
        out_specs=pl.BlockSpec((1, tq, w_b), lambda i, t: (i, t, 0)),
        out_shape=jax.ShapeDtypeStruct((b, l, w_b), F32),
        compiler_params=_cparams(("arbitrary", "arbitrary")),
        name="fox_prompt",
    )(q, kt, vt, cumt, norm_w_tiled)


def _page_suffix_kernel(x_ref, m_ref, o_ref):
    x = x_ref[...]
    x1 = x.astype(BF16)
    r1 = x - x1.astype(F32)
    x2 = r1.astype(BF16)
    x3 = (r1 - x2.astype(F32)).astype(BF16)
    m = m_ref[...]
    o_ref[...] = (jnp.dot(x1, m, preferred_element_type=F32) + jnp.dot(x2, m, preferred_element_type=F32)
                  + jnp.dot(x3, m, preferred_element_type=F32))


def _page_suffix(logf_rows, tr):
    r, page = logf_rows.shape
    assert page == LANES and r % tr == 0
    idx = jnp.arange(page)
    m = (idx[:, None] >= idx[None, :]).astype(BF16)
    return pl.pallas_call(
        _page_suffix_kernel,
        grid=(r // tr,),
        in_specs=[pl.BlockSpec((tr, page), lambda i: (i, 0)),
                  pl.BlockSpec(m.shape, lambda i: (0, 0))],
        out_specs=pl.BlockSpec((tr, page), lambda i: (i, 0)),
        out_shape=jax.ShapeDtypeStruct((r, page), F32),
        compiler_params=_cparams(("arbitrary",)),
        name="page_suffix",
    )(logf_rows, m)


def _fox_decode_kernel(pt_ref, q_ref, kn_ref, vn_ref, cn_ref, nw_ref, *refs, n_heads, n_new, pps, sub_pages):
    k_refs = refs[0:pps]
    v_refs = refs[pps:2 * pps]
    s_refs = refs[2 * pps:3 * pps]
    o_ref = refs[3 * pps]
    qbd_ref, m_ref, l_ref, acc_ref, carry_ref = refs[3 * pps + 1:]
    bb = pl.program_id(0)
    j = pl.program_id(1)
    nj = pl.num_programs(1)
    n_tok = q_ref.shape[1]
    w_b = q_ref.shape[2]
    rows = n_tok * SUBLANES
    page = k_refs[0].shape[3]
    n_pages = nj * pps
    scale = HEAD_DIM ** -0.5

    @pl.when(j == 0)
    def _():
        q = q_ref[0] * scale
        headmask = (_iota((SUBLANES, w_b), 1) // HEAD_DIM == _iota((SUBLANES, w_b), 0)).astype(F32)
        qbd = jnp.concatenate([q[t:t + 1, :] * headmask for t in range(n_tok)], axis=0)
        qbd_ref[...] = qbd.astype(BF16)
        kn = kn_ref[0]
        s = _dot_nt(qbd, kn)
        cn = cn_ref[0]
        s = s - jnp.concatenate([cn] * n_tok, axis=0)
        kidx = _iota((rows, SUBLANES), 1)
        tidx = _iota((rows, SUBLANES), 0) // SUBLANES
        s = jnp.where((kidx <= tidx) & (kidx < n_new), s, NEG_INF)
        m = jnp.max(s, axis=-1, keepdims=True)
        p = jnp.exp(s - m)
        m_ref[...] = m
        l_ref[...] = jnp.sum(p, axis=-1, keepdims=True)
        acc_ref[...] = _dot(p, vn_ref[0])
        carry_ref[...] = jnp.zeros_like(carry_ref)

    lane = _iota((SUBLANES, page), 1)
    carry = carry_ref[...]
    biases = []
    zrows = jnp.zeros((SUBLANES - n_heads, page), F32)
    for i in range(pps):
        sub = pt_ref[bb, n_pages - 1 - (j * pps + i)] % SUBLANES
        incl = jnp.concatenate([s_refs[i][0, h, pl.ds(sub, 1), :] for h in range(n_heads)] + [zrows], axis=0)
        excl = jnp.where(lane < page - 1, pltpu.roll(incl, page - 1, axis=1), 0.0)
        biases.append(excl + carry)
        carry = carry + incl[:, 0:1]
    carry_ref[...] = carry
    qbd = qbd_ref[...]
    n_sub = pps // sub_pages
    subs = [range(u * sub_pages, (u + 1) * sub_pages) for u in range(n_sub)]
    kts = [jnp.concatenate([k_refs[i][0, 0].astype(BF16) for i in sb], axis=1) for sb in subs]
    ss = [jnp.dot(qbd, kt, preferred_element_type=F32) for kt in kts]
    ss = [s + jnp.concatenate([jnp.concatenate([biases[i] for i in sb], axis=1)] * n_tok, axis=0)
          for s, sb in zip(ss, subs)]
    ms = [jnp.max(s, axis=-1, keepdims=True) for s in ss]
    ps = [jnp.exp(s - m) for s, m in zip(ss, ms)]
    ls = [jnp.sum(p, axis=-1, keepdims=True) for p in ps]
    vts = [jnp.concatenate([v_refs[i][0, 0].astype(BF16) for i in sb], axis=1) for sb in subs]
    pvs = [lax.dot_general(p.astype(BF16), vt, (((1,), (1,)), ((), ())), preferred_element_type=F32)
           for p, vt in zip(ps, vts)]
    m_old = m_ref[...]
    m_new = m_old
    for m in ms:
        m_new = jnp.maximum(m_new, m)
    alpha = jnp.exp(m_old - m_new)
    l_new = alpha * l_ref[...]
    acc = alpha * acc_ref[...]
    for m, l_u, pv in zip(ms, ls, pvs):
        w_u = jnp.exp(m - m_new)
        l_new = l_new + w_u * l_u
        acc = acc + w_u * pv
    m_ref[...] = m_new
    l_ref[...] = l_new
    acc_ref[...] = acc

    @pl.when(j == nj - 1)
    def _():
        o = acc_ref[...] / l_ref[...]
        diag = (_iota((rows, w_b), 1) // HEAD_DIM) == (_iota((rows, w_b), 0) % SUBLANES)
        o = jnp.where(diag, o, 0.0)
        group = (_iota((SUBLANES, rows), 1) // SUBLANES == _iota((SUBLANES, rows), 0)).astype(F32)
        o_tok = jnp.dot(group, o, precision=HIGHEST, preferred_element_type=F32)
        ones_h = _head_ones(w_b)
        ss = jnp.dot(o_tok * o_tok, ones_h, precision=HIGHEST, preferred_element_type=F32)
        on = o_tok * lax.rsqrt(ss * (1.0 / HEAD_DIM) + EPS) * nw_ref[...]
        o_ref[0] = on[0:n_tok, :]


def _fox_decode(layer, page_table, q, k_new8, v_new8, cn8, norm_w_tiled, cache_kt, cache_vt, suffix, n_new):
    b, n_tok, w_b = q.shape
    n_phys, page = cache_kt.shape[1], cache_kt.shape[3]
    n_heads = suffix.shape[1]
    n_pages = page_table.shape[1]
    pps = math.gcd(n_pages, FOX_PAGES_PER_STEP)
    sub_pages = math.gcd(pps, FOX_PAGES_PER_SUBBLOCK)
    assert n_phys % SUBLANES == 0
    rows = n_tok * SUBLANES

    def kv_map(i):
        return lambda bb, j, pt: (layer, pt[bb, n_pages - 1 - (j * pps + i)], 0, 0)

    def sfx_map(i):
        return lambda bb, j, pt: (layer, 0, pt[bb, n_pages - 1 - (j * pps + i)] // SUBLANES, 0)

    in_specs = [pl.BlockSpec((1, n_tok, w_b), lambda bb, j, pt: (bb, 0, 0)),
                pl.BlockSpec((1, SUBLANES, w_b), lambda bb, j, pt: (bb, 0, 0)),
                pl.BlockSpec((1, SUBLANES, w_b), lambda bb, j, pt: (bb, 0, 0)),
                pl.BlockSpec((1, SUBLANES, SUBLANES), lambda bb, j, pt: (bb, 0, 0)),
                pl.BlockSpec(norm_w_tiled.shape, lambda bb, j, pt: (0, 0))]
    in_specs += [pl.BlockSpec((1, 1, w_b, page), kv_map(i)) for i in range(pps)]
    in_specs += [pl.BlockSpec((1, 1, w_b, page), kv_map(i)) for i in range(pps)]
    in_specs += [pl.BlockSpec((1, n_heads, SUBLANES, page), sfx_map(i)) for i in range(pps)]
    grid_spec = pltpu.PrefetchScalarGridSpec(
        num_scalar_prefetch=1,
        grid=(b, n_pages // pps),
        in_specs=in_specs,
        out_specs=pl.BlockSpec((1, n_tok, w_b), lambda bb, j, pt: (bb, 0, 0)),
        scratch_shapes=[pltpu.VMEM((rows, w_b), BF16), pltpu.VMEM((rows, 1), F32), pltpu.VMEM((rows, 1), F32),
                        pltpu.VMEM((rows, w_b), F32), pltpu.VMEM((SUBLANES, page), F32)])
    return pl.pallas_call(
        functools.partial(_fox_decode_kernel, n_heads=n_heads, n_new=n_new, pps=pps,
                          sub_pages=sub_pages),
        grid_spec=grid_spec,
        out_shape=jax.ShapeDtypeStruct((b, n_tok, w_b), F32),
        compiler_params=_cparams(("arbitrary", "arbitrary")),
        name="fox_decode",
    )(page_table, q, k_new8, v_new8, cn8, norm_w_tiled,
      *([cache_kt] * pps), *([cache_vt] * pps), *([suffix] * pps))


def _rms(x, w):
    return x * lax.rsqrt(jnp.mean(x * x, axis=-1, keepdims=True) + EPS) * w


def _gelu_tanh(x):
    return x * (0.5 * (1.0 + jnp.tanh(math.sqrt(2.0 / math.pi) * (x + 0.044715 * (x * x * x)))))


def _mix_ffn_kernel(x_ref, oa_ref, ob_ref, cuv_ref, cvw_ref, wm_ref, bse_ref, wo_ref, nf_ref,
                    wg_ref, wu_ref, cw_ref, wd_ref, pm_ref, nfin_ref,
                    y_ref, tail_ref, cv_ref, carry_ref,
                    *, seq_len, dff_chunk, final_norm, multi_seq):
    t = pl.program_id(1)
    tm = x_ref.shape[0]
    w_a, w_b = oa_ref.shape[1], ob_ref.shape[1]
    w_c = cuv_ref.shape[1] // 2
    d_ff = wg_ref.shape[1]
    tail_rows = tail_ref.shape[1]

    gel = _gelu_tanh(cuv_ref[...])
    cu, cv = gel[:, :w_c], gel[:, w_c:]
    ones_h = _head_ones(w_c)
    cvn = cv * lax.rsqrt(_dot2(cv * cv, ones_h) * (1.0 / HEAD_DIM) + EPS) * cvw_ref[...]
    cv_ref[...] = cvn
    cvn_b = cvn.astype(BF16)
    lane_lo = _iota((LANES, PAIR), 1) < HEAD_DIM
    row_blocks = []
    for n in range(tm // LANES):
        rs = slice(n * LANES, (n + 1) * LANES)
        pair_blocks = []
        for j in range(w_c // PAIR):
            ls = slice(j * PAIR, (j + 1) * PAIR)
            v_p = cvn_b[rs, ls]
            mix_a = jnp.dot(wm_ref[2 * j], v_p, preferred_element_type=F32)
            mix_b = jnp.dot(wm_ref[2 * j + 1], v_p, preferred_element_type=F32)
            pair_blocks.append(cu[rs, ls] * (jnp.where(lane_lo, mix_a, mix_b) + bse_ref[:, ls]))
        row_blocks.append(jnp.concatenate(pair_blocks, axis=1))
    o_c = jnp.concatenate(row_blocks, axis=0)

    x1 = (x_ref[...]
          + jnp.dot(oa_ref[...].astype(BF16), wo_ref[0:w_a, :], preferred_element_type=F32)
          + jnp.dot(ob_ref[...].astype(BF16), wo_ref[w_a:w_a + w_b, :], preferred_element_type=F32)
          + jnp.dot(o_c.astype(BF16), wo_ref[w_a + w_b:, :], preferred_element_type=F32))
    h2 = _rms(x1, nf_ref[...]).astype(BF16)

    row = _iota((tm, 1), 0)
    if multi_seq:
        pos = row % seq_len
    else:
        pos = row

        @pl.when(t == 0)
        def _():
            carry_ref[...] = pm_ref[0]

    acc = jnp.zeros((tm, x_ref.shape[1]), F32)
    for c0 in range(0, d_ff, dff_chunk):
        cs = slice(c0, c0 + dff_chunk)
        gp = jnp.dot(h2, wg_ref[:, cs], preferred_element_type=F32)
        up = jnp.dot(h2, wu_ref[:, cs], preferred_element_type=F32)
        if multi_seq:
            first1, first2 = pm_ref[0, :, cs], pm_ref[1, :, cs]
        else:
            c6, c7 = carry_ref[SUBLANES - 2:SUBLANES - 1, cs], carry_ref[SUBLANES - 1:SUBLANES, cs]
            first1 = c7
            first2 = jnp.where(row == 0, c6, c7)
            carry_ref[:, cs] = gp[tm - SUBLANES:tm, :]
        gm1 = jnp.where(pos >= 1, pltpu.roll(gp, 1, axis=0), first1)
        gm2 = jnp.where(pos >= 2, pltpu.roll(gp, 2, axis=0), first2)
        gate = cw_ref[0:1, cs] * gm2 + cw_ref[1:2, cs] * gm1 + cw_ref[2:3, cs] * gp
        act = (_silu(gate) * up).astype(BF16)
        acc = acc + jnp.dot(act, wd_ref[cs, :], preferred_element_type=F32)
        tail_ref[0, :, cs] = gp[tm - tail_rows:tm, :]
    x2 = x1 + acc
    if final_norm:
        x2 = _rms(x2, nfin_ref[...])
    y_ref[...] = x2


def _mix_ffn(x, oa, ob, cuv, cvw, wm, bse, wo, nf, wg, wu, cw, wd, pm, nfin, seq_len, tm, final_norm):
    n, d = x.shape
    multi_seq = seq_len < tm
    assert n % tm == 0 and (tm % seq_len == 0 if multi_seq else seq_len % tm == 0)
    assert tm % LANES == 0 and wm.shape[1] == LANES
    n_seq = 1 if multi_seq else n // seq_len
    seq_tiles = 1 if multi_seq else seq_len // tm
    n_tiles = n // tm
    d_ff = wg.shape[1]
    dff_chunk = 256
    assert d_ff % dff_chunk == 0
    tail_rows = tm if multi_seq else SUBLANES
    w_c = cuv.shape[1] // 2

    def rows(w):
        return pl.BlockSpec((tm, w), lambda s, t: (s * seq_tiles + t, 0))

    def const(a):
        nd = a.ndim
        return pl.BlockSpec(a.shape, lambda s, t: (0,) * nd, pipeline_mode=pl.Buffered(1))

    if multi_seq:
        pm_spec = pl.BlockSpec((2, tm, d_ff), lambda s, t: (0, s * seq_tiles + t, 0))
        grid = (n_tiles, 1)
    else:
        pm_spec = pl.BlockSpec((1, SUBLANES, d_ff), lambda s, t: (s, 0, 0))
        grid = (n_seq, seq_tiles)
    n_tail = n_tiles if multi_seq else n_seq
    return pl.pallas_call(
        functools.partial(_mix_ffn_kernel, seq_len=seq_len, dff_chunk=dff_chunk, final_norm=final_norm,
                          multi_seq=multi_seq),
        grid=grid,
        in_specs=[rows(d), rows(oa.shape[1]), rows(ob.shape[1]), rows(cuv.shape[1]),
                  const(cvw), const(wm), const(bse), const(wo), const(nf), const(wg), const(wu), const(cw),
                  const(wd), pm_spec, const(nfin)],
        out_specs=[rows(d),
                   pl.BlockSpec((1, tail_rows, d_ff), lambda s, t: (s, 0, 0)),
                   rows(w_c)],
        out_shape=[jax.ShapeDtypeStruct((n, d), F32),
                   jax.ShapeDtypeStruct((n_tail, tail_rows, d_ff), F32),
                   jax.ShapeDtypeStruct((n, w_c), F32)],
        scratch_shapes=[pltpu.VMEM((SUBLANES, d_ff), F32)],
        compiler_params=_cparams(("arbitrary", "arbitrary")),
        name="mix_ffn",
    )(x, oa, ob, cuv, cvw, wm, bse, wo, nf, wg, wu, cw, wd, pm, nfin)


def _pad_axis(a, axis, before, after):
    pads = [(0, 0)] * a.ndim
    pads[axis] = (before, after)
    return jnp.pad(a, pads)


def _lanes_block(parts):
    lead = parts[0][1].shape[:-1]
    out = jnp.zeros(lead + (LANES,), F32)
    for off, a in parts:
        out = lax.dynamic_update_slice_in_dim(out, a.astype(F32), off, axis=a.ndim - 1)
    return out


def kernel(x_prompt, x_sample, cache_fox_k, cache_fox_v, cache_fox_logf, page_table, state_gdn, state_gdn_conv,
           state_ffn_conv, norm_mix, w_in, gdn_conv_w, gdn_a_log, gdn_dt_bias, gdn_norm_w, fox_f_bias,
           fox_out_norm, chunk_v_norm, chunk_ws, chunk_bs, w_out, norm_ffn, ffn_w_gate, ffn_w_up, ffn_conv_w,
           ffn_w_down, norm_final):
    depth, d_model, _ = w_in.shape
    b_p, seq, _ = x_prompt.shape
    b_s, t_s, _ = x_sample.shape
    gh, fh, cg = gdn_a_log.shape[1], fox_f_bias.shape[1], chunk_ws.shape[1]
    assert gdn_norm_w.shape[1] == HEAD_DIM and gh % 2 == 0 and fh % 2 == 0 and cg % 2 == 0
    w_a, w_b, w_c = gh * HEAD_DIM, fh * HEAD_DIM, cg * HEAD_DIM
    d_ff = ffn_w_gate.shape[2]
    conv_k = gdn_conv_w.shape[1]
    ffn_k = ffn_conv_w.shape[1]
    chunk_len = chunk_ws.shape[2]
    n_phys, page = cache_fox_k.shape[1], cache_fox_k.shape[2]
    assert gh <= SUBLANES and fh <= SUBLANES and conv_k - 1 <= t_s <= GDN_CHUNK and ffn_k == 3
    assert chunk_len == LANES and seq % chunk_len == 0 and t_s <= chunk_len

    sizes = [3 * w_a, w_a, gh, gh, 3 * w_b, fh, 2 * w_c]
    starts = [0]
    for s in sizes:
        starts.append(starts[-1] + s)
    i_aqkv, i_az, i_beta, i_alpha, i_bqkv, i_bf, i_cuv = starts[:-1]
    offs, pos = {}, 0
    for name, width in [('a_qkv', 3 * w_a), ('a_z', w_a), ('b_q', w_b), ('b_k', w_b), ('b_v', w_b),
                        ('c_uv', 2 * w_c), ('small', LANES)]:
        offs[name] = (pos, pos + width)
        pos += width

    xp = x_prompt.reshape(b_p * seq, d_model)
    xs = x_sample.reshape(b_s * t_s, d_model)
    cache_kt = cache_fox_k.transpose(0, 1, 3, 4, 2).reshape(depth, n_phys, w_b, page)
    cache_vt = cache_fox_v.transpose(0, 1, 3, 4, 2).reshape(depth, n_phys, w_b, page)
    logf_rows = cache_fox_logf.transpose(0, 3, 1, 2).reshape(depth * fh * n_phys, page)
    suffix = _page_suffix(logf_rows, math.gcd(depth * fh * n_phys, 1024)).reshape(depth, fh, n_phys, page)

    tri = jnp.tril(jnp.ones((chunk_len, chunk_len), bool))
    tri_s = jnp.tril(jnp.ones((t_s, t_s), bool))
    n_rep = chunk_len // t_s
    assert (b_s * t_s) % chunk_len == 0 and chunk_len % t_s == 0

    prompt_rows, sample_rows = [], []
    w_in_t = w_in.transpose(2, 0, 1)
    kt_all = vt_all = None
    logft_rows = []
    for l in range(depth):
        w = w_in_t[:, l, :]

        def zrows(n):
            return jnp.zeros((n, d_model), F32)

        small_w = jnp.concatenate([
            w[i_beta:i_beta + gh], zrows(SM_G - SM_BETA - gh),
            w[i_alpha:i_alpha + gh], zrows(SM_F - SM_G - gh),
            w[i_bf:i_bf + fh], zrows(LANES - SM_F - fh)], axis=0)
        w_r = jnp.concatenate([w[i_aqkv:i_aqkv + 3 * w_a], w[i_az:i_az + w_a],
                               w[i_bqkv:i_bqkv + 3 * w_b], w[i_cuv:i_cuv + 2 * w_c], small_w],
                              axis=0).astype(BF16)
        par = jnp.concatenate([
            _lanes_block([(SM_G, gdn_dt_bias[l][None, :]), (SM_F, fox_f_bias[l][None, :])]),
            _lanes_block([(SM_G, gdn_a_log[l][None, :])]),
            jnp.zeros((SUBLANES - 2, LANES), F32)], axis=0)
        nm = norm_mix[l][None, :]
        gnw = jnp.tile(gdn_norm_w[l], gh)[None, :]
        fnw = jnp.tile(fox_out_norm[l], fh)[None, :]
        cvw = jnp.tile(chunk_v_norm[l], cg)[None, :]
        wo = w_out[l].astype(BF16)
        wg, wu, wd = ffn_w_gate[l].astype(BF16), ffn_w_up[l].astype(BF16), ffn_w_down[l].astype(BF16)
        nf = norm_ffn[l][None, :]
        nfin = norm_final[None, :]
        last = l == depth - 1

        aqkv, az, bq, kt_all, vt_all, cuv, small, cumt, logft = _inproj(
            xp, nm, w_r, par, offs, seq, 512, kv_stack=(l, depth, kt_all, vt_all))
        logft_rows.append(logft)
        o_a, s_new = _gdn(aqkv.reshape(b_p, seq, 3 * w_a), az.reshape(b_p, seq, w_a),
                          small.reshape(b_p, seq, LANES), jnp.zeros((b_p, SUBLANES, 3 * w_a), F32),
                          jnp.zeros((b_p, gh, HEAD_DIM, HEAD_DIM), F32), gdn_conv_w[l], gnw, seq, 512, 1)
        o_b = _fox_prompt(l, bq.reshape(b_p, seq, w_b), kt_all, vt_all, cumt, fnw, 512)
        wm = jnp.where(tri, chunk_ws[l], 0.0).astype(BF16)
        bse = jnp.repeat(chunk_bs[l].T, HEAD_DIM, axis=1)
        xp, tail, _ = _mix_ffn(xp, o_a.reshape(b_p * seq, w_a), o_b.reshape(b_p * seq, w_b), cuv, cvw, wm, bse,
                               wo, nf, wg, wu, ffn_conv_w[l], wd, jnp.zeros((b_p, SUBLANES, d_ff), F32), nfin,
                               seq, 512, last)
        prompt_rows.append((None, None, None, s_new,
                            aqkv.reshape(b_p, seq, 3 * w_a)[:, seq - (conv_k - 1):, :],
                            tail[:, SUBLANES - (ffn_k - 1):, :]))

        n_s = b_s * t_s
        aqkv, az, bq, bk, bv, cuv, small, cumt, _ = _inproj(xs, nm, w_r, par, offs, t_s, n_s)
        pad_t = GDN_CHUNK - t_s
        o_a, s_new = _gdn(_pad_axis(aqkv.reshape(b_s, t_s, 3 * w_a), 1, 0, pad_t),
                          _pad_axis(az.reshape(b_s, t_s, w_a), 1, 0, pad_t),
                          _pad_axis(small.reshape(b_s, t_s, LANES), 1, 0, pad_t),
                          _pad_axis(state_gdn_conv[l], 1, SUBLANES - (conv_k - 1), 0),
                          state_gdn[l], gdn_conv_w[l], gnw, t_s, GDN_CHUNK, math.gcd(b_s, 4))
        o_a = o_a[:, :t_s, :]
        cn8 = _pad_axis(_pad_axis(cumt.reshape(SUBLANES, b_s, t_s).transpose(1, 0, 2), 2, 0, SUBLANES - t_s),
                        1, 0, 0)
        o_b = _fox_decode(l, page_table, bq.reshape(b_s, t_s, w_b),
                          _pad_axis(bk.reshape(b_s, t_s, w_b), 1, 0, SUBLANES - t_s),
                          _pad_axis(bv.reshape(b_s, t_s, w_b), 1, 0, SUBLANES - t_s),
                          cn8, fnw, cache_kt, cache_vt, suffix, t_s)
        eye_rep = jnp.eye(n_rep, dtype=F32)
        wm_s = jnp.stack([jnp.kron(eye_rep, jnp.where(tri_s, chunk_ws[l, g, :t_s, :t_s], 0.0))
                          for g in range(cg)]).astype(BF16)
        bse_s = jnp.repeat(jnp.tile(chunk_bs[l, :, :t_s], (1, n_rep)).T, HEAD_DIM, axis=1)
        buf = state_ffn_conv[l]
        zrow = jnp.zeros((b_s, 1, d_ff), F32)
        pm1 = jnp.concatenate([buf[:, 1:2]] + [zrow] * (t_s - 1), axis=1).reshape(n_s, d_ff)
        pm2 = jnp.concatenate([buf[:, 0:1], buf[:, 1:2]] + [zrow] * (t_s - 2), axis=1).reshape(n_s, d_ff)
        xs, tail, cvn = _mix_ffn(xs, o_a.reshape(n_s, w_a), o_b.reshape(n_s, w_b), cuv, cvw, wm_s, bse_s,
                                 wo, nf, wg, wu, ffn_conv_w[l], wd, jnp.stack([pm1, pm2]), nfin,
                                 t_s, chunk_len, last)
        sample_rows.append((bk.reshape(b_s, t_s, fh, HEAD_DIM), bv.reshape(b_s, t_s, fh, HEAD_DIM),
                            small[:, SM_F:SM_F + fh].reshape(b_s, t_s, fh), s_new,
                            aqkv.reshape(b_s, t_s, 3 * w_a)[:, t_s - (conv_k - 1):, :],
                            tail.reshape(b_s, t_s, d_ff)[:, t_s - (ffn_k - 1):, :],
                            cvn.reshape(b_s, t_s, w_c)))

    def stack(rows_, i):
        return jnp.stack([r[i] for r in rows_])

    y_prompt = xp.reshape(b_p, seq, d_model)
    y_sample = xs.reshape(b_s, t_s, d_model)
    fox_k_prompt = kt_all.reshape(depth, b_p, fh, HEAD_DIM, seq).transpose(0, 1, 4, 2, 3)
    fox_v_prompt = vt_all.reshape(depth, b_p, fh, HEAD_DIM, seq).transpose(0, 1, 4, 2, 3)
    fox_logf_prompt = jnp.stack(logft_rows)[:, :fh, :].reshape(depth, fh, b_p, seq).transpose(0, 2, 3, 1)
    return (y_prompt, y_sample,
            fox_k_prompt, fox_v_prompt, fox_logf_prompt, stack(prompt_rows, 3),
            stack(prompt_rows, 4), stack(prompt_rows, 5),
            stack(sample_rows, 0), stack(sample_rows, 1), stack(sample_rows, 2), stack(sample_rows, 3),
            stack(sample_rows, 4), stack(sample_rows, 5), stack(sample_rows, 6))
```

```python
import functools
import math

import jax
import jax.numpy as jnp
from jax import lax
from jax.experimental import pallas as pl
from jax.experimental.pallas import tpu as pltpu

F32 = jnp.float32
BF16 = jnp.bfloat16
EPS = 1e-6
LANES = 128
SUBLANES = 8
HEAD_DIM = 64
PAIR = 2 * HEAD_DIM
GDN_CHUNK = 64
FOX_PAGES_PER_STEP = 16
FOX_PAGES_PER_SUBBLOCK = 4
VMEM_LIMIT = 56 * 1024 * 1024
HIGHEST = lax.Precision.HIGHEST
NEG_INF = -1e30

SM_BETA, SM_G, SM_F = 0, 8, 16


def _cparams(sem):
    return pltpu.CompilerParams(dimension_semantics=sem, vmem_limit_bytes=VMEM_LIMIT)


def _dot(a, b):
    return jnp.dot(a.astype(BF16), b.astype(BF16), preferred_element_type=F32)


def _dot_nt(a, b):
    return lax.dot_general(a.astype(BF16), b.astype(BF16), (((1,), (1,)), ((), ())),
                           preferred_element_type=F32)


def _dot_tn(a, b):
    return lax.dot_general(a.astype(BF16), b.astype(BF16), (((0,), (0,)), ((), ())),
                           preferred_element_type=F32)


def _dot2(a, b01):
    hi = a.astype(BF16)
    lo = (a - hi.astype(F32)).astype(BF16)
    b = b01.astype(BF16)
    return (jnp.dot(hi, b, preferred_element_type=F32) + jnp.dot(lo, b, preferred_element_type=F32))


def _split3(a):
    a1 = a.astype(BF16)
    r1 = a - a1.astype(F32)
    a2 = r1.astype(BF16)
    a3 = (r1 - a2.astype(F32)).astype(BF16)
    return a1, a2, a3


def _dot01_left(m01, a):
    m = m01.astype(BF16)
    t1, t2, t3 = _split3(a)
    return (jnp.dot(m, t1, preferred_element_type=F32) + jnp.dot(m, t2, preferred_element_type=F32)
            + jnp.dot(m, t3, preferred_element_type=F32))


def _dot01_right(a, m01):
    m = m01.astype(BF16)
    t1, t2, t3 = _split3(a)
    return (jnp.dot(t1, m, preferred_element_type=F32) + jnp.dot(t2, m, preferred_element_type=F32)
            + jnp.dot(t3, m, preferred_element_type=F32))


def _iota(shape, dim):
    return lax.broadcasted_iota(jnp.int32, shape, dim)


def _head_ones(n):
    return (_iota((n, n), 0) // HEAD_DIM == _iota((n, n), 1) // HEAD_DIM).astype(F32)


def _silu(x):
    return x * jax.nn.sigmoid(x)


def _inproj_kernel(x_ref, nw_ref, wt_ref, par_ref, *refs, offs, seq_tiles, seq_len, kv_transposed, kv_slot,
                   n_alias):
    (aqkv_ref, az_ref, bq_ref, bk_ref, bv_ref, cuv_ref, small_ref, cumt_ref, logft_ref,
     carry_ref) = refs[n_alias:]
    i = pl.program_id(0)
    tm = x_ref.shape[0]
    x = x_ref[...]
    h = (x * lax.rsqrt(jnp.mean(x * x, axis=-1, keepdims=True) + EPS) * nw_ref[...]).astype(BF16)

    def mm(name):
        lo, hi = offs[name]
        return lax.dot_general(h, wt_ref[lo:hi, :], (((1,), (1,)), ((), ())), preferred_element_type=F32)

    aqkv_ref[...] = mm('a_qkv')
    az_ref[...] = mm('a_z')
    bq_ref[...] = mm('b_q')
    if kv_transposed:
        bk_ref[kv_slot, 0] = mm('b_k').T
        bv_ref[kv_slot, 0] = mm('b_v').T
        for d in range(bk_ref.shape[0]):
            if d != kv_slot:
                bk_ref[d, 0] = jnp.zeros(bk_ref.shape[2:], F32)
                bv_ref[d, 0] = jnp.zeros(bv_ref.shape[2:], F32)
    else:
        bk_ref[...] = mm('b_k')
        bv_ref[...] = mm('b_v')
    cuv_ref[...] = mm('c_uv')

    z = mm('small') + par_ref[0:1, :]
    lane = _iota(z.shape, 1)
    t = jnp.log1p(jnp.exp(-jnp.abs(z)))
    softplus = jnp.maximum(z, 0.0) + t
    log_sig = jnp.minimum(z, 0.0) - t
    sm = jnp.where(lane < SM_G, jax.nn.sigmoid(z),
                   jnp.where(lane < SM_F, -jnp.exp(par_ref[1:2, :]) * softplus, log_sig))
    small_ref[...] = sm

    lb = min(seq_len, tm)
    row, col = _iota((tm, tm), 0), _iota((tm, tm), 1)
    tri = jnp.where((col <= row) & (row // lb == col // lb), 1.0, 0.0)
    cum = _dot01_left(tri, sm)
    if seq_tiles > 1:
        @pl.when(i % seq_tiles == 0)
        def _():
            carry_ref[...] = jnp.zeros_like(carry_ref)
        cum = cum + carry_ref[0:1, :]
        carry_ref[0:1, :] = cum[tm - 1:tm, :]
    cumt_ref[...] = cum.T[SM_F:SM_F + SUBLANES, :]
    logft_ref[...] = sm.T[SM_F:SM_F + SUBLANES, :]


def _inproj(x, nw, wt, par, offs, seq_len, tm, kv_stack=None):
    n, d = x.shape
    assert n % tm == 0 and (seq_len % tm == 0 or tm % seq_len == 0)
    seq_tiles = max(seq_len // tm, 1)
    widths = {k: hi - lo for k, (lo, hi) in offs.items()}
    names = ['a_qkv', 'a_z', 'b_q', 'b_k', 'b_v', 'c_uv', 'small']
    out_shape = [jax.ShapeDtypeStruct((n, widths[k]), F32) for k in names]
    out_specs = [pl.BlockSpec((tm, widths[k]), lambda i: (i, 0)) for k in names]
    out_shape += [jax.ShapeDtypeStruct((SUBLANES, n), F32)] * 2
    out_specs += [pl.BlockSpec((SUBLANES, tm), lambda i: (0, i))] * 2
    args = [x, nw, wt, par]
    in_specs = [pl.BlockSpec((tm, d), lambda i: (i, 0)),
                pl.BlockSpec((1, d), lambda i: (0, 0)),
                pl.BlockSpec(wt.shape, lambda i: (0, 0)),
                pl.BlockSpec(par.shape, lambda i: (0, 0))]
    aliases = {}
    kv_slot = 0
    if kv_stack is not None:
        layer, depth, kt_prev, vt_prev = kv_stack
        assert seq_len % tm == 0
        w_b = widths['b_k']
        if kt_prev is None:
            kv_slot, blk, first = layer, depth, 0
        else:
            kv_slot, blk, first = 0, 1, layer
            args += [kt_prev, vt_prev]
            in_specs += [pl.BlockSpec(memory_space=pl.ANY)] * 2
            aliases = {4: 3, 5: 4}
        for pos in (3, 4):
            out_shape[pos] = jax.ShapeDtypeStruct((depth, n // seq_len, w_b, seq_len), F32)
            out_specs[pos] = pl.BlockSpec((blk, 1, w_b, tm), lambda i: (first, i // seq_tiles, 0, i % seq_tiles))
    return pl.pallas_call(
        functools.partial(_inproj_kernel, offs=offs, seq_tiles=seq_tiles, seq_len=seq_len,
                          kv_transposed=kv_stack is not None, kv_slot=kv_slot, n_alias=len(aliases)),
        grid=(n // tm,),
        in_specs=in_specs,
        out_specs=out_specs,
        out_shape=out_shape,
        input_output_aliases=aliases,
        scratch_shapes=[pltpu.VMEM((SUBLANES, LANES), F32)],
        compiler_params=_cparams(("arbitrary",)),
        name="inproj",
    )(*args)


def _unit_lower_inverses(lmats):
    n = lmats[0].shape[0]
    row, col = _iota((n, n), 0), _iota((n, n), 1)
    eye = (row == col).astype(F32)
    base = SUBLANES
    in_base = row // base == col // base
    lds = [jnp.where(in_base, lm, 0.0) for lm in lmats]
    xs = [eye - ld for ld in lds]
    ps = [_dot(ld, ld) for ld in lds]
    xs = [x + _dot(x, p) for x, p in zip(xs, ps)]
    ps = [_dot(p, p) for p in ps]
    xs = [x + _dot(x, p) for x, p in zip(xs, ps)]
    s = base
    while s < HEAD_DIM:
        off_mask = (row // (2 * s) == col // (2 * s)) & (row // s != col // s)
        ts_ = [_dot(x, jnp.where(off_mask, lm, 0.0)) for x, lm in zip(xs, lmats)]
        xs = [x - _dot(t_, x) for x, t_ in zip(xs, ts_)]
        s *= 2
    return xs


def _gdn_kernel(aqkv_ref, az_ref, small_ref, buf_ref, s0_ref, cw_ref, nw_ref,
                o_ref, sout_ref,
                s_ref, prev_ref, qs_ref, qg_ref, k_ref, kb_ref, kbg_ref, vb_ref, gc_ref, oacc_ref,
                conv_ref, uw_ref, at_ref, *, n_pairs, valid_len, group):
    t = pl.program_id(1)
    nt = pl.num_programs(1)
    nb, ts, w3 = aqkv_ref.shape
    w_a = n_pairs * PAIR
    c = GDN_CHUNK
    rows_all = nb * ts
    cps = ts // c

    @pl.when(t == 0)
    def _():
        for bi in range(nb):
            for j in range(n_pairs):
                s_a = s0_ref[bi, 2 * j]
                s_b = s0_ref[bi, 2 * j + 1]
                top = jnp.concatenate([s_a, jnp.zeros_like(s_a)], axis=1)
                bot = jnp.concatenate([jnp.zeros_like(s_b), s_b], axis=1)
                s_ref[bi, j] = jnp.concatenate([top, bot], axis=0)
        prev_ref[...] = buf_ref[...]

    x = aqkv_ref[...].reshape(rows_all, w3)
    kw = cw_ref.shape[0]
    conv = cw_ref[kw - 1:kw, :] * x
    for i in range(1, kw):
        conv = conv + cw_ref[kw - 1 - i:kw - i, :] * pltpu.roll(x, i, axis=0)
    conv_ref[...] = conv
    for bi in range(nb):
        head = jnp.concatenate([prev_ref[bi], x[bi * ts:bi * ts + SUBLANES]], axis=0)
        fixed = cw_ref[0:1, :] * head[SUBLANES - kw + 1:2 * SUBLANES - kw + 1]
        for i in range(1, kw):
            fixed = fixed + cw_ref[i:i + 1, :] * head[SUBLANES - kw + 1 + i:2 * SUBLANES - kw + 1 + i]
        conv_ref[bi * ts:bi * ts + SUBLANES, :] = fixed
        prev_ref[bi] = x[(bi + 1) * ts - SUBLANES:(bi + 1) * ts]
    a = _silu(conv_ref[...])
    rowid = _iota((rows_all, 1), 0) % ts + t * ts
    valid = (rowid < valid_len).astype(F32)
    ones_h = _head_ones(w_a)
    q = a[:, 0:w_a]
    k = a[:, w_a:2 * w_a]
    v = a[:, 2 * w_a:3 * w_a] * valid
    qn = q * lax.rsqrt(_dot2(q * q, ones_h) + EPS) * valid
    kn = k * lax.rsqrt(_dot2(k * k, ones_h) + EPS) * valid

    sm = small_ref[...].reshape(rows_all, LANES) * valid
    row, col = _iota((rows_all, rows_all), 0), _iota((rows_all, rows_all), 1)
    tri = jnp.where((col <= row) & (row // c == col // c), 1.0, 0.0)
    gcum = _dot01_left(tri, sm)
    gc_ref[...] = gcum
    lane, hrow = _iota((LANES, w_a), 1), _iota((LANES, w_a), 0)
    exp_b = (hrow == lane // HEAD_DIM + SM_BETA).astype(F32)
    exp_g = (hrow == lane // HEAD_DIM + SM_G).astype(F32)
    beta_e = _dot2(sm, exp_b)
    gcum_e = _dot01_right(gcum, exp_g)
    eg = jnp.exp(gcum_e)
    scale = HEAD_DIM ** -0.5
    qs_ref[...] = qn * scale
    qg_ref[...] = qn * (scale * eg)
    k_ref[...] = kn
    kb = kn * beta_e
    kb_ref[...] = kb
    kbg_ref[...] = kb * eg
    vb_ref[...] = v * beta_e

    r2, c2 = _iota((PAIR, PAIR), 0), _iota((PAIR, PAIR), 1)
    same_head = (r2 // HEAD_DIM) == (c2 // HEAD_DIM)
    strict = same_head & (c2 < r2)
    rw, cw_ = _iota((c, PAIR), 0), _iota((c, PAIR), 1)
    incl_wide = (cw_ % HEAD_DIM) <= rw
    lane_lo = _iota((c, PAIR), 1) < HEAD_DIM

    def stack(m):
        return jnp.concatenate([jnp.where(lane_lo, m, 0.0), jnp.where(lane_lo, 0.0, m)], axis=0)

    def gate_cols(g_c, j):
        return g_c[:, SM_G + 2 * j:SM_G + 2 * j + 1], g_c[:, SM_G + 2 * j + 1:SM_G + 2 * j + 2]

    def prepare(gi, carry):
        items = []
        for i in range(group):
            ci = gi * group + i
            r0 = pl.multiple_of(ci * c, c)
            g_c = gc_ref[pl.ds(r0, c), :]
            g_t = g_c.T
            for j in range(n_pairs):
                ls = slice(j * PAIR, (j + 1) * PAIR)
                ga_col, gb_col = gate_cols(g_c, j)
                g_row = jnp.concatenate([g_t[SM_G + 2 * j:SM_G + 2 * j + 1, :],
                                         g_t[SM_G + 2 * j + 1:SM_G + 2 * j + 2, :]], axis=1)
                g_col2 = jnp.concatenate([ga_col, gb_col], axis=0)
                g_wide = jnp.where(lane_lo, ga_col, gb_col)
                items.append(dict(
                    ci=ci, j=j,
                    k2=stack(k_ref[pl.ds(r0, c), ls]),
                    kb2=stack(kb_ref[pl.ds(r0, c), ls]),
                    rhs=jnp.concatenate([stack(vb_ref[pl.ds(r0, c), ls]), stack(kbg_ref[pl.ds(r0, c), ls])],
                                        axis=1),
                    qs=qs_ref[pl.ds(r0, c), ls],
                    decay2=jnp.exp(jnp.minimum(g_col2 - g_row, 0.0)),
                    decay_w=jnp.exp(jnp.minimum(g_wide - g_row, 0.0))))
        kk = [_dot_nt(it['kb2'], it['k2']) for it in items]
        qk = [_dot_nt(it['qs'], it['k2']) for it in items]
        lmats = [jnp.where(strict, m * it['decay2'], 0.0) for m, it in zip(kk, items)]
        for m, it in zip(qk, items):
            at_ref[it['ci'], it['j']] = jnp.where(incl_wide, m * it['decay_w'], 0.0)
        tinvs = _unit_lower_inverses(lmats)
        sols = [_dot(tinv, it['rhs']) for tinv, it in zip(tinvs, items)]
        for sol, it in zip(sols, items):
            uw_ref[it['ci'], it['j']] = sol
        return carry

    lax.fori_loop(0, rows_all // c // group, prepare, 0)

    def recur(n, carry):
        items = []
        for bi in range(nb):
            ci = bi * cps + n
            r0 = pl.multiple_of(ci * c, c)
            g_c = gc_ref[pl.ds(r0, c), :]
            for j in range(n_pairs):
                ls = slice(j * PAIR, (j + 1) * PAIR)
                ga_col, gb_col = gate_cols(g_c, j)
                g_wide = jnp.where(lane_lo, ga_col, gb_col)
                g_last = jnp.where(lane_lo, ga_col[c - 1:c, :], gb_col[c - 1:c, :])
                g_last_col = jnp.where(_iota((PAIR, 1), 0) < HEAD_DIM, ga_col[c - 1:c, :], gb_col[c - 1:c, :])
                items.append(dict(
                    bi=bi, r0=r0, ls=ls, j=j,
                    u2=uw_ref[ci, j, :, 0:PAIR], w2=uw_ref[ci, j, :, PAIR:2 * PAIR], attn=at_ref[ci, j],
                    s2=s_ref[bi, j], qg=qg_ref[pl.ds(r0, c), ls],
                    kdec=k_ref[pl.ds(r0, c), ls] * jnp.exp(g_last - g_wide),
                    s_decay=jnp.exp(g_last_col)))
        ws = [_dot(it['w2'], it['s2']) for it in items]
        qss = [_dot(it['qg'], it['s2']) for it in items]
        vnew2 = [it['u2'] - w for it, w in zip(items, ws)]
        avs = [_dot(it['attn'], v2) for it, v2 in zip(items, vnew2)]
        kvs = [_dot_tn(it['kdec'], v2[:c, :] + v2[c:, :]) for it, v2 in zip(items, vnew2)]
        for it, q_s, av, kv in zip(items, qss, avs, kvs):
            oacc_ref[pl.ds(it['r0'], c), it['ls']] = q_s + av
            s_ref[it['bi'], it['j']] = it['s2'] * it['s_decay'] + jnp.where(same_head, kv, 0.0)
        return carry

    lax.fori_loop(0, cps, recur, 0)

    o = oacc_ref[...]
    on = o * lax.rsqrt(_dot2(o * o, ones_h) * (1.0 / HEAD_DIM) + EPS) * nw_ref[...]
    o_ref[...] = (on * _silu(az_ref[...].reshape(rows_all, w_a))).reshape(nb, ts, w_a)

    @pl.when(t == nt - 1)
    def _():
        for bi in range(nb):
            for j in range(n_pairs):
                s2 = s_ref[bi, j]
                sout_ref[bi, 2 * j] = s2[:HEAD_DIM, :HEAD_DIM]
                sout_ref[bi, 2 * j + 1] = s2[HEAD_DIM:, HEAD_DIM:]


def _gdn(aqkv, az, small, bufpad, s0, conv_w, norm_w_tiled, valid_len, ts, nb):
    b, l, w3 = aqkv.shape
    w_a = w3 // 3
    n_pairs = w_a // PAIR
    heads = s0.shape[1]
    n_chunks = nb * ts // GDN_CHUNK
    group = math.gcd(n_chunks, 4)
    assert l % ts == 0 and ts % GDN_CHUNK == 0 and heads == 2 * n_pairs and b % nb == 0
    rows_all = nb * ts
    scratch = [pltpu.VMEM((nb, n_pairs, PAIR, PAIR), F32), pltpu.VMEM((nb, SUBLANES, w3), F32)]
    scratch += [pltpu.VMEM((rows_all, w_a), F32) for _ in range(6)]
    scratch += [pltpu.VMEM((rows_all, LANES), F32), pltpu.VMEM((rows_all, w_a), F32),
                pltpu.VMEM((rows_all, w3), F32),
                pltpu.VMEM((n_chunks, n_pairs, PAIR, 2 * PAIR), F32),
                pltpu.VMEM((n_chunks, n_pairs, GDN_CHUNK, PAIR), F32)]
    return pl.pallas_call(
        functools.partial(_gdn_kernel, n_pairs=n_pairs, valid_len=valid_len, group=group),
        grid=(b // nb, l // ts),
        in_specs=[pl.BlockSpec((nb, ts, w3), lambda i, t: (i, t, 0)),
                  pl.BlockSpec((nb, ts, w_a), lambda i, t: (i, t, 0)),
                  pl.BlockSpec((nb, ts, LANES), lambda i, t: (i, t, 0)),
                  pl.BlockSpec((nb, SUBLANES, w3), lambda i, t: (i, 0, 0)),
                  pl.BlockSpec((nb, heads, HEAD_DIM, HEAD_DIM), lambda i, t: (i, 0, 0, 0)),
                  pl.BlockSpec(conv_w.shape, lambda i, t: (0, 0)),
                  pl.BlockSpec(norm_w_tiled.shape, lambda i, t: (0, 0))],
        out_specs=[pl.BlockSpec((nb, ts, w_a), lambda i, t: (i, t, 0)),
                   pl.BlockSpec((nb, heads, HEAD_DIM, HEAD_DIM), lambda i, t: (i, 0, 0, 0))],
        out_shape=[jax.ShapeDtypeStruct((b, l, w_a), F32),
                   jax.ShapeDtypeStruct((b, heads, HEAD_DIM, HEAD_DIM), F32)],
        scratch_shapes=scratch,
        compiler_params=_cparams(("arbitrary", "arbitrary")),
        name="gdn",
    )(aqkv, az, small, bufpad, s0, conv_w, norm_w_tiled)


def _fox_prompt_kernel(q_ref, kt_ref, vt_ref, cumt_ref, nw_ref, o_ref, *, n_pairs):
    i = pl.program_id(1)
    tq = q_ref.shape[1]
    tk = tq
    scale = HEAD_DIM ** -0.5
    lane_lo = _iota((tq, PAIR), 1) < HEAD_DIM
    causal = _iota((tq, tk), 1) <= _iota((tq, tk), 0)
    q0 = pl.multiple_of(i * tq, tq)
    pairs = []
    for j in range(n_pairs):
        ls = slice(j * PAIR, (j + 1) * PAIR)
        q_p = q_ref[0, :, ls] * scale
        heads = (2 * j, 2 * j + 1)
        q_hs = [jnp.where(lane_lo, q_p, 0.0).astype(BF16), jnp.where(lane_lo, 0.0, q_p).astype(BF16)]
        c_refs = [cumt_ref[h:h + 1, pl.ds(q0, LANES)][:, 0:1] for h in heads]

        def block(k0, carry, masked, q_hs=q_hs, c_refs=c_refs, heads=heads, ls=ls):
            k_blk = kt_ref[0, 0, ls, pl.ds(k0, tk)].astype(BF16)
            v_blk = vt_ref[0, 0, ls, pl.ds(k0, tk)].astype(BF16)
            ss = [jnp.dot(q_h, k_blk, preferred_element_type=F32) for q_h in q_hs]
            ss = [s + (c_ref - cumt_ref[h:h + 1, pl.ds(k0, tk)]) for s, c_ref, h in zip(ss, c_refs, heads)]
            if masked:
                ss = [jnp.where(causal, s, NEG_INF) for s in ss]
            m_news = [jnp.maximum(st[0], jnp.max(s, axis=-1, keepdims=True)) for st, s in zip(carry, ss)]
            ps = [jnp.exp(s - m_new) for s, m_new in zip(ss, m_news)]
            alphas = [jnp.exp(st[0] - m_new) for st, m_new in zip(carry, m_news)]
            pvs = [lax.dot_general(p.astype(BF16), v_blk, (((1,), (1,)), ((), ())), preferred_element_type=F32)
                   for p in ps]
            return tuple((m_new, a * st[1] + jnp.sum(p, axis=-1, keepdims=True), a * st[2] + pv)
                         for st, m_new, a, p, pv in zip(carry, m_news, alphas, ps, pvs))

        init = tuple((jnp.full((tq, 1), NEG_INF, F32), jnp.zeros((tq, 1), F32), jnp.zeros((tq, PAIR), F32))
                     for _ in heads)
        carry = lax.fori_loop(
            0, i, lambda kb, c_, block=block: block(pl.multiple_of(kb * tk, tk), c_, False), init)
        outs = [acc / l for _, l, acc in block(q0, carry, True)]
        pairs.append(jnp.where(lane_lo, outs[0], outs[1]))
    o = jnp.concatenate(pairs, axis=1)
    ones_h = _head_ones(o.shape[1])
    o_ref[0] = o * lax.rsqrt(_dot2(o * o, ones_h) * (1.0 / HEAD_DIM) + EPS) * nw_ref[...]


def _fox_prompt(layer, q, kt, vt, cumt, norm_w_tiled, tq):
    b, l, w_b = q.shape
    assert l % tq == 0 and tq % LANES == 0
    return pl.pallas_call(
        functools.partial(_fox_prompt_kernel, n_pairs=w_b // PAIR),
        grid=(b, l // tq),
        in_specs=[pl.BlockSpec((1, tq, w_b), lambda i, t: (i, t, 0)),
                  pl.BlockSpec((1, 1, w_b, l), lambda i, t: (layer, i, 0, 0)),
                  pl.BlockSpec((1, 1, w_b, l), lambda i, t: (layer, i, 0, 0)),
                  pl.BlockSpec((SUBLANES, l), lambda i, t: (0, i)),
                  pl.BlockSpec(norm_w_tiled.shape, lambda i, t: (0, 0))],
        out_specs=pl.BlockSpec((1, tq, w_b), lambda i, t: (i, t, 0)),
        out_shape=jax.ShapeDtypeStruct((b, l, w_b), F32),
        compiler_params=_cparams(("arbitrary", "arbitrary")),
        name="fox_prompt",
    )(q, kt, vt, cumt, norm_w_tiled)


def _page_suffix_kernel(x_ref, m_ref, o_ref):
    n_heads = x_ref.shape[1]
    for h in range(SUBLANES):
        if h < n_heads:
            o_ref[0, :, h, :] = _dot01_right(x_ref[0, h], m_ref[...])
        else:
            o_ref[0, :, h, :] = jnp.zeros((o_ref.shape[1], o_ref.shape[3]), F32)


def _page_suffix(logf_t, tp):
    depth, n_heads, n_phys, page = logf_t.shape
    assert page == LANES and n_phys % tp == 0
    idx = jnp.arange(page)
    m = (idx[:, None] >= idx[None, :]).astype(BF16)
    return pl.pallas_call(
        _page_suffix_kernel,
        grid=(depth, n_phys // tp),
        in_specs=[pl.BlockSpec((1, n_heads, tp, page), lambda d, i: (d, 0, i, 0)),
                  pl.BlockSpec(m.shape, lambda d, i: (0, 0))],
        out_specs=pl.BlockSpec((1, tp, SUBLANES, page), lambda d, i: (d, i, 0, 0)),
        out_shape=jax.ShapeDtypeStruct((depth, n_phys, SUBLANES, page), F32),
        compiler_params=_cparams(("arbitrary", "arbitrary")),
        name="page_suffix",
    )(logf_t, m)


def _fox_decode_kernel(pt_ref, q_ref, kn_ref, vn_ref, cn_ref, nw_ref, kt_hbm, vt_hbm, sfx_hbm, o_ref,
                       kbuf, vbuf, sbuf, ksem, vsem, ssem, qbd_ref, m_ref, l_ref, acc_ref, carry_ref,
                       *, layer, n_new, pps, sub_pages):
    bb = pl.program_id(0)
    j = pl.program_id(1)
    nj = pl.num_programs(1)
    n_tok = q_ref.shape[1]
    w_b = q_ref.shape[2]
    rows = n_tok * SUBLANES
    page = kbuf.shape[3]
    n_pages = nj * pps
    scale = HEAD_DIM ** -0.5

    step = bb * nj + j
    n_steps = pl.num_programs(0) * nj
    slot = step % 2

    def page_copies(step_x, slot_x, i):
        pg = pt_ref[step_x // nj, n_pages - 1 - ((step_x % nj) * pps + i)]
        return (pltpu.make_async_copy(kt_hbm.at[layer, pg], kbuf.at[slot_x, i], ksem.at[slot_x]),
                pltpu.make_async_copy(vt_hbm.at[layer, pg], vbuf.at[slot_x, i], vsem.at[slot_x]),
                pltpu.make_async_copy(sfx_hbm.at[layer, pg], sbuf.at[slot_x, i], ssem.at[slot_x]))

    def request(step_x, slot_x):
        for i in range(pps):
            for cp in page_copies(step_x, slot_x, i):
                cp.start()

    @pl.when(step == 0)
    def _():
        request(step, slot)

    @pl.when(step + 1 < n_steps)
    def _():
        request(step + 1, 1 - slot)

    for i in range(pps):
        for cp in page_copies(step, slot, i):
            cp.wait()

    @pl.when(j == 0)
    def _():
        q = q_ref[0] * scale
        headmask = (_iota((SUBLANES, w_b), 1) // HEAD_DIM == _iota((SUBLANES, w_b), 0)).astype(F32)
        qbd = jnp.concatenate([q[t:t + 1, :] * headmask for t in range(n_tok)], axis=0)
        qbd_ref[...] = qbd.astype(BF16)
        kn = kn_ref[0]
        s = _dot_nt(qbd, kn)
        cn = cn_ref[0]
        s = s - jnp.concatenate([cn] * n_tok, axis=0)
        kidx = _iota((rows, SUBLANES), 1)
        tidx = _iota((rows, SUBLANES), 0) // SUBLANES
        s = jnp.where((kidx <= tidx) & (kidx < n_new), s, NEG_INF)
        m = jnp.max(s, axis=-1, keepdims=True)
        p = jnp.exp(s - m)
        m_ref[...] = m
        l_ref[...] = jnp.sum(p, axis=-1, keepdims=True)
        acc_ref[...] = _dot(p, vn_ref[0])
        carry_ref[...] = jnp.zeros_like(carry_ref)

    lane = _iota((SUBLANES, page), 1)
    carry = carry_ref[...]
    biases = []
    for i in range(pps):
        incl = sbuf[slot, i]
        excl = jnp.where(lane < page - 1, pltpu.roll(incl, page - 1, axis=1), 0.0)
        biases.append(excl + carry)
        carry = carry + incl[:, 0:1]
    carry_ref[...] = carry
    qbd = qbd_ref[...]
    n_sub = pps // sub_pages
    subs = [range(u * sub_pages, (u + 1) * sub_pages) for u in range(n_sub)]
    kts = [jnp.concatenate([kbuf[slot, i].astype(BF16) for i in sb], axis=1) for sb in subs]
    ss = [jnp.dot(qbd, kt, preferred_element_type=F32) for kt in kts]
    ss = [s + jnp.concatenate([jnp.concatenate([biases[i] for i in sb], axis=1)] * n_tok, axis=0)
          for s, sb in zip(ss, subs)]
    ms = [jnp.max(s, axis=-1, keepdims=True) for s in ss]
    ps = [jnp.exp(s - m) for s, m in zip(ss, ms)]
    ls = [jnp.sum(p, axis=-1, keepdims=True) for p in ps]
    vts = [jnp.concatenate([vbuf[slot, i].astype(BF16) for i in sb], axis=1) for sb in subs]
    pvs = [lax.dot_general(p.astype(BF16), vt, (((1,), (1,)), ((), ())), preferred_element_type=F32)
           for p, vt in zip(ps, vts)]
    m_old = m_ref[...]
    m_new = m_old
    for m in ms:
        m_new = jnp.maximum(m_new, m)
    alpha = jnp.exp(m_old - m_new)
    l_new = alpha * l_ref[...]
    acc = alpha * acc_ref[...]
    for m, l_u, pv in zip(ms, ls, pvs):
        w_u = jnp.exp(m - m_new)
        l_new = l_new + w_u * l_u
        acc = acc + w_u * pv
    m_ref[...] = m_new
    l_ref[...] = l_new
    acc_ref[...] = acc

    @pl.when(j == nj - 1)
    def _():
        o = acc_ref[...] / l_ref[...]
        diag = (_iota((rows, w_b), 1) // HEAD_DIM) == (_iota((rows, w_b), 0) % SUBLANES)
        o = jnp.where(diag, o, 0.0)
        group = (_iota((SUBLANES, rows), 1) // SUBLANES == _iota((SUBLANES, rows), 0)).astype(F32)
        o_tok = jnp.dot(group, o, precision=HIGHEST, preferred_element_type=F32)
        ones_h = _head_ones(w_b)
        ss = jnp.dot(o_tok * o_tok, ones_h, precision=HIGHEST, preferred_element_type=F32)
        on = o_tok * lax.rsqrt(ss * (1.0 / HEAD_DIM) + EPS) * nw_ref[...]
        o_ref[0] = on[0:n_tok, :]


def _fox_decode(layer, page_table, q, k_new8, v_new8, cn8, norm_w_tiled, cache_kt, cache_vt, suffix, n_new):
    b, n_tok, w_b = q.shape
    page = cache_kt.shape[3]
    n_pages = page_table.shape[1]
    pps = math.gcd(n_pages, FOX_PAGES_PER_STEP)
    sub_pages = math.gcd(pps, FOX_PAGES_PER_SUBBLOCK)
    rows = n_tok * SUBLANES
    n_slots = 2
    in_specs = [pl.BlockSpec((1, n_tok, w_b), lambda bb, j, pt: (bb, 0, 0)),
                pl.BlockSpec((1, SUBLANES, w_b), lambda bb, j, pt: (bb, 0, 0)),
                pl.BlockSpec((1, SUBLANES, w_b), lambda bb, j, pt: (bb, 0, 0)),
                pl.BlockSpec((1, SUBLANES, SUBLANES), lambda bb, j, pt: (bb, 0, 0)),
                pl.BlockSpec(norm_w_tiled.shape, lambda bb, j, pt: (0, 0)),
                pl.BlockSpec(memory_space=pl.ANY), pl.BlockSpec(memory_space=pl.ANY),
                pl.BlockSpec(memory_space=pl.ANY)]
    grid_spec = pltpu.PrefetchScalarGridSpec(
        num_scalar_prefetch=1,
        grid=(b, n_pages // pps),
        in_specs=in_specs,
        out_specs=pl.BlockSpec((1, n_tok, w_b), lambda bb, j, pt: (bb, 0, 0)),
        scratch_shapes=[pltpu.VMEM((n_slots, pps, w_b, page), F32), pltpu.VMEM((n_slots, pps, w_b, page), F32),
                        pltpu.VMEM((n_slots, pps, SUBLANES, page), F32),
                        pltpu.SemaphoreType.DMA((n_slots,)), pltpu.SemaphoreType.DMA((n_slots,)),
                        pltpu.SemaphoreType.DMA((n_slots,)),
                        pltpu.VMEM((rows, w_b), BF16), pltpu.VMEM((rows, 1), F32), pltpu.VMEM((rows, 1), F32),
                        pltpu.VMEM((rows, w_b), F32), pltpu.VMEM((SUBLANES, page), F32)])
    return pl.pallas_call(
        functools.partial(_fox_decode_kernel, layer=layer, n_new=n_new, pps=pps, sub_pages=sub_pages),
        grid_spec=grid_spec,
        out_shape=jax.ShapeDtypeStruct((b, n_tok, w_b), F32),
        compiler_params=_cparams(("arbitrary", "arbitrary")),
        name="fox_decode",
    )(page_table, q, k_new8, v_new8, cn8, norm_w_tiled, cache_kt, cache_vt, suffix)


def _rms(x, w):
    return x * lax.rsqrt(jnp.mean(x * x, axis=-1, keepdims=True) + EPS) * w


def _gelu_tanh(x):
    return x * (0.5 * (1.0 + jnp.tanh(math.sqrt(2.0 / math.pi) * (x + 0.044715 * (x * x * x)))))


def _mix_ffn_kernel(x_ref, oa_ref, ob_ref, cuv_ref, cvw_ref, wm_ref, bse_ref, wo_ref, nf_ref,
                    wg_ref, wu_ref, cw_ref, wd_ref, pm_ref, nfin_ref,
                    y_ref, tail_ref, cv_ref, carry_ref,
                    *, seq_len, dff_chunk, final_norm, multi_seq):
    t = pl.program_id(1)
    tm = x_ref.shape[0]
    w_a, w_b = oa_ref.shape[1], ob_ref.shape[1]
    w_c = cuv_ref.shape[1] // 2
    d_ff = wg_ref.shape[1]
    tail_rows = tail_ref.shape[1]

    gel = _gelu_tanh(cuv_ref[...])
    cu, cv = gel[:, :w_c], gel[:, w_c:]
    ones_h = _head_ones(w_c)
    cvn = cv * lax.rsqrt(_dot2(cv * cv, ones_h) * (1.0 / HEAD_DIM) + EPS) * cvw_ref[...]
    cv_ref[...] = cvn
    cvn_b = cvn.astype(BF16)
    lane_lo = _iota((LANES, PAIR), 1) < HEAD_DIM
    row_blocks = []
    for n in range(tm // LANES):
        rs = slice(n * LANES, (n + 1) * LANES)
        pair_blocks = []
        for j in range(w_c // PAIR):
            ls = slice(j * PAIR, (j + 1) * PAIR)
            v_p = cvn_b[rs, ls]
            mix_a = jnp.dot(wm_ref[2 * j], v_p, preferred_element_type=F32)
            mix_b = jnp.dot(wm_ref[2 * j + 1], v_p, preferred_element_type=F32)
            pair_blocks.append(cu[rs, ls] * (jnp.where(lane_lo, mix_a, mix_b) + bse_ref[:, ls]))
        row_blocks.append(jnp.concatenate(pair_blocks, axis=1))
    o_c = jnp.concatenate(row_blocks, axis=0)

    x1 = (x_ref[...]
          + jnp.dot(oa_ref[...].astype(BF16), wo_ref[0:w_a, :], preferred_element_type=F32)
          + jnp.dot(ob_ref[...].astype(BF16), wo_ref[w_a:w_a + w_b, :], preferred_element_type=F32)
          + jnp.dot(o_c.astype(BF16), wo_ref[w_a + w_b:, :], preferred_element_type=F32))
    h2 = _rms(x1, nf_ref[...]).astype(BF16)

    row = _iota((tm, 1), 0)
    if multi_seq:
        pos = row % seq_len
    else:
        pos = row

        @pl.when(t == 0)
        def _():
            carry_ref[...] = pm_ref[0]

    def project(c0):
        cs = slice(c0, c0 + dff_chunk)
        return (jnp.dot(h2, wg_ref[:, cs], preferred_element_type=F32),
                jnp.dot(h2, wu_ref[:, cs], preferred_element_type=F32))

    acc = jnp.zeros((tm, x_ref.shape[1]), F32)
    nxt = project(0)
    for c0 in range(0, d_ff, dff_chunk):
        cs = slice(c0, c0 + dff_chunk)
        gp, up = nxt
        if c0 + dff_chunk < d_ff:
            nxt = project(c0 + dff_chunk)
        if multi_seq:
            first1, first2 = pm_ref[0, :, cs], pm_ref[1, :, cs]
        else:
            c6, c7 = carry_ref[SUBLANES - 2:SUBLANES - 1, cs], carry_ref[SUBLANES - 1:SUBLANES, cs]
            first1 = c7
            first2 = jnp.where(row == 0, c6, c7)
            carry_ref[:, cs] = gp[tm - SUBLANES:tm, :]
        gm1 = jnp.where(pos >= 1, pltpu.roll(gp, 1, axis=0), first1)
        gm2 = jnp.where(pos >= 2, pltpu.roll(gp, 2, axis=0), first2)
        gate = cw_ref[0:1, cs] * gm2 + cw_ref[1:2, cs] * gm1 + cw_ref[2:3, cs] * gp
        act = (_silu(gate) * up).astype(BF16)
        acc = acc + jnp.dot(act, wd_ref[cs, :], preferred_element_type=F32)
        tail_ref[0, :, cs] = gp[tm - tail_rows:tm, :]
    x2 = x1 + acc
    if final_norm:
        x2 = _rms(x2, nfin_ref[...])
    y_ref[...] = x2


def _mix_ffn(x, oa, ob, cuv, cvw, wm, bse, wo, nf, wg, wu, cw, wd, pm, nfin, seq_len, tm, final_norm):
    n, d = x.shape
    multi_seq = seq_len < tm
    assert n % tm == 0 and (tm % seq_len == 0 if multi_seq else seq_len % tm == 0)
    assert tm % LANES == 0 and wm.shape[1] == LANES
    n_seq = 1 if multi_seq else n // seq_len
    seq_tiles = 1 if multi_seq else seq_len // tm
    n_tiles = n // tm
    d_ff = wg.shape[1]
    dff_chunk = 256
    assert d_ff % dff_chunk == 0
    tail_rows = tm if multi_seq else SUBLANES
    w_c = cuv.shape[1] // 2

    def rows(w):
        return pl.BlockSpec((tm, w), lambda s, t: (s * seq_tiles + t, 0))

    def const(a):
        nd = a.ndim
        return pl.BlockSpec(a.shape, lambda s, t: (0,) * nd, pipeline_mode=pl.Buffered(1))

    if multi_seq:
        pm_spec = pl.BlockSpec((2, tm, d_ff), lambda s, t: (0, s * seq_tiles + t, 0))
        grid = (n_tiles, 1)
    else:
        pm_spec = pl.BlockSpec((1, SUBLANES, d_ff), lambda s, t: (s, 0, 0))
        grid = (n_seq, seq_tiles)
    n_tail = n_tiles if multi_seq else n_seq
    return pl.pallas_call(
        functools.partial(_mix_ffn_kernel, seq_len=seq_len, dff_chunk=dff_chunk, final_norm=final_norm,
                          multi_seq=multi_seq),
        grid=grid,
        in_specs=[rows(d), rows(oa.shape[1]), rows(ob.shape[1]), rows(cuv.shape[1]),
                  const(cvw), const(wm), const(bse), const(wo), const(nf), const(wg), const(wu), const(cw),
                  const(wd), pm_spec, const(nfin)],
        out_specs=[rows(d),
                   pl.BlockSpec((1, tail_rows, d_ff), lambda s, t: (s, 0, 0)),
                   rows(w_c)],
        out_shape=[jax.ShapeDtypeStruct((n, d), F32),
                   jax.ShapeDtypeStruct((n_tail, tail_rows, d_ff), F32),
                   jax.ShapeDtypeStruct((n, w_c), F32)],
        scratch_shapes=[pltpu.VMEM((SUBLANES, d_ff), F32)],
        compiler_params=_cparams(("arbitrary", "arbitrary")),
        name="mix_ffn",
    )(x, oa, ob, cuv, cvw, wm, bse, wo, nf, wg, wu, cw, wd, pm, nfin)


def _pad_axis(a, axis, before, after):
    pads = [(0, 0)] * a.ndim
    pads[axis] = (before, after)
    return jnp.pad(a, pads)


def _lanes_block(parts):
    lead = parts[0][1].shape[:-1]
    out = jnp.zeros(lead + (LANES,), F32)
    for off, a in parts:
        out = lax.dynamic_update_slice_in_dim(out, a.astype(F32), off, axis=a.ndim - 1)
    return out


def kernel(x_prompt, x_sample, cache_fox_k, cache_fox_v, cache_fox_logf, page_table, state_gdn, state_gdn_conv,
           state_ffn_conv, norm_mix, w_in, gdn_conv_w, gdn_a_log, gdn_dt_bias, gdn_norm_w, fox_f_bias,
           fox_out_norm, chunk_v_norm, chunk_ws, chunk_bs, w_out, norm_ffn, ffn_w_gate, ffn_w_up, ffn_conv_w,
           ffn_w_down, norm_final):
    depth, d_model, _ = w_in.shape
    b_p, seq, _ = x_prompt.shape
    b_s, t_s, _ = x_sample.shape
    gh, fh, cg = gdn_a_log.shape[1], fox_f_bias.shape[1], chunk_ws.shape[1]
    assert gdn_norm_w.shape[1] == HEAD_DIM and gh % 2 == 0 and fh % 2 == 0 and cg % 2 == 0
    w_a, w_b, w_c = gh * HEAD_DIM, fh * HEAD_DIM, cg * HEAD_DIM
    d_ff = ffn_w_gate.shape[2]
    conv_k = gdn_conv_w.shape[1]
    ffn_k = ffn_conv_w.shape[1]
    chunk_len = chunk_ws.shape[2]
    n_phys, page = cache_fox_k.shape[1], cache_fox_k.shape[2]
    assert gh <= SUBLANES and fh <= SUBLANES and conv_k - 1 <= t_s <= GDN_CHUNK and ffn_k == 3
    assert chunk_len == LANES and seq % chunk_len == 0 and t_s <= chunk_len

    sizes = [3 * w_a, w_a, gh, gh, 3 * w_b, fh, 2 * w_c]
    starts = [0]
    for s in sizes:
        starts.append(starts[-1] + s)
    i_aqkv, i_az, i_beta, i_alpha, i_bqkv, i_bf, i_cuv = starts[:-1]
    offs, pos = {}, 0
    for name, width in [('a_qkv', 3 * w_a), ('a_z', w_a), ('b_q', w_b), ('b_k', w_b), ('b_v', w_b),
                        ('c_uv', 2 * w_c), ('small', LANES)]:
        offs[name] = (pos, pos + width)
        pos += width

    xp = x_prompt.reshape(b_p * seq, d_model)
    xs = x_sample.reshape(b_s * t_s, d_model)
    cache_kt = cache_fox_k.transpose(0, 1, 3, 4, 2).reshape(depth, n_phys, w_b, page)
    cache_vt = cache_fox_v.transpose(0, 1, 3, 4, 2).reshape(depth, n_phys, w_b, page)
    suffix = _page_suffix(cache_fox_logf.transpose(0, 3, 1, 2), math.gcd(n_phys, 512))

    tri = jnp.tril(jnp.ones((chunk_len, chunk_len), bool))
    tri_s = jnp.tril(jnp.ones((t_s, t_s), bool))
    n_rep = chunk_len // t_s
    assert (b_s * t_s) % chunk_len == 0 and chunk_len % t_s == 0

    prompt_rows, sample_rows = [], []
    w_in_t = w_in.transpose(2, 0, 1)
    kt_all = vt_all = None
    logft_rows = []
    for l in range(depth):
        w = w_in_t[:, l, :]

        def zrows(n):
            return jnp.zeros((n, d_model), F32)

        small_w = jnp.concatenate([
            w[i_beta:i_beta + gh], zrows(SM_G - SM_BETA - gh),
            w[i_alpha:i_alpha + gh], zrows(SM_F - SM_G - gh),
            w[i_bf:i_bf + fh], zrows(LANES - SM_F - fh)], axis=0)
        w_r = jnp.concatenate([w[i_aqkv:i_aqkv + 3 * w_a], w[i_az:i_az + w_a],
                               w[i_bqkv:i_bqkv + 3 * w_b], w[i_cuv:i_cuv + 2 * w_c], small_w],
                              axis=0).astype(BF16)
        par = jnp.concatenate([
            _lanes_block([(SM_G, gdn_dt_bias[l][None, :]), (SM_F, fox_f_bias[l][None, :])]),
            _lanes_block([(SM_G, gdn_a_log[l][None, :])]),
            jnp.zeros((SUBLANES - 2, LANES), F32)], axis=0)
        nm = norm_mix[l][None, :]
        gnw = jnp.tile(gdn_norm_w[l], gh)[None, :]
        fnw = jnp.tile(fox_out_norm[l], fh)[None, :]
        cvw = jnp.tile(chunk_v_norm[l], cg)[None, :]
        wo = w_out[l].astype(BF16)
        wg, wu, wd = ffn_w_gate[l].astype(BF16), ffn_w_up[l].astype(BF16), ffn_w_down[l].astype(BF16)
        nf = norm_ffn[l][None, :]
        nfin = norm_final[None, :]
        last = l == depth - 1

        aqkv, az, bq, kt_all, vt_all, cuv, small, cumt, logft = _inproj(
            xp, nm, w_r, par, offs, seq, 512, kv_stack=(l, depth, kt_all, vt_all))
        logft_rows.append(logft)
        o_a, s_new = _gdn(aqkv.reshape(b_p, seq, 3 * w_a), az.reshape(b_p, seq, w_a),
                          small.reshape(b_p, seq, LANES), jnp.zeros((b_p, SUBLANES, 3 * w_a), F32),
                          jnp.zeros((b_p, gh, HEAD_DIM, HEAD_DIM), F32), gdn_conv_w[l], gnw, seq, 512, 1)
        o_b = _fox_prompt(l, bq.reshape(b_p, seq, w_b), kt_all, vt_all, cumt, fnw, 512)
        wm = jnp.where(tri, chunk_ws[l], 0.0).astype(BF16)
        bse = jnp.repeat(chunk_bs[l].T, HEAD_DIM, axis=1)
        xp, tail, _ = _mix_ffn(xp, o_a.reshape(b_p * seq, w_a), o_b.reshape(b_p * seq, w_b), cuv, cvw, wm, bse,
                               wo, nf, wg, wu, ffn_conv_w[l], wd, jnp.zeros((b_p, SUBLANES, d_ff), F32), nfin,
                               seq, 512, last)
        prompt_rows.append((None, None, None, s_new,
                            aqkv.reshape(b_p, seq, 3 * w_a)[:, seq - (conv_k - 1):, :],
                            tail[:, SUBLANES - (ffn_k - 1):, :]))

        n_s = b_s * t_s
        aqkv, az, bq, bk, bv, cuv, small, cumt, _ = _inproj(xs, nm, w_r, par, offs, t_s, n_s)
        pad_t = GDN_CHUNK - t_s
        o_a, s_new = _gdn(_pad_axis(aqkv.reshape(b_s, t_s, 3 * w_a), 1, 0, pad_t),
                          _pad_axis(az.reshape(b_s, t_s, w_a), 1, 0, pad_t),
                          _pad_axis(small.reshape(b_s, t_s, LANES), 1, 0, pad_t),
                          _pad_axis(state_gdn_conv[l], 1, SUBLANES - (conv_k - 1), 0),
                          state_gdn[l], gdn_conv_w[l], gnw, t_s, GDN_CHUNK, math.gcd(b_s, 4))
        o_a = o_a[:, :t_s, :]
        cn8 = _pad_axis(_pad_axis(cumt.reshape(SUBLANES, b_s, t_s).transpose(1, 0, 2), 2, 0, SUBLANES - t_s),
                        1, 0, 0)
        o_b = _fox_decode(l, page_table, bq.reshape(b_s, t_s, w_b),
                          _pad_axis(bk.reshape(b_s, t_s, w_b), 1, 0, SUBLANES - t_s),
                          _pad_axis(bv.reshape(b_s, t_s, w_b), 1, 0, SUBLANES - t_s),
                          cn8, fnw, cache_kt, cache_vt, suffix, t_s)
        eye_rep = jnp.eye(n_rep, dtype=F32)
        wm_s = jnp.stack([jnp.kron(eye_rep, jnp.where(tri_s, chunk_ws[l, g, :t_s, :t_s], 0.0))
                          for g in range(cg)]).astype(BF16)
        bse_s = jnp.repeat(jnp.tile(chunk_bs[l, :, :t_s], (1, n_rep)).T, HEAD_DIM, axis=1)
        buf = state_ffn_conv[l]
        zrow = jnp.zeros((b_s, 1, d_ff), F32)
        pm1 = jnp.concatenate([buf[:, 1:2]] + [zrow] * (t_s - 1), axis=1).reshape(n_s, d_ff)
        pm2 = jnp.concatenate([buf[:, 0:1], buf[:, 1:2]] + [zrow] * (t_s - 2), axis=1).reshape(n_s, d_ff)
        xs, tail, cvn = _mix_ffn(xs, o_a.reshape(n_s, w_a), o_b.reshape(n_s, w_b), cuv, cvw, wm_s, bse_s,
                                 wo, nf, wg, wu, ffn_conv_w[l], wd, jnp.stack([pm1, pm2]), nfin,
                                 t_s, chunk_len, last)
        sample_rows.append((bk.reshape(b_s, t_s, fh, HEAD_DIM), bv.reshape(b_s, t_s, fh, HEAD_DIM),
                            small[:, SM_F:SM_F + fh].reshape(b_s, t_s, fh), s_new,
                            aqkv.reshape(b_s, t_s, 3 * w_a)[:, t_s - (conv_k - 1):, :],
                            tail.reshape(b_s, t_s, d_ff)[:, t_s - (ffn_k - 1):, :],
                            cvn.reshape(b_s, t_s, w_c)))

    def stack(rows_, i):
        return jnp.stack([r[i] for r in rows_])

    y_prompt = xp.reshape(b_p, seq, d_model)
    y_sample = xs.reshape(b_s, t_s, d_model)
    fox_k_prompt = kt_all.reshape(depth, b_p, fh, HEAD_DIM, seq).transpose(0, 1, 4, 2, 3)
    fox_v_prompt = vt_all.reshape(depth, b_p, fh, HEAD_DIM, seq).transpose(0, 1, 4, 2, 3)
    fox_logf_prompt = jnp.stack(logft_rows)[:, :fh, :].reshape(depth, fh, b_p, seq).transpose(0, 2, 3, 1)
    return (y_prompt, y_sample,
            fox_k_prompt, fox_v_prompt, fox_logf_prompt, stack(prompt_rows, 3),
            stack(prompt_rows, 4), stack(prompt_rows, 5),
            stack(sample_rows, 0), stack(sample_rows, 1), stack(sample_rows, 2), stack(sample_rows, 3),
            stack(sample_rows, 4), stack(sample_rows, 5), stack(sample_rows, 6))
```

```python
import functools
import math

import jax
import jax.numpy as jnp
from jax import lax
from jax.experimental import pallas as pl
from jax.experimental.pallas import tpu as pltpu

F32 = jnp.float32
BF16 = jnp.bfloat16
EPS = 1e-6
LANES = 128
SUBLANES = 8
HEAD_DIM = 64
PAIR = 2 * HEAD_DIM
GDN_CHUNK = 64
FOX_PAGES_PER_STEP = 8
FOX_PAGES_PER_SUBBLOCK = 4
VMEM_LIMIT = 56 * 1024 * 1024
HIGHEST = lax.Precision.HIGHEST
NEG_INF = -1e30

SM_BETA, SM_G, SM_F = 0, 8, 16


def _cparams(sem):
    return pltpu.CompilerParams(dimension_semantics=sem, vmem_limit_bytes=VMEM_LIMIT)


def _dot(a, b):
    return jnp.dot(a.astype(BF16), b.astype(BF16), preferred_element_type=F32)


def _dot_nt(a, b):
    return lax.dot_general(a.astype(BF16), b.astype(BF16), (((1,), (1,)), ((), ())),
                           preferred_element_type=F32)


def _dot_tn(a, b):
    return lax.dot_general(a.astype(BF16), b.astype(BF16), (((0,), (0,)), ((), ())),
                           preferred_element_type=F32)


def _dot2(a, b01):
    hi = a.astype(BF16)
    lo = (a - hi.astype(F32)).astype(BF16)
    b = b01.astype(BF16)
    return (jnp.dot(hi, b, preferred_element_type=F32) + jnp.dot(lo, b, preferred_element_type=F32))


def _split3(a):
    a1 = a.astype(BF16)
    r1 = a - a1.astype(F32)
    a2 = r1.astype(BF16)
    a3 = (r1 - a2.astype(F32)).astype(BF16)
    return a1, a2, a3


def _dot01_left(m01, a):
    m = m01.astype(BF16)
    t1, t2, t3 = _split3(a)
    return (jnp.dot(m, t1, preferred_element_type=F32) + jnp.dot(m, t2, preferred_element_type=F32)
            + jnp.dot(m, t3, preferred_element_type=F32))


def _dot01_right(a, m01):
    m = m01.astype(BF16)
    t1, t2, t3 = _split3(a)
    return (jnp.dot(t1, m, preferred_element_type=F32) + jnp.dot(t2, m, preferred_element_type=F32)
            + jnp.dot(t3, m, preferred_element_type=F32))


def _iota(shape, dim):
    return lax.broadcasted_iota(jnp.int32, shape, dim)


def _head_ones(n):
    return (_iota((n, n), 0) // HEAD_DIM == _iota((n, n), 1) // HEAD_DIM).astype(F32)


def _silu(x):
    return x * jax.nn.sigmoid(x)


def _inproj_kernel(x_ref, nw_ref, wt_ref, par_ref, *refs, offs, seq_tiles, seq_len, kv_transposed, kv_slot,
                   n_alias):
    (aqkv_ref, az_ref, bq_ref, bk_ref, bv_ref, cuv_ref, small_ref, cumt_ref, logft_ref,
     carry_ref) = refs[n_alias:]
    i = pl.program_id(0)
    tm = x_ref.shape[0]
    x = x_ref[...]
    h = (x * lax.rsqrt(jnp.mean(x * x, axis=-1, keepdims=True) + EPS) * nw_ref[...]).astype(BF16)

    y = lax.dot_general(h, wt_ref[...], (((1,), (1,)), ((), ())), preferred_element_type=F32)

    def mm(name):
        lo, hi = offs[name]
        return y[:, lo:hi]

    aqkv_ref[...] = mm('a_qkv')
    az_ref[...] = mm('a_z')
    bq_ref[...] = mm('b_q')
    if kv_transposed:
        bk_ref[kv_slot, 0] = mm('b_k').T
        bv_ref[kv_slot, 0] = mm('b_v').T
        for d in range(bk_ref.shape[0]):
            if d != kv_slot:
                bk_ref[d, 0] = jnp.zeros(bk_ref.shape[2:], F32)
                bv_ref[d, 0] = jnp.zeros(bv_ref.shape[2:], F32)
    else:
        bk_ref[...] = mm('b_k')
        bv_ref[...] = mm('b_v')
    cuv_ref[...] = mm('c_uv')

    z = mm('small') + par_ref[0:1, :]
    lane = _iota(z.shape, 1)
    t = jnp.log1p(jnp.exp(-jnp.abs(z)))
    softplus = jnp.maximum(z, 0.0) + t
    log_sig = jnp.minimum(z, 0.0) - t
    sm = jnp.where(lane < SM_G, jax.nn.sigmoid(z),
                   jnp.where(lane < SM_F, -jnp.exp(par_ref[1:2, :]) * softplus, log_sig))
    small_ref[...] = sm

    lb = min(seq_len, tm)
    row, col = _iota((tm, tm), 0), _iota((tm, tm), 1)
    tri = jnp.where((col <= row) & (row // lb == col // lb), 1.0, 0.0)
    cum = _dot01_left(tri, sm)
    if seq_tiles > 1:
        @pl.when(i % seq_tiles == 0)
        def _():
            carry_ref[...] = jnp.zeros_like(carry_ref)
        cum = cum + carry_ref[0:1, :]
        carry_ref[0:1, :] = cum[tm - 1:tm, :]
    cumt_ref[...] = cum.T[SM_F:SM_F + SUBLANES, :]
    logft_ref[...] = sm.T[SM_F:SM_F + SUBLANES, :]


def _inproj(x, nw, wt, par, offs, seq_len, tm, kv_stack=None):
    n, d = x.shape
    assert n % tm == 0 and (seq_len % tm == 0 or tm % seq_len == 0)
    seq_tiles = max(seq_len // tm, 1)
    widths = {k: hi - lo for k, (lo, hi) in offs.items()}
    names = ['a_qkv', 'a_z', 'b_q', 'b_k', 'b_v', 'c_uv', 'small']
    out_shape = [jax.ShapeDtypeStruct((n, widths[k]), F32) for k in names]
    out_specs = [pl.BlockSpec((tm, widths[k]), lambda i: (i, 0)) for k in names]
    out_shape += [jax.ShapeDtypeStruct((SUBLANES, n), F32)] * 2
    out_specs += [pl.BlockSpec((SUBLANES, tm), lambda i: (0, i))] * 2
    args = [x, nw, wt, par]
    in_specs = [pl.BlockSpec((tm, d), lambda i: (i, 0)),
                pl.BlockSpec((1, d), lambda i: (0, 0)),
                pl.BlockSpec(wt.shape, lambda i: (0, 0)),
                pl.BlockSpec(par.shape, lambda i: (0, 0))]
    aliases = {}
    kv_slot = 0
    if kv_stack is not None:
        layer, depth, kt_prev, vt_prev = kv_stack
        assert seq_len % tm == 0
        w_b = widths['b_k']
        if kt_prev is None:
            kv_slot, blk, first = layer, depth, 0
        else:
            kv_slot, blk, first = 0, 1, layer
            args += [kt_prev, vt_prev]
            in_specs += [pl.BlockSpec(memory_space=pl.ANY)] * 2
            aliases = {4: 3, 5: 4}
        for pos in (3, 4):
            out_shape[pos] = jax.ShapeDtypeStruct((depth, n // seq_len, w_b, seq_len), F32)
            out_specs[pos] = pl.BlockSpec((blk, 1, w_b, tm), lambda i: (first, i // seq_tiles, 0, i % seq_tiles))
    return pl.pallas_call(
        functools.partial(_inproj_kernel, offs=offs, seq_tiles=seq_tiles, seq_len=seq_len,
                          kv_transposed=kv_stack is not None, kv_slot=kv_slot, n_alias=len(aliases)),
        grid=(n // tm,),
        in_specs=in_specs,
        out_specs=out_specs,
        out_shape=out_shape,
        input_output_aliases=aliases,
        scratch_shapes=[pltpu.VMEM((SUBLANES, LANES), F32)],
        compiler_params=_cparams(("arbitrary",)),
        name="inproj",
    )(*args)


def _unit_lower_inverses(lmats):
    n = lmats[0].shape[0]
    row, col = _iota((n, n), 0), _iota((n, n), 1)
    eye = (row == col).astype(F32)
    base = SUBLANES
    in_base = row // base == col // base
    lds = [jnp.where(in_base, lm, 0.0) for lm in lmats]
    xs = [eye - ld for ld in lds]
    ps = [_dot(ld, ld) for ld in lds]
    xs = [x + _dot(x, p) for x, p in zip(xs, ps)]
    ps = [_dot(p, p) for p in ps]
    xs = [x + _dot(x, p) for x, p in zip(xs, ps)]
    s = base
    while s < HEAD_DIM:
        off_mask = (row // (2 * s) == col // (2 * s)) & (row // s != col // s)
        ts_ = [_dot(x, jnp.where(off_mask, lm, 0.0)) for x, lm in zip(xs, lmats)]
        xs = [x - _dot(t_, x) for x, t_ in zip(xs, ts_)]
        s *= 2
    return xs


def _gdn_kernel(aqkv_ref, az_ref, small_ref, buf_ref, s0_ref, cw_ref, nw_ref,
                o_ref, sout_ref,
                s_ref, prev_ref, qs_ref, qg_ref, k_ref, kb_ref, kbg_ref, vb_ref, gc_ref, oacc_ref,
                conv_ref, uw_ref, at_ref, *, n_pairs, valid_len, group):
    t = pl.program_id(1)
    nt = pl.num_programs(1)
    nb, ts, w3 = aqkv_ref.shape
    w_a = n_pairs * PAIR
    c = GDN_CHUNK
    rows_all = nb * ts
    cps = ts // c

    @pl.when(t == 0)
    def _():
        for bi in range(nb):
            for j in range(n_pairs):
                s_a = s0_ref[bi, 2 * j]
                s_b = s0_ref[bi, 2 * j + 1]
                top = jnp.concatenate([s_a, jnp.zeros_like(s_a)], axis=1)
                bot = jnp.concatenate([jnp.zeros_like(s_b), s_b], axis=1)
                s_ref[bi, j] = jnp.concatenate([top, bot], axis=0)
        prev_ref[...] = buf_ref[...]

    x = aqkv_ref[...].reshape(rows_all, w3)
    kw = cw_ref.shape[0]
    conv = cw_ref[kw - 1:kw, :] * x
    for i in range(1, kw):
        conv = conv + cw_ref[kw - 1 - i:kw - i, :] * pltpu.roll(x, i, axis=0)
    conv_ref[...] = conv
    for bi in range(nb):
        head = jnp.concatenate([prev_ref[bi], x[bi * ts:bi * ts + SUBLANES]], axis=0)
        fixed = cw_ref[0:1, :] * head[SUBLANES - kw + 1:2 * SUBLANES - kw + 1]
        for i in range(1, kw):
            fixed = fixed + cw_ref[i:i + 1, :] * head[SUBLANES - kw + 1 + i:2 * SUBLANES - kw + 1 + i]
        conv_ref[bi * ts:bi * ts + SUBLANES, :] = fixed
        prev_ref[bi] = x[(bi + 1) * ts - SUBLANES:(bi + 1) * ts]
    a = _silu(conv_ref[...])
    rowid = _iota((rows_all, 1), 0) % ts + t * ts
    valid = (rowid < valid_len).astype(F32)
    ones_h = _head_ones(w_a)
    q = a[:, 0:w_a]
    k = a[:, w_a:2 * w_a]
    v = a[:, 2 * w_a:3 * w_a] * valid
    qn = q * lax.rsqrt(_dot2(q * q, ones_h) + EPS) * valid
    kn = k * lax.rsqrt(_dot2(k * k, ones_h) + EPS) * valid

    sm = small_ref[...].reshape(rows_all, LANES) * valid
    row, col = _iota((rows_all, rows_all), 0), _iota((rows_all, rows_all), 1)
    tri = jnp.where((col <= row) & (row // c == col // c), 1.0, 0.0)
    gcum = _dot01_left(tri, sm)
    gc_ref[...] = gcum
    lane, hrow = _iota((LANES, w_a), 1), _iota((LANES, w_a), 0)
    exp_b = (hrow == lane // HEAD_DIM + SM_BETA).astype(F32)
    exp_g = (hrow == lane // HEAD_DIM + SM_G).astype(F32)
    beta_e = _dot2(sm, exp_b)
    gcum_e = _dot01_right(gcum, exp_g)
    eg = jnp.exp(gcum_e)
    scale = HEAD_DIM ** -0.5
    qs_ref[...] = qn * scale
    qg_ref[...] = qn * (scale * eg)
    k_ref[...] = kn
    kb = kn * beta_e
    kb_ref[...] = kb
    kbg_ref[...] = kb * eg
    vb_ref[...] = v * beta_e

    r2, c2 = _iota((PAIR, PAIR), 0), _iota((PAIR, PAIR), 1)
    same_head = (r2 // HEAD_DIM) == (c2 // HEAD_DIM)
    strict = same_head & (c2 < r2)
    rw, cw_ = _iota((c, PAIR), 0), _iota((c, PAIR), 1)
    incl_wide = (cw_ % HEAD_DIM) <= rw
    lane_lo = _iota((c, PAIR), 1) < HEAD_DIM

    def stack(m):
        return jnp.concatenate([jnp.where(lane_lo, m, 0.0), jnp.where(lane_lo, 0.0, m)], axis=0)

    def gate_cols(g_c, j):
        return g_c[:, SM_G + 2 * j:SM_G + 2 * j + 1], g_c[:, SM_G + 2 * j + 1:SM_G + 2 * j + 2]

    def prepare(gi, carry):
        items = []
        for i in range(group):
            ci = gi * group + i
            r0 = pl.multiple_of(ci * c, c)
            g_c = gc_ref[pl.ds(r0, c), :]
            g_t = g_c.T
            for j in range(n_pairs):
                ls = slice(j * PAIR, (j + 1) * PAIR)
                ga_col, gb_col = gate_cols(g_c, j)
                g_row = jnp.concatenate([g_t[SM_G + 2 * j:SM_G + 2 * j + 1, :],
                                         g_t[SM_G + 2 * j + 1:SM_G + 2 * j + 2, :]], axis=1)
                g_col2 = jnp.concatenate([ga_col, gb_col], axis=0)
                g_wide = jnp.where(lane_lo, ga_col, gb_col)
                items.append(dict(
                    ci=ci, j=j,
                    k2=stack(k_ref[pl.ds(r0, c), ls]),
                    kb2=stack(kb_ref[pl.ds(r0, c), ls]),
                    rhs=jnp.concatenate([stack(vb_ref[pl.ds(r0, c), ls]), stack(kbg_ref[pl.ds(r0, c), ls])],
                                        axis=1),
                    qs=qs_ref[pl.ds(r0, c), ls],
                    decay2=jnp.exp(jnp.minimum(g_col2 - g_row, 0.0)),
                    decay_w=jnp.exp(jnp.minimum(g_wide - g_row, 0.0))))
        kk = [_dot_nt(it['kb2'], it['k2']) for it in items]
        qk = [_dot_nt(it['qs'], it['k2']) for it in items]
        lmats = [jnp.where(strict, m * it['decay2'], 0.0) for m, it in zip(kk, items)]
        for m, it in zip(qk, items):
            at_ref[it['ci'], it['j']] = jnp.where(incl_wide, m * it['decay_w'], 0.0)
        tinvs = _unit_lower_inverses(lmats)
        sols = [_dot(tinv, it['rhs']) for tinv, it in zip(tinvs, items)]
        for sol, it in zip(sols, items):
            uw_ref[it['ci'], it['j']] = sol
        return carry

    lax.fori_loop(0, rows_all // c // group, prepare, 0)

    def recur(n, carry):
        items = []
        for bi in range(nb):
            ci = bi * cps + n
            r0 = pl.multiple_of(ci * c, c)
            g_c = gc_ref[pl.ds(r0, c), :]
            for j in range(n_pairs):
                ls = slice(j * PAIR, (j + 1) * PAIR)
                ga_col, gb_col = gate_cols(g_c, j)
                g_wide = jnp.where(lane_lo, ga_col, gb_col)
                g_last = jnp.where(lane_lo, ga_col[c - 1:c, :], gb_col[c - 1:c, :])
                g_last_col = jnp.where(_iota((PAIR, 1), 0) < HEAD_DIM, ga_col[c - 1:c, :], gb_col[c - 1:c, :])
                items.append(dict(
                    bi=bi, r0=r0, ls=ls, j=j,
                    u2=uw_ref[ci, j, :, 0:PAIR], w2=uw_ref[ci, j, :, PAIR:2 * PAIR], attn=at_ref[ci, j],
                    s2=s_ref[bi, j], qg=qg_ref[pl.ds(r0, c), ls],
                    kdec=k_ref[pl.ds(r0, c), ls] * jnp.exp(g_last - g_wide),
                    s_decay=jnp.exp(g_last_col)))
        ws = [_dot(it['w2'], it['s2']) for it in items]
        qss = [_dot(it['qg'], it['s2']) for it in items]
        vnew2 = [it['u2'] - w for it, w in zip(items, ws)]
        avs = [_dot(it['attn'], v2) for it, v2 in zip(items, vnew2)]
        kvs = [_dot_tn(it['kdec'], v2[:c, :] + v2[c:, :]) for it, v2 in zip(items, vnew2)]
        for it, q_s, av, kv in zip(items, qss, avs, kvs):
            oacc_ref[pl.ds(it['r0'], c), it['ls']] = q_s + av
            s_ref[it['bi'], it['j']] = it['s2'] * it['s_decay'] + jnp.where(same_head, kv, 0.0)
        return carry

    lax.fori_loop(0, cps, recur, 0)

    o = oacc_ref[...]
    on = o * lax.rsqrt(_dot2(o * o, ones_h) * (1.0 / HEAD_DIM) + EPS) * nw_ref[...]
    o_ref[...] = (on * _silu(az_ref[...].reshape(rows_all, w_a))).reshape(nb, ts, w_a)

    @pl.when(t == nt - 1)
    def _():
        for bi in range(nb):
            for j in range(n_pairs):
                s2 = s_ref[bi, j]
                sout_ref[bi, 2 * j] = s2[:HEAD_DIM, :HEAD_DIM]
                sout_ref[bi, 2 * j + 1] = s2[HEAD_DIM:, HEAD_DIM:]


def _gdn(aqkv, az, small, bufpad, s0, conv_w, norm_w_tiled, valid_len, ts, nb):
    b, l, w3 = aqkv.shape
    w_a = w3 // 3
    n_pairs = w_a // PAIR
    heads = s0.shape[1]
    n_chunks = nb * ts // GDN_CHUNK
    group = math.gcd(n_chunks, 4)
    assert l % ts == 0 and ts % GDN_CHUNK == 0 and heads == 2 * n_pairs and b % nb == 0
    rows_all = nb * ts
    scratch = [pltpu.VMEM((nb, n_pairs, PAIR, PAIR), F32), pltpu.VMEM((nb, SUBLANES, w3), F32)]
    scratch += [pltpu.VMEM((rows_all, w_a), F32) for _ in range(6)]
    scratch += [pltpu.VMEM((rows_all, LANES), F32), pltpu.VMEM((rows_all, w_a), F32),
                pltpu.VMEM((rows_all, w3), F32),
                pltpu.VMEM((n_chunks, n_pairs, PAIR, 2 * PAIR), F32),
                pltpu.VMEM((n_chunks, n_pairs, GDN_CHUNK, PAIR), F32)]
    return pl.pallas_call(
        functools.partial(_gdn_kernel, n_pairs=n_pairs, valid_len=valid_len, group=group),
        grid=(b // nb, l // ts),
        in_specs=[pl.BlockSpec((nb, ts, w3), lambda i, t: (i, t, 0)),
                  pl.BlockSpec((nb, ts, w_a), lambda i, t: (i, t, 0)),
                  pl.BlockSpec((nb, ts, LANES), lambda i, t: (i, t, 0)),
                  pl.BlockSpec((nb, SUBLANES, w3), lambda i, t: (i, 0, 0)),
                  pl.BlockSpec((nb, heads, HEAD_DIM, HEAD_DIM), lambda i, t: (i, 0, 0, 0)),
                  pl.BlockSpec(conv_w.shape, lambda i, t: (0, 0)),
                  pl.BlockSpec(norm_w_tiled.shape, lambda i, t: (0, 0))],
        out_specs=[pl.BlockSpec((nb, ts, w_a), lambda i, t: (i, t, 0)),
                   pl.BlockSpec((nb, heads, HEAD_DIM, HEAD_DIM), lambda i, t: (i, 0, 0, 0))],
        out_shape=[jax.ShapeDtypeStruct((b, l, w_a), F32),
                   jax.ShapeDtypeStruct((b, heads, HEAD_DIM, HEAD_DIM), F32)],
        scratch_shapes=scratch,
        compiler_params=_cparams(("arbitrary", "arbitrary")),
        name="gdn",
    )(aqkv, az, small, bufpad, s0, conv_w, norm_w_tiled)


def _fox_prompt_kernel(q_ref, kt_ref, vt_ref, cumt_ref, nw_ref, o_ref, *, n_pairs):
    i = pl.program_id(1)
    tq = q_ref.shape[1]
    tk = tq
    scale = HEAD_DIM ** -0.5
    lane_lo = _iota((tq, PAIR), 1) < HEAD_DIM
    causal = _iota((tq, tk), 1) <= _iota((tq, tk), 0)
    q0 = pl.multiple_of(i * tq, tq)
    pairs = []
    for j in range(n_pairs):
        ls = slice(j * PAIR, (j + 1) * PAIR)
        q_p = q_ref[0, :, ls] * scale
        heads = (2 * j, 2 * j + 1)
        q_hs = [jnp.where(lane_lo, q_p, 0.0).astype(BF16), jnp.where(lane_lo, 0.0, q_p).astype(BF16)]
        c_refs = [cumt_ref[h:h + 1, pl.ds(q0, LANES)][:, 0:1] for h in heads]

        def block(k0, carry, masked, q_hs=q_hs, c_refs=c_refs, heads=heads, ls=ls):
            k_blk = kt_ref[0, 0, ls, pl.ds(k0, tk)].astype(BF16)
            v_blk = vt_ref[0, 0, ls, pl.ds(k0, tk)].astype(BF16)
            ss = [jnp.dot(q_h, k_blk, preferred_element_type=F32) for q_h in q_hs]
            ss = [s + (c_ref - cumt_ref[h:h + 1, pl.ds(k0, tk)]) for s, c_ref, h in zip(ss, c_refs, heads)]
            if masked:
                ss = [jnp.where(causal, s, NEG_INF) for s in ss]
            m_news = [jnp.maximum(st[0], jnp.max(s, axis=-1, keepdims=True)) for st, s in zip(carry, ss)]
            ps = [jnp.exp(s - m_new) for s, m_new in zip(ss, m_news)]
            alphas = [jnp.exp(st[0] - m_new) for st, m_new in zip(carry, m_news)]
            pvs = [lax.dot_general(p.astype(BF16), v_blk, (((1,), (1,)), ((), ())), preferred_element_type=F32)
                   for p in ps]
            return tuple((m_new, a * st[1] + jnp.sum(p, axis=-1, keepdims=True), a * st[2] + pv)
                         for st, m_new, a, p, pv in zip(carry, m_news, alphas, ps, pvs))

        init = tuple((jnp.full((tq, 1), NEG_INF, F32), jnp.zeros((tq, 1), F32), jnp.zeros((tq, PAIR), F32))
                     for _ in heads)
        carry = lax.fori_loop(
            0, i, lambda kb, c_, block=block: block(pl.multiple_of(kb * tk, tk), c_, False), init)
        outs = [acc / l for _, l, acc in block(q0, carry, True)]
        pairs.append(jnp.where(lane_lo, outs[0], outs[1]))
    o = jnp.concatenate(pairs, axis=1)
    ones_h = _head_ones(o.shape[1])
    o_ref[0] = o * lax.rsqrt(_dot2(o * o, ones_h) * (1.0 / HEAD_DIM) + EPS) * nw_ref[...]


def _fox_prompt(layer, q, kt, vt, cumt, norm_w_tiled, tq):
    b, l, w_b = q.shape
    assert l % tq == 0 and tq % LANES == 0
    return pl.pallas_call(
        functools.partial(_fox_prompt_kernel, n_pairs=w_b // PAIR),
        grid=(b, l // tq),
        in_specs=[pl.BlockSpec((1, tq, w_b), lambda i, t: (i, t, 0)),
                  pl.BlockSpec((1, 1, w_b, l), lambda i, t: (layer, i, 0, 0)),
                  pl.BlockSpec((1, 1, w_b, l), lambda i, t: (layer, i, 0, 0)),
                  pl.BlockSpec((SUBLANES, l), lambda i, t: (0, i)),
                  pl.BlockSpec(norm_w_tiled.shape, lambda i, t: (0, 0))],
        out_specs=pl.BlockSpec((1, tq, w_b), lambda i, t: (i, t, 0)),
        out_shape=jax.ShapeDtypeStruct((b, l, w_b), F32),
        compiler_params=_cparams(("arbitrary", "arbitrary")),
        name="fox_prompt",
    )(q, kt, vt, cumt, norm_w_tiled)


def _page_suffix_kernel(x_ref, m_ref, o_ref):
    n_heads = x_ref.shape[1]
    for h in range(SUBLANES):
        if h < n_heads:
            o_ref[0, :, h, :] = _dot01_right(x_ref[0, h], m_ref[...])
        else:
            o_ref[0, :, h, :] = jnp.zeros((o_ref.shape[1], o_ref.shape[3]), F32)


def _page_suffix(logf_t, tp):
    depth, n_heads, n_phys, page = logf_t.shape
    assert page == LANES and n_phys % tp == 0
    idx = jnp.arange(page)
    m = (idx[:, None] >= idx[None, :]).astype(BF16)
    return pl.pallas_call(
        _page_suffix_kernel,
        grid=(depth, n_phys // tp),
        in_specs=[pl.BlockSpec((1, n_heads, tp, page), lambda d, i: (d, 0, i, 0)),
                  pl.BlockSpec(m.shape, lambda d, i: (0, 0))],
        out_specs=pl.BlockSpec((1, tp, SUBLANES, page), lambda d, i: (d, i, 0, 0)),
        out_shape=jax.ShapeDtypeStruct((depth, n_phys, SUBLANES, page), F32),
        compiler_params=_cparams(("arbitrary", "arbitrary")),
        name="page_suffix",
    )(logf_t, m)


def _run_if(cond, fn):
    if isinstance(cond, bool):
        if cond:
            fn()
    else:
        pl.when(cond)(fn)


def _decode_round_scores(step, n_steps, refs, *, layer, n_new, pps, sub_pages, nj, j_static, maybe_first,
                         maybe_last):
    (pt_ref, q_ref, kn_ref, vn_ref, cn_ref, nw_ref, kt_hbm, vt_hbm, sfx_hbm, o_ref,
     kbuf, vbuf, sbuf, ksem, vsem, ssem, qbd_ref, m_ref, l_ref, acc_ref, carry_ref) = refs
    bb = step // nj
    j = step % nj if j_static is None else j_static
    n_tok = q_ref.shape[1]
    w_b = q_ref.shape[2]
    rows = n_tok * SUBLANES
    page = kbuf.shape[3]
    n_pages = nj * pps
    scale = HEAD_DIM ** -0.5
    slot = step % 2

    def page_copies(step_x, slot_x, i):
        pg = pt_ref[step_x // nj, n_pages - 1 - ((step_x % nj) * pps + i)]
        return (pltpu.make_async_copy(kt_hbm.at[layer, pg], kbuf.at[slot_x, i], ksem.at[slot_x]),
                pltpu.make_async_copy(vt_hbm.at[layer, pg], vbuf.at[slot_x, i], vsem.at[slot_x]),
                pltpu.make_async_copy(sfx_hbm.at[layer, pg], sbuf.at[slot_x, i], ssem.at[slot_x]))

    def request(step_x, slot_x):
        for i in range(pps):
            for cp in page_copies(step_x, slot_x, i):
                cp.start()

    if maybe_first:
        _run_if(step == 0, lambda: request(step, slot))
    _run_if(step + 1 < n_steps if maybe_last else True, lambda: request(step + 1, 1 - slot))

    for i in range(pps):
        for cp in page_copies(step, slot, i):
            cp.wait()

    def start_sequence():
        q = q_ref[bb] * scale
        headmask = (_iota((SUBLANES, w_b), 1) // HEAD_DIM == _iota((SUBLANES, w_b), 0)).astype(F32)
        qbd = jnp.concatenate([q[t:t + 1, :] * headmask for t in range(n_tok)], axis=0)
        qbd_ref[...] = qbd.astype(BF16)
        kn = kn_ref[bb]
        s = _dot_nt(qbd, kn)
        cn = cn_ref[bb]
        s = s - jnp.concatenate([cn] * n_tok, axis=0)
        kidx = _iota((rows, SUBLANES), 1)
        tidx = _iota((rows, SUBLANES), 0) // SUBLANES
        s = jnp.where((kidx <= tidx) & (kidx < n_new), s, NEG_INF)
        m = jnp.max(s, axis=-1, keepdims=True)
        p = jnp.exp(s - m)
        m_ref[...] = m
        l_ref[...] = jnp.sum(p, axis=-1, keepdims=True)
        acc_ref[...] = _dot(p, vn_ref[bb])
        carry_ref[...] = jnp.zeros_like(carry_ref)

    _run_if(j == 0, start_sequence)

    lane = _iota((SUBLANES, page), 1)
    carry = carry_ref[...]
    biases = []
    for i in range(pps):
        incl = sbuf[slot, i]
        excl = jnp.where(lane < page - 1, pltpu.roll(incl, page - 1, axis=1), 0.0)
        biases.append(excl + carry)
        carry = carry + incl[:, 0:1]
    carry_ref[...] = carry
    qbd = qbd_ref[...]
    n_sub = pps // sub_pages
    subs = [range(u * sub_pages, (u + 1) * sub_pages) for u in range(n_sub)]
    kts = [jnp.concatenate([kbuf[slot, i].astype(BF16) for i in sb], axis=1) for sb in subs]
    ss = [jnp.dot(qbd, kt, preferred_element_type=F32) for kt in kts]
    ss = [s + jnp.concatenate([jnp.concatenate([biases[i] for i in sb], axis=1)] * n_tok, axis=0)
          for s, sb in zip(ss, subs)]
    ms = [jnp.max(s, axis=-1, keepdims=True) for s in ss]
    ps = [jnp.exp(s - m) for s, m in zip(ss, ms)]
    ls = [jnp.sum(p, axis=-1, keepdims=True) for p in ps]
    return dict(refs=refs, slot=slot, subs=subs, ms=ms, ps=ps, ls=ls, bb=bb, last_of_sequence=j == nj - 1)


def _decode_round_values(state):
    (_, q_ref, _, _, _, nw_ref, _, _, _, o_ref,
     _, vbuf, _, _, _, _, _, m_ref, l_ref, acc_ref, _) = state['refs']
    slot, subs, ms, ps, ls, bb = (state[k] for k in ('slot', 'subs', 'ms', 'ps', 'ls', 'bb'))
    n_tok = q_ref.shape[1]
    w_b = q_ref.shape[2]
    rows = n_tok * SUBLANES
    vts = [jnp.concatenate([vbuf[slot, i].astype(BF16) for i in sb], axis=1) for sb in subs]
    pvs = [lax.dot_general(p.astype(BF16), vt, (((1,), (1,)), ((), ())), preferred_element_type=F32)
           for p, vt in zip(ps, vts)]
    m_old = m_ref[...]
    m_new = m_old
    for m in ms:
        m_new = jnp.maximum(m_new, m)
    alpha = jnp.exp(m_old - m_new)
    l_new = alpha * l_ref[...]
    acc = alpha * acc_ref[...]
    for m, l_u, pv in zip(ms, ls, pvs):
        w_u = jnp.exp(m - m_new)
        l_new = l_new + w_u * l_u
        acc = acc + w_u * pv
    m_ref[...] = m_new
    l_ref[...] = l_new
    acc_ref[...] = acc

    def finish_sequence():
        o = acc_ref[...] / l_ref[...]
        diag = (_iota((rows, w_b), 1) // HEAD_DIM) == (_iota((rows, w_b), 0) % SUBLANES)
        o = jnp.where(diag, o, 0.0)
        group = (_iota((SUBLANES, rows), 1) // SUBLANES == _iota((SUBLANES, rows), 0)).astype(F32)
        o_tok = jnp.dot(group, o, precision=HIGHEST, preferred_element_type=F32)
        ones_h = _head_ones(w_b)
        ss = jnp.dot(o_tok * o_tok, ones_h, precision=HIGHEST, preferred_element_type=F32)
        on = o_tok * lax.rsqrt(ss * (1.0 / HEAD_DIM) + EPS) * nw_ref[...]
        o_ref[bb] = on[0:n_tok, :]

    _run_if(state['last_of_sequence'], finish_sequence)


N_DECODE_INPUTS = 9
N_DECODE_SCRATCH = 11


def _decode_plumbing(layer, page_table, q, k_new8, v_new8, cn8, norm_w_tiled, cache_kt, cache_vt, suffix, n_new,
                     n_hosts):
    b, n_tok, w_b = q.shape
    page = cache_kt.shape[3]
    n_pages = page_table.shape[1]
    pps = math.gcd(n_pages, FOX_PAGES_PER_STEP)
    sub_pages = math.gcd(pps, FOX_PAGES_PER_SUBBLOCK)
    nj = n_pages // pps
    assert (b * nj) % n_hosts == 0
    rows = n_tok * SUBLANES
    n_slots = 2

    def whole(a):
        nd = a.ndim
        return pl.BlockSpec(a.shape, lambda *_: (0,) * nd)

    args = [page_table, q, k_new8, v_new8, cn8, norm_w_tiled, cache_kt, cache_vt, suffix]
    in_specs = [pl.BlockSpec(memory_space=pltpu.SMEM), whole(q), whole(k_new8), whole(v_new8), whole(cn8),
                whole(norm_w_tiled), pl.BlockSpec(memory_space=pl.ANY), pl.BlockSpec(memory_space=pl.ANY),
                pl.BlockSpec(memory_space=pl.ANY)]
    assert len(args) == N_DECODE_INPUTS
    out_shape = jax.ShapeDtypeStruct((b, n_tok, w_b), F32)
    out_spec = pl.BlockSpec((b, n_tok, w_b), lambda *_: (0, 0, 0))
    scratch = [pltpu.VMEM((n_slots, pps, w_b, page), F32), pltpu.VMEM((n_slots, pps, w_b, page), F32),
               pltpu.VMEM((n_slots, pps, SUBLANES, page), F32),
               pltpu.SemaphoreType.DMA((n_slots,)), pltpu.SemaphoreType.DMA((n_slots,)),
               pltpu.SemaphoreType.DMA((n_slots,)),
               pltpu.VMEM((rows, w_b), BF16), pltpu.VMEM((rows, 1), F32), pltpu.VMEM((rows, 1), F32),
               pltpu.VMEM((rows, w_b), F32), pltpu.VMEM((SUBLANES, page), F32)]
    assert len(scratch) == N_DECODE_SCRATCH
    config = dict(layer=layer, n_new=n_new, pps=pps, sub_pages=sub_pages, nj=nj,
                  rounds_per_host=b * nj // n_hosts, n_steps=b * nj)
    return args, in_specs, out_shape, out_spec, scratch, config


def _rms(x, w):
    return x * lax.rsqrt(jnp.mean(x * x, axis=-1, keepdims=True) + EPS) * w


def _gelu_tanh(x):
    return x * (0.5 * (1.0 + jnp.tanh(math.sqrt(2.0 / math.pi) * (x + 0.044715 * (x * x * x)))))


def _mix_ffn_kernel(x_ref, oa_ref, ob_ref, cuv_ref, cvw_ref, wm_ref, bse_ref, wo_ref, nf_ref,
                    wg_ref, wu_ref, cw_ref, wd_ref, pm_ref, nfin_ref, *refs,
                    seq_len, dff_chunk, final_norm, multi_seq, decode):
    if decode is None:
        y_ref, tail_ref, cv_ref, carry_ref = refs
        decode_refs = None
    else:
        dec_in, rest = refs[:N_DECODE_INPUTS], refs[N_DECODE_INPUTS:]
        y_ref, tail_ref, cv_ref, dec_out, carry_ref = rest[:5]
        decode_refs = tuple(dec_in) + (dec_out,) + tuple(rest[5:])
    t = pl.program_id(1)
    tm = x_ref.shape[0]
    w_a, w_b = oa_ref.shape[1], ob_ref.shape[1]
    w_c = cuv_ref.shape[1] // 2
    d_ff = wg_ref.shape[1]
    tail_rows = tail_ref.shape[1]

    gel = _gelu_tanh(cuv_ref[...])
    cu, cv = gel[:, :w_c], gel[:, w_c:]
    ones_h = _head_ones(w_c)
    cvn = cv * lax.rsqrt(_dot2(cv * cv, ones_h) * (1.0 / HEAD_DIM) + EPS) * cvw_ref[...]
    cv_ref[...] = cvn
    cvn_b = cvn.astype(BF16)
    lane_lo = _iota((LANES, PAIR), 1) < HEAD_DIM
    row_blocks = []
    for n in range(tm // LANES):
        rs = slice(n * LANES, (n + 1) * LANES)
        pair_blocks = []
        for j in range(w_c // PAIR):
            ls = slice(j * PAIR, (j + 1) * PAIR)
            v_p = cvn_b[rs, ls]
            mix_a = jnp.dot(wm_ref[2 * j], v_p, preferred_element_type=F32)
            mix_b = jnp.dot(wm_ref[2 * j + 1], v_p, preferred_element_type=F32)
            pair_blocks.append(cu[rs, ls] * (jnp.where(lane_lo, mix_a, mix_b) + bse_ref[:, ls]))
        row_blocks.append(jnp.concatenate(pair_blocks, axis=1))
    o_c = jnp.concatenate(row_blocks, axis=0)

    mix = jnp.concatenate([oa_ref[...].astype(BF16), ob_ref[...].astype(BF16), o_c.astype(BF16)], axis=1)
    x1 = x_ref[...] + jnp.dot(mix, wo_ref[...], preferred_element_type=F32)
    h2 = _rms(x1, nf_ref[...]).astype(BF16)

    row = _iota((tm, 1), 0)
    if multi_seq:
        pos = row % seq_len
    else:
        pos = row

        @pl.when(t == 0)
        def _():
            carry_ref[...] = pm_ref[0]

    def project(c0):
        cs = slice(c0, c0 + dff_chunk)
        return (jnp.dot(h2, wg_ref[:, cs], preferred_element_type=F32),
                jnp.dot(h2, wu_ref[:, cs], preferred_element_type=F32))

    n_chunks = d_ff // dff_chunk
    rounds_at = [[] for _ in range(n_chunks)]
    if decode is not None:
        rph = decode['rounds_per_host']
        first_round = (pl.program_id(0) * pl.num_programs(1) + t) * rph
        for r in range(rph):
            rounds_at[r * n_chunks // rph].append(r)
        round_cfg = {k: decode[k] for k in ('layer', 'n_new', 'pps', 'sub_pages', 'nj')}
        whole_sequences = rph % decode['nj'] == 0

    acc = jnp.zeros((tm, x_ref.shape[1]), F32)
    nxt = project(0)
    for ci, c0 in enumerate(range(0, d_ff, dff_chunk)):
        cs = slice(c0, c0 + dff_chunk)
        gp, up = nxt
        def scores(r):
            return _decode_round_scores(first_round + r, decode['n_steps'], decode_refs,
                                        j_static=r % decode['nj'] if whole_sequences else None,
                                        maybe_first=r == 0, maybe_last=r == rph - 1, **round_cfg)

        hosted = rounds_at[ci]
        state = scores(hosted[0]) if hosted else None
        if c0 + dff_chunk < d_ff:
            nxt = project(c0 + dff_chunk)
        if hosted:
            _decode_round_values(state)
        for r in hosted[1:]:
            _decode_round_values(scores(r))
        if multi_seq:
            first1, first2 = pm_ref[0, :, cs], pm_ref[1, :, cs]
        else:
            c6, c7 = carry_ref[SUBLANES - 2:SUBLANES - 1, cs], carry_ref[SUBLANES - 1:SUBLANES, cs]
            first1 = c7
            first2 = jnp.where(row == 0, c6, c7)
            carry_ref[:, cs] = gp[tm - SUBLANES:tm, :]
        gm1 = jnp.where(pos >= 1, pltpu.roll(gp, 1, axis=0), first1)
        gm2 = jnp.where(pos >= 2, pltpu.roll(gp, 2, axis=0), first2)
        gate = cw_ref[0:1, cs] * gm2 + cw_ref[1:2, cs] * gm1 + cw_ref[2:3, cs] * gp
        act = (_silu(gate) * up).astype(BF16)
        acc = acc + jnp.dot(act, wd_ref[cs, :], preferred_element_type=F32)
        tail_ref[0, :, cs] = gp[tm - tail_rows:tm, :]
    x2 = x1 + acc
    if final_norm:
        x2 = _rms(x2, nfin_ref[...])
    y_ref[...] = x2


def _mix_ffn(x, oa, ob, cuv, cvw, wm, bse, wo, nf, wg, wu, cw, wd, pm, nfin, seq_len, tm, final_norm,
             decode_operands=None):
    n, d = x.shape
    multi_seq = seq_len < tm
    assert n % tm == 0 and (tm % seq_len == 0 if multi_seq else seq_len % tm == 0)
    assert tm % LANES == 0 and wm.shape[1] == LANES
    n_seq = 1 if multi_seq else n // seq_len
    seq_tiles = 1 if multi_seq else seq_len // tm
    n_tiles = n // tm
    d_ff = wg.shape[1]
    dff_chunk = 256
    assert d_ff % dff_chunk == 0
    tail_rows = tm if multi_seq else SUBLANES
    w_c = cuv.shape[1] // 2

    def rows(w):
        return pl.BlockSpec((tm, w), lambda s, t: (s * seq_tiles + t, 0))

    def const(a):
        nd = a.ndim
        return pl.BlockSpec(a.shape, lambda s, t: (0,) * nd, pipeline_mode=pl.Buffered(1))

    if multi_seq:
        pm_spec = pl.BlockSpec((2, tm, d_ff), lambda s, t: (0, s * seq_tiles + t, 0))
        grid = (n_tiles, 1)
    else:
        pm_spec = pl.BlockSpec((1, SUBLANES, d_ff), lambda s, t: (s, 0, 0))
        grid = (n_seq, seq_tiles)
    n_tail = n_tiles if multi_seq else n_seq
    args = [x, oa, ob, cuv, cvw, wm, bse, wo, nf, wg, wu, cw, wd, pm, nfin]
    in_specs = [rows(d), rows(oa.shape[1]), rows(ob.shape[1]), rows(cuv.shape[1]),
                const(cvw), const(wm), const(bse), const(wo), const(nf), const(wg), const(wu), const(cw),
                const(wd), pm_spec, const(nfin)]
    out_specs = [rows(d), pl.BlockSpec((1, tail_rows, d_ff), lambda s, t: (s, 0, 0)), rows(w_c)]
    out_shape = [jax.ShapeDtypeStruct((n, d), F32),
                 jax.ShapeDtypeStruct((n_tail, tail_rows, d_ff), F32),
                 jax.ShapeDtypeStruct((n, w_c), F32)]
    scratch = [pltpu.VMEM((SUBLANES, d_ff), F32)]
    decode = None
    if decode_operands is not None:
        d_args, d_in_specs, d_out_shape, d_out_spec, d_scratch, decode = _decode_plumbing(
            *decode_operands, n_hosts=n_tiles)
        args += d_args
        in_specs += d_in_specs
        out_specs.append(d_out_spec)
        out_shape.append(d_out_shape)
        scratch += d_scratch
    return pl.pallas_call(
        functools.partial(_mix_ffn_kernel, seq_len=seq_len, dff_chunk=dff_chunk, final_norm=final_norm,
                          multi_seq=multi_seq, decode=decode),
        grid=grid,
        in_specs=in_specs,
        out_specs=out_specs,
        out_shape=out_shape,
        scratch_shapes=scratch,
        compiler_params=_cparams(("arbitrary", "arbitrary")),
        name="mix_ffn",
    )(*args)


def _pad_axis(a, axis, before, after):
    pads = [(0, 0)] * a.ndim
    pads[axis] = (before, after)
    return jnp.pad(a, pads)


def _lanes_block(parts):
    lead = parts[0][1].shape[:-1]
    out = jnp.zeros(lead + (LANES,), F32)
    for off, a in parts:
        out = lax.dynamic_update_slice_in_dim(out, a.astype(F32), off, axis=a.ndim - 1)
    return out


def kernel(x_prompt, x_sample, cache_fox_k, cache_fox_v, cache_fox_logf, page_table, state_gdn, state_gdn_conv,
           state_ffn_conv, norm_mix, w_in, gdn_conv_w, gdn_a_log, gdn_dt_bias, gdn_norm_w, fox_f_bias,
           fox_out_norm, chunk_v_norm, chunk_ws, chunk_bs, w_out, norm_ffn, ffn_w_gate, ffn_w_up, ffn_conv_w,
           ffn_w_down, norm_final):
    depth, d_model, _ = w_in.shape
    b_p, seq, _ = x_prompt.shape
    b_s, t_s, _ = x_sample.shape
    gh, fh, cg = gdn_a_log.shape[1], fox_f_bias.shape[1], chunk_ws.shape[1]
    assert gdn_norm_w.shape[1] == HEAD_DIM and gh % 2 == 0 and fh % 2 == 0 and cg % 2 == 0
    w_a, w_b, w_c = gh * HEAD_DIM, fh * HEAD_DIM, cg * HEAD_DIM
    d_ff = ffn_w_gate.shape[2]
    conv_k = gdn_conv_w.shape[1]
    ffn_k = ffn_conv_w.shape[1]
    chunk_len = chunk_ws.shape[2]
    n_phys, page = cache_fox_k.shape[1], cache_fox_k.shape[2]
    assert gh <= SUBLANES and fh <= SUBLANES and conv_k - 1 <= t_s <= GDN_CHUNK and ffn_k == 3
    assert chunk_len == LANES and seq % chunk_len == 0 and t_s <= chunk_len

    sizes = [3 * w_a, w_a, gh, gh, 3 * w_b, fh, 2 * w_c]
    starts = [0]
    for s in sizes:
        starts.append(starts[-1] + s)
    i_aqkv, i_az, i_beta, i_alpha, i_bqkv, i_bf, i_cuv = starts[:-1]
    offs, pos = {}, 0
    for name, width in [('a_qkv', 3 * w_a), ('a_z', w_a), ('b_q', w_b), ('b_k', w_b), ('b_v', w_b),
                        ('c_uv', 2 * w_c), ('small', LANES)]:
        offs[name] = (pos, pos + width)
        pos += width

    xp = x_prompt.reshape(b_p * seq, d_model)
    xs = x_sample.reshape(b_s * t_s, d_model)
    cache_kt = cache_fox_k.transpose(0, 1, 3, 4, 2).reshape(depth, n_phys, w_b, page)
    cache_vt = cache_fox_v.transpose(0, 1, 3, 4, 2).reshape(depth, n_phys, w_b, page)
    suffix = _page_suffix(cache_fox_logf.transpose(0, 3, 1, 2), math.gcd(n_phys, 512))

    tri = jnp.tril(jnp.ones((chunk_len, chunk_len), bool))
    tri_s = jnp.tril(jnp.ones((t_s, t_s), bool))
    n_rep = chunk_len // t_s
    assert (b_s * t_s) % chunk_len == 0 and chunk_len % t_s == 0

    prompt_rows, sample_rows = [], []
    w_in_t = w_in.transpose(2, 0, 1)
    kt_all = vt_all = None
    logft_rows = []
    for l in range(depth):
        w = w_in_t[:, l, :]

        def zrows(n):
            return jnp.zeros((n, d_model), F32)

        small_w = jnp.concatenate([
            w[i_beta:i_beta + gh], zrows(SM_G - SM_BETA - gh),
            w[i_alpha:i_alpha + gh], zrows(SM_F - SM_G - gh),
            w[i_bf:i_bf + fh], zrows(LANES - SM_F - fh)], axis=0)
        w_r = jnp.concatenate([w[i_aqkv:i_aqkv + 3 * w_a], w[i_az:i_az + w_a],
                               w[i_bqkv:i_bqkv + 3 * w_b], w[i_cuv:i_cuv + 2 * w_c], small_w],
                              axis=0).astype(BF16)
        par = jnp.concatenate([
            _lanes_block([(SM_G, gdn_dt_bias[l][None, :]), (SM_F, fox_f_bias[l][None, :])]),
            _lanes_block([(SM_G, gdn_a_log[l][None, :])]),
            jnp.zeros((SUBLANES - 2, LANES), F32)], axis=0)
        nm = norm_mix[l][None, :]
        gnw = jnp.tile(gdn_norm_w[l], gh)[None, :]
        fnw = jnp.tile(fox_out_norm[l], fh)[None, :]
        cvw = jnp.tile(chunk_v_norm[l], cg)[None, :]
        wo = w_out[l].astype(BF16)
        wg, wu, wd = ffn_w_gate[l].astype(BF16), ffn_w_up[l].astype(BF16), ffn_w_down[l].astype(BF16)
        nf = norm_ffn[l][None, :]
        nfin = norm_final[None, :]
        last = l == depth - 1

        aqkv, az, bq, kt_all, vt_all, cuv, small, cumt, logft = _inproj(
            xp, nm, w_r, par, offs, seq, 512, kv_stack=(l, depth, kt_all, vt_all))
        logft_rows.append(logft)
        o_a, s_new = _gdn(aqkv.reshape(b_p, seq, 3 * w_a), az.reshape(b_p, seq, w_a),
                          small.reshape(b_p, seq, LANES), jnp.zeros((b_p, SUBLANES, 3 * w_a), F32),
                          jnp.zeros((b_p, gh, HEAD_DIM, HEAD_DIM), F32), gdn_conv_w[l], gnw, seq, 512, 1)
        o_b = _fox_prompt(l, bq.reshape(b_p, seq, w_b), kt_all, vt_all, cumt, fnw, 512)
        wm = jnp.where(tri, chunk_ws[l], 0.0).astype(BF16)
        bse = jnp.repeat(chunk_bs[l].T, HEAD_DIM, axis=1)
        n_s = b_s * t_s
        aqkv_p, cuv_p = aqkv, cuv
        aqkv, az, bq, bk, bv, cuv, small, cumt, _ = _inproj(xs, nm, w_r, par, offs, t_s, n_s)
        cn8 = _pad_axis(cumt.reshape(SUBLANES, b_s, t_s).transpose(1, 0, 2), 2, 0, SUBLANES - t_s)
        decode_operands = (l, page_table, bq.reshape(b_s, t_s, w_b),
                           _pad_axis(bk.reshape(b_s, t_s, w_b), 1, 0, SUBLANES - t_s),
                           _pad_axis(bv.reshape(b_s, t_s, w_b), 1, 0, SUBLANES - t_s),
                           cn8, fnw, cache_kt, cache_vt, suffix, t_s)
        xp, tail, _, o_b = _mix_ffn(xp, o_a.reshape(b_p * seq, w_a), o_b.reshape(b_p * seq, w_b), cuv_p, cvw, wm,
                                    bse, wo, nf, wg, wu, ffn_conv_w[l], wd,
                                    jnp.zeros((b_p, SUBLANES, d_ff), F32), nfin, seq, 512, last,
                                    decode_operands=decode_operands)
        prompt_rows.append((None, None, None, s_new,
                            aqkv_p.reshape(b_p, seq, 3 * w_a)[:, seq - (conv_k - 1):, :],
                            tail[:, SUBLANES - (ffn_k - 1):, :]))

        pad_t = GDN_CHUNK - t_s
        o_a, s_new = _gdn(_pad_axis(aqkv.reshape(b_s, t_s, 3 * w_a), 1, 0, pad_t),
                          _pad_axis(az.reshape(b_s, t_s, w_a), 1, 0, pad_t),
                          _pad_axis(small.reshape(b_s, t_s, LANES), 1, 0, pad_t),
                          _pad_axis(state_gdn_conv[l], 1, SUBLANES - (conv_k - 1), 0),
                          state_gdn[l], gdn_conv_w[l], gnw, t_s, GDN_CHUNK, math.gcd(b_s, 4))
        o_a = o_a[:, :t_s, :]
        eye_rep = jnp.eye(n_rep, dtype=F32)
        wm_s = jnp.stack([jnp.kron(eye_rep, jnp.where(tri_s, chunk_ws[l, g, :t_s, :t_s], 0.0))
                          for g in range(cg)]).astype(BF16)
        bse_s = jnp.repeat(jnp.tile(chunk_bs[l, :, :t_s], (1, n_rep)).T, HEAD_DIM, axis=1)
        buf = state_ffn_conv[l]
        zrow = jnp.zeros((b_s, 1, d_ff), F32)
        pm1 = jnp.concatenate([buf[:, 1:2]] + [zrow] * (t_s - 1), axis=1).reshape(n_s, d_ff)
        pm2 = jnp.concatenate([buf[:, 0:1], buf[:, 1:2]] + [zrow] * (t_s - 2), axis=1).reshape(n_s, d_ff)
        xs, tail, cvn = _mix_ffn(xs, o_a.reshape(n_s, w_a), o_b.reshape(n_s, w_b), cuv, cvw, wm_s, bse_s,
                                 wo, nf, wg, wu, ffn_conv_w[l], wd, jnp.stack([pm1, pm2]), nfin,
                                 t_s, chunk_len, last)
        sample_rows.append((bk.reshape(b_s, t_s, fh, HEAD_DIM), bv.reshape(b_s, t_s, fh, HEAD_DIM),
                            small[:, SM_F:SM_F + fh].reshape(b_s, t_s, fh), s_new,
                            aqkv.reshape(b_s, t_s, 3 * w_a)[:, t_s - (conv_k - 1):, :],
                            tail.reshape(b_s, t_s, d_ff)[:, t_s - (ffn_k - 1):, :],
                            cvn.reshape(b_s, t_s, w_c)))

    def stack(rows_, i):
        return jnp.stack([r[i] for r in rows_])

    y_prompt = xp.reshape(b_p, seq, d_model)
    y_sample = xs.reshape(b_s, t_s, d_model)
    fox_k_prompt = kt_all.reshape(depth, b_p, fh, HEAD_DIM, seq).transpose(0, 1, 4, 2, 3)
    fox_v_prompt = vt_all.reshape(depth, b_p, fh, HEAD_DIM, seq).transpose(0, 1, 4, 2, 3)
    fox_logf_prompt = jnp.stack(logft_rows)[:, :fh, :].reshape(depth, fh, b_p, seq).transpose(0, 2, 3, 1)
    return (y_prompt, y_sample,
            fox_k_prompt, fox_v_prompt, fox_logf_prompt, stack(prompt_rows, 3),
            stack(prompt_rows, 4), stack(prompt_rows, 5),
            stack(sample_rows, 0), stack(sample_rows, 1), stack(sample_rows, 2), stack(sample_rows, 3),
            stack(sample_rows, 4), stack(sample_rows, 5), stack(sample_rows, 6))
```

```python
import functools
import math

import jax
import jax.numpy as jnp
from jax import lax
from jax.experimental import pallas as pl
from jax.experimental.pallas import tpu as pltpu

F32 = jnp.float32
BF16 = jnp.bfloat16
EPS = 1e-6
LANES = 128
SUBLANES = 8
HEAD_DIM = 64
PAIR = 2 * HEAD_DIM
GDN_CHUNK = 64
FOX_PAGES_PER_STEP = 8
FOX_PAGES_PER_SUBBLOCK = 4
VMEM_LIMIT = 56 * 1024 * 1024
HIGHEST = lax.Precision.HIGHEST
NEG_INF = -1e30

SM_BETA, SM_G, SM_F = 0, 8, 16


def _cparams(sem):
    return pltpu.CompilerParams(dimension_semantics=sem, vmem_limit_bytes=VMEM_LIMIT)


def _dot(a, b):
    return jnp.dot(a.astype(BF16), b.astype(BF16), preferred_element_type=F32)


def _dot_nt(a, b):
    return lax.dot_general(a.astype(BF16), b.astype(BF16), (((1,), (1,)), ((), ())),
                           preferred_element_type=F32)


def _dot_tn(a, b):
    return lax.dot_general(a.astype(BF16), b.astype(BF16), (((0,), (0,)), ((), ())),
                           preferred_element_type=F32)


def _dot2(a, b01):
    hi = a.astype(BF16)
    lo = (a - hi.astype(F32)).astype(BF16)
    b = b01.astype(BF16)
    return (jnp.dot(hi, b, preferred_element_type=F32) + jnp.dot(lo, b, preferred_element_type=F32))


def _split3(a):
    a1 = a.astype(BF16)
    r1 = a - a1.astype(F32)
    a2 = r1.astype(BF16)
    a3 = (r1 - a2.astype(F32)).astype(BF16)
    return a1, a2, a3


def _dot01_left(m01, a):
    m = m01.astype(BF16)
    t1, t2, t3 = _split3(a)
    return (jnp.dot(m, t1, preferred_element_type=F32) + jnp.dot(m, t2, preferred_element_type=F32)
            + jnp.dot(m, t3, preferred_element_type=F32))


def _dot01_right(a, m01):
    m = m01.astype(BF16)
    t1, t2, t3 = _split3(a)
    return (jnp.dot(t1, m, preferred_element_type=F32) + jnp.dot(t2, m, preferred_element_type=F32)
            + jnp.dot(t3, m, preferred_element_type=F32))


def _iota(shape, dim):
    return lax.broadcasted_iota(jnp.int32, shape, dim)


def _head_ones(n):
    return (_iota((n, n), 0) // HEAD_DIM == _iota((n, n), 1) // HEAD_DIM).astype(F32)


def _silu(x):
    return x * jax.nn.sigmoid(x)


def _inproj_kernel(x_ref, nw_ref, wt_ref, par_ref, *refs, offs, seq_tiles, seq_len, kv_transposed, kv_slot,
                   n_alias):
    (aqkv_ref, az_ref, bq_ref, bk_ref, bv_ref, cuv_ref, small_ref, cumt_ref, logft_ref,
     carry_ref) = refs[n_alias:]
    i = pl.program_id(0)
    tm = x_ref.shape[0]
    x = x_ref[...]
    h = (x * lax.rsqrt(jnp.mean(x * x, axis=-1, keepdims=True) + EPS) * nw_ref[...]).astype(BF16)

    y = lax.dot_general(h, wt_ref[...], (((1,), (1,)), ((), ())), preferred_element_type=F32)

    def mm(name):
        lo, hi = offs[name]
        return y[:, lo:hi]

    aqkv_ref[...] = mm('a_qkv')
    az_ref[...] = mm('a_z')
    bq_ref[...] = mm('b_q')
    if kv_transposed:
        bk_ref[kv_slot, 0] = mm('b_k').T
        bv_ref[kv_slot, 0] = mm('b_v').T
        for d in range(bk_ref.shape[0]):
            if d != kv_slot:
                bk_ref[d, 0] = jnp.zeros(bk_ref.shape[2:], F32)
                bv_ref[d, 0] = jnp.zeros(bv_ref.shape[2:], F32)
    else:
        bk_ref[...] = mm('b_k')
        bv_ref[...] = mm('b_v')
    cuv_ref[...] = mm('c_uv')

    z = mm('small') + par_ref[0:1, :]
    lane = _iota(z.shape, 1)
    t = jnp.log1p(jnp.exp(-jnp.abs(z)))
    softplus = jnp.maximum(z, 0.0) + t
    log_sig = jnp.minimum(z, 0.0) - t
    sm = jnp.where(lane < SM_G, jax.nn.sigmoid(z),
                   jnp.where(lane < SM_F, -jnp.exp(par_ref[1:2, :]) * softplus, log_sig))
    small_ref[...] = sm

    lb = min(seq_len, tm)
    row, col = _iota((tm, tm), 0), _iota((tm, tm), 1)
    tri = jnp.where((col <= row) & (row // lb == col // lb), 1.0, 0.0)
    cum = _dot01_left(tri, sm)
    if seq_tiles > 1:
        @pl.when(i % seq_tiles == 0)
        def _():
            carry_ref[...] = jnp.zeros_like(carry_ref)
        cum = cum + carry_ref[0:1, :]
        carry_ref[0:1, :] = cum[tm - 1:tm, :]
    cumt_ref[...] = cum.T[SM_F:SM_F + SUBLANES, :]
    logft_ref[...] = sm.T[SM_F:SM_F + SUBLANES, :]


def _inproj(x, nw, wt, par, offs, seq_len, tm, kv_stack=None):
    n, d = x.shape
    assert n % tm == 0 and (seq_len % tm == 0 or tm % seq_len == 0)
    seq_tiles = max(seq_len // tm, 1)
    widths = {k: hi - lo for k, (lo, hi) in offs.items()}
    names = ['a_qkv', 'a_z', 'b_q', 'b_k', 'b_v', 'c_uv', 'small']
    out_shape = [jax.ShapeDtypeStruct((n, widths[k]), F32) for k in names]
    out_specs = [pl.BlockSpec((tm, widths[k]), lambda i: (i, 0)) for k in names]
    out_shape += [jax.ShapeDtypeStruct((SUBLANES, n), F32)] * 2
    out_specs += [pl.BlockSpec((SUBLANES, tm), lambda i: (0, i))] * 2
    args = [x, nw, wt, par]
    in_specs = [pl.BlockSpec((tm, d), lambda i: (i, 0)),
                pl.BlockSpec((1, d), lambda i: (0, 0)),
                pl.BlockSpec(wt.shape, lambda i: (0, 0)),
                pl.BlockSpec(par.shape, lambda i: (0, 0))]
    aliases = {}
    kv_slot = 0
    if kv_stack is not None:
        layer, depth, kt_prev, vt_prev = kv_stack
        assert seq_len % tm == 0
        w_b = widths['b_k']
        if kt_prev is None:
            kv_slot, blk, first = layer, depth, 0
        else:
            kv_slot, blk, first = 0, 1, layer
            args += [kt_prev, vt_prev]
            in_specs += [pl.BlockSpec(memory_space=pl.ANY)] * 2
            aliases = {4: 3, 5: 4}
        for pos in (3, 4):
            out_shape[pos] = jax.ShapeDtypeStruct((depth, n // seq_len, w_b, seq_len), F32)
            out_specs[pos] = pl.BlockSpec((blk, 1, w_b, tm), lambda i: (first, i // seq_tiles, 0, i % seq_tiles))
    return pl.pallas_call(
        functools.partial(_inproj_kernel, offs=offs, seq_tiles=seq_tiles, seq_len=seq_len,
                          kv_transposed=kv_stack is not None, kv_slot=kv_slot, n_alias=len(aliases)),
        grid=(n // tm,),
        in_specs=in_specs,
        out_specs=out_specs,
        out_shape=out_shape,
        input_output_aliases=aliases,
        scratch_shapes=[pltpu.VMEM((SUBLANES, LANES), F32)],
        compiler_params=_cparams(("arbitrary",)),
        name="inproj",
    )(*args)


def _unit_lower_inverses(lmats):
    n = lmats[0].shape[0]
    row, col = _iota((n, n), 0), _iota((n, n), 1)
    eye = (row == col).astype(F32)
    base = SUBLANES
    in_base = row // base == col // base
    lds = [jnp.where(in_base, lm, 0.0) for lm in lmats]
    xs = [eye - ld for ld in lds]
    ps = [_dot(ld, ld) for ld in lds]
    xs = [x + _dot(x, p) for x, p in zip(xs, ps)]
    ps = [_dot(p, p) for p in ps]
    xs = [x + _dot(x, p) for x, p in zip(xs, ps)]
    s = base
    while s < HEAD_DIM:
        off_mask = (row // (2 * s) == col // (2 * s)) & (row // s != col // s)
        ts_ = [_dot(x, jnp.where(off_mask, lm, 0.0)) for x, lm in zip(xs, lmats)]
        xs = [x - _dot(t_, x) for x, t_ in zip(xs, ts_)]
        s *= 2
    return xs


def _gdn_kernel(aqkv_ref, az_ref, small_ref, buf_ref, s0_ref, cw_ref, nw_ref,
                o_ref, sout_ref,
                s_ref, prev_ref, qs_ref, qg_ref, k_ref, kb_ref, kbg_ref, vb_ref, gc_ref, oacc_ref,
                conv_ref, uw_ref, at_ref, *, n_pairs, valid_len, group):
    t = pl.program_id(1)
    nt = pl.num_programs(1)
    nb, ts, w3 = aqkv_ref.shape
    w_a = n_pairs * PAIR
    c = GDN_CHUNK
    rows_all = nb * ts
    cps = ts // c

    @pl.when(t == 0)
    def _():
        for bi in range(nb):
            for j in range(n_pairs):
                s_a = s0_ref[bi, 2 * j]
                s_b = s0_ref[bi, 2 * j + 1]
                top = jnp.concatenate([s_a, jnp.zeros_like(s_a)], axis=1)
                bot = jnp.concatenate([jnp.zeros_like(s_b), s_b], axis=1)
                s_ref[bi, j] = jnp.concatenate([top, bot], axis=0)
        prev_ref[...] = buf_ref[...]

    x = aqkv_ref[...].reshape(rows_all, w3)
    kw = cw_ref.shape[0]
    conv = cw_ref[kw - 1:kw, :] * x
    for i in range(1, kw):
        conv = conv + cw_ref[kw - 1 - i:kw - i, :] * pltpu.roll(x, i, axis=0)
    conv_ref[...] = conv
    for bi in range(nb):
        head = jnp.concatenate([prev_ref[bi], x[bi * ts:bi * ts + SUBLANES]], axis=0)
        fixed = cw_ref[0:1, :] * head[SUBLANES - kw + 1:2 * SUBLANES - kw + 1]
        for i in range(1, kw):
            fixed = fixed + cw_ref[i:i + 1, :] * head[SUBLANES - kw + 1 + i:2 * SUBLANES - kw + 1 + i]
        conv_ref[bi * ts:bi * ts + SUBLANES, :] = fixed
        prev_ref[bi] = x[(bi + 1) * ts - SUBLANES:(bi + 1) * ts]
    a = _silu(conv_ref[...])
    rowid = _iota((rows_all, 1), 0) % ts + t * ts
    valid = (rowid < valid_len).astype(F32)
    ones_h = _head_ones(w_a)
    q = a[:, 0:w_a]
    k = a[:, w_a:2 * w_a]
    v = a[:, 2 * w_a:3 * w_a] * valid
    qn = q * lax.rsqrt(_dot2(q * q, ones_h) + EPS) * valid
    kn = k * lax.rsqrt(_dot2(k * k, ones_h) + EPS) * valid

    sm = small_ref[...].reshape(rows_all, LANES) * valid
    row, col = _iota((rows_all, rows_all), 0), _iota((rows_all, rows_all), 1)
    tri = jnp.where((col <= row) & (row // c == col // c), 1.0, 0.0)
    gcum = _dot01_left(tri, sm)
    gc_ref[...] = gcum
    lane, hrow = _iota((LANES, w_a), 1), _iota((LANES, w_a), 0)
    exp_b = (hrow == lane // HEAD_DIM + SM_BETA).astype(F32)
    exp_g = (hrow == lane // HEAD_DIM + SM_G).astype(F32)
    beta_e = _dot2(sm, exp_b)
    gcum_e = _dot01_right(gcum, exp_g)
    eg = jnp.exp(gcum_e)
    scale = HEAD_DIM ** -0.5
    qs_ref[...] = qn * scale
    qg_ref[...] = qn * (scale * eg)
    k_ref[...] = kn
    kb = kn * beta_e
    kb_ref[...] = kb
    kbg_ref[...] = kb * eg
    vb_ref[...] = v * beta_e

    r2, c2 = _iota((PAIR, PAIR), 0), _iota((PAIR, PAIR), 1)
    same_head = (r2 // HEAD_DIM) == (c2 // HEAD_DIM)
    strict = same_head & (c2 < r2)
    rw, cw_ = _iota((c, PAIR), 0), _iota((c, PAIR), 1)
    incl_wide = (cw_ % HEAD_DIM) <= rw
    lane_lo = _iota((c, PAIR), 1) < HEAD_DIM

    def stack(m):
        return jnp.concatenate([jnp.where(lane_lo, m, 0.0), jnp.where(lane_lo, 0.0, m)], axis=0)

    def gate_cols(g_c, j):
        return g_c[:, SM_G + 2 * j:SM_G + 2 * j + 1], g_c[:, SM_G + 2 * j + 1:SM_G + 2 * j + 2]

    def prepare(gi, carry):
        items = []
        for i in range(group):
            ci = gi * group + i
            r0 = pl.multiple_of(ci * c, c)
            g_c = gc_ref[pl.ds(r0, c), :]
            g_t = g_c.T
            for j in range(n_pairs):
                ls = slice(j * PAIR, (j + 1) * PAIR)
                ga_col, gb_col = gate_cols(g_c, j)
                g_row = jnp.concatenate([g_t[SM_G + 2 * j:SM_G + 2 * j + 1, :],
                                         g_t[SM_G + 2 * j + 1:SM_G + 2 * j + 2, :]], axis=1)
                g_col2 = jnp.concatenate([ga_col, gb_col], axis=0)
                g_wide = jnp.where(lane_lo, ga_col, gb_col)
                items.append(dict(
                    ci=ci, j=j,
                    k2=stack(k_ref[pl.ds(r0, c), ls]),
                    kb2=stack(kb_ref[pl.ds(r0, c), ls]),
                    rhs=jnp.concatenate([stack(vb_ref[pl.ds(r0, c), ls]), stack(kbg_ref[pl.ds(r0, c), ls])],
                                        axis=1),
                    qs=qs_ref[pl.ds(r0, c), ls],
                    decay2=jnp.exp(jnp.minimum(g_col2 - g_row, 0.0)),
                    decay_w=jnp.exp(jnp.minimum(g_wide - g_row, 0.0))))
        kk = [_dot_nt(it['kb2'], it['k2']) for it in items]
        qk = [_dot_nt(it['qs'], it['k2']) for it in items]
        lmats = [jnp.where(strict, m * it['decay2'], 0.0) for m, it in zip(kk, items)]
        for m, it in zip(qk, items):
            at_ref[it['ci'], it['j']] = jnp.where(incl_wide, m * it['decay_w'], 0.0)
        tinvs = _unit_lower_inverses(lmats)
        sols = [_dot(tinv, it['rhs']) for tinv, it in zip(tinvs, items)]
        for sol, it in zip(sols, items):
            uw_ref[it['ci'], it['j']] = sol
        return carry

    lax.fori_loop(0, rows_all // c // group, prepare, 0)

    def recur(n, carry):
        items = []
        for bi in range(nb):
            ci = bi * cps + n
            r0 = pl.multiple_of(ci * c, c)
            g_c = gc_ref[pl.ds(r0, c), :]
            for j in range(n_pairs):
                ls = slice(j * PAIR, (j + 1) * PAIR)
                ga_col, gb_col = gate_cols(g_c, j)
                g_wide = jnp.where(lane_lo, ga_col, gb_col)
                g_last = jnp.where(lane_lo, ga_col[c - 1:c, :], gb_col[c - 1:c, :])
                g_last_col = jnp.where(_iota((PAIR, 1), 0) < HEAD_DIM, ga_col[c - 1:c, :], gb_col[c - 1:c, :])
                items.append(dict(
                    bi=bi, r0=r0, ls=ls, j=j,
                    u2=uw_ref[ci, j, :, 0:PAIR], w2=uw_ref[ci, j, :, PAIR:2 * PAIR], attn=at_ref[ci, j],
                    s2=s_ref[bi, j], qg=qg_ref[pl.ds(r0, c), ls],
                    kdec=k_ref[pl.ds(r0, c), ls] * jnp.exp(g_last - g_wide),
                    s_decay=jnp.exp(g_last_col)))
        ws = [_dot(it['w2'], it['s2']) for it in items]
        qss = [_dot(it['qg'], it['s2']) for it in items]
        vnew2 = [it['u2'] - w for it, w in zip(items, ws)]
        avs = [_dot(it['attn'], v2) for it, v2 in zip(items, vnew2)]
        kvs = [_dot_tn(it['kdec'], v2[:c, :] + v2[c:, :]) for it, v2 in zip(items, vnew2)]
        for it, q_s, av, kv in zip(items, qss, avs, kvs):
            oacc_ref[pl.ds(it['r0'], c), it['ls']] = q_s + av
            s_ref[it['bi'], it['j']] = it['s2'] * it['s_decay'] + jnp.where(same_head, kv, 0.0)
        return carry

    lax.fori_loop(0, cps, recur, 0)

    o = oacc_ref[...]
    on = o * lax.rsqrt(_dot2(o * o, ones_h) * (1.0 / HEAD_DIM) + EPS) * nw_ref[...]
    o_ref[...] = (on * _silu(az_ref[...].reshape(rows_all, w_a))).reshape(nb, ts, w_a)

    @pl.when(t == nt - 1)
    def _():
        for bi in range(nb):
            for j in range(n_pairs):
                s2 = s_ref[bi, j]
                sout_ref[bi, 2 * j] = s2[:HEAD_DIM, :HEAD_DIM]
                sout_ref[bi, 2 * j + 1] = s2[HEAD_DIM:, HEAD_DIM:]


def _gdn(aqkv, az, small, bufpad, s0, conv_w, norm_w_tiled, valid_len, ts, nb):
    b, l, w3 = aqkv.shape
    w_a = w3 // 3
    n_pairs = w_a // PAIR
    heads = s0.shape[1]
    n_chunks = nb * ts // GDN_CHUNK
    group = math.gcd(n_chunks, 4)
    assert l % ts == 0 and ts % GDN_CHUNK == 0 and heads == 2 * n_pairs and b % nb == 0
    rows_all = nb * ts
    scratch = [pltpu.VMEM((nb, n_pairs, PAIR, PAIR), F32), pltpu.VMEM((nb, SUBLANES, w3), F32)]
    scratch += [pltpu.VMEM((rows_all, w_a), F32) for _ in range(6)]
    scratch += [pltpu.VMEM((rows_all, LANES), F32), pltpu.VMEM((rows_all, w_a), F32),
                pltpu.VMEM((rows_all, w3), F32),
                pltpu.VMEM((n_chunks, n_pairs, PAIR, 2 * PAIR), F32),
                pltpu.VMEM((n_chunks, n_pairs, GDN_CHUNK, PAIR), F32)]
    return pl.pallas_call(
        functools.partial(_gdn_kernel, n_pairs=n_pairs, valid_len=valid_len, group=group),
        grid=(b // nb, l // ts),
        in_specs=[pl.BlockSpec((nb, ts, w3), lambda i, t: (i, t, 0)),
                  pl.BlockSpec((nb, ts, w_a), lambda i, t: (i, t, 0)),
                  pl.BlockSpec((nb, ts, LANES), lambda i, t: (i, t, 0)),
                  pl.BlockSpec((nb, SUBLANES, w3), lambda i, t: (i, 0, 0)),
                  pl.BlockSpec((nb, heads, HEAD_DIM, HEAD_DIM), lambda i, t: (i, 0, 0, 0)),
                  pl.BlockSpec(conv_w.shape, lambda i, t: (0, 0)),
                  pl.BlockSpec(norm_w_tiled.shape, lambda i, t: (0, 0))],
        out_specs=[pl.BlockSpec((nb, ts, w_a), lambda i, t: (i, t, 0)),
                   pl.BlockSpec((nb, heads, HEAD_DIM, HEAD_DIM), lambda i, t: (i, 0, 0, 0))],
        out_shape=[jax.ShapeDtypeStruct((b, l, w_a), F32),
                   jax.ShapeDtypeStruct((b, heads, HEAD_DIM, HEAD_DIM), F32)],
        scratch_shapes=scratch,
        compiler_params=_cparams(("arbitrary", "arbitrary")),
        name="gdn",
    )(aqkv, az, small, bufpad, s0, conv_w, norm_w_tiled)


def _fox_prompt_kernel(q_ref, kt_ref, vt_ref, cumt_ref, nw_ref, o_ref, *, n_pairs):
    i = pl.program_id(1)
    tq = q_ref.shape[1]
    tk = tq
    scale = HEAD_DIM ** -0.5
    lane_lo = _iota((tq, PAIR), 1) < HEAD_DIM
    causal = _iota((tq, tk), 1) <= _iota((tq, tk), 0)
    q0 = pl.multiple_of(i * tq, tq)
    pairs = []
    for j in range(n_pairs):
        ls = slice(j * PAIR, (j + 1) * PAIR)
        q_p = q_ref[0, :, ls] * scale
        heads = (2 * j, 2 * j + 1)
        q_hs = [jnp.where(lane_lo, q_p, 0.0).astype(BF16), jnp.where(lane_lo, 0.0, q_p).astype(BF16)]
        c_refs = [cumt_ref[h:h + 1, pl.ds(q0, LANES)][:, 0:1] for h in heads]

        def block(k0, carry, masked, q_hs=q_hs, c_refs=c_refs, heads=heads, ls=ls):
            k_blk = kt_ref[0, 0, ls, pl.ds(k0, tk)].astype(BF16)
            v_blk = vt_ref[0, 0, ls, pl.ds(k0, tk)].astype(BF16)
            ss = [jnp.dot(q_h, k_blk, preferred_element_type=F32) for q_h in q_hs]
            ss = [s + (c_ref - cumt_ref[h:h + 1, pl.ds(k0, tk)]) for s, c_ref, h in zip(ss, c_refs, heads)]
            if masked:
                ss = [jnp.where(causal, s, NEG_INF) for s in ss]
            m_news = [jnp.maximum(st[0], jnp.max(s, axis=-1, keepdims=True)) for st, s in zip(carry, ss)]
            ps = [jnp.exp(s - m_new) for s, m_new in zip(ss, m_news)]
            alphas = [jnp.exp(st[0] - m_new) for st, m_new in zip(carry, m_news)]
            pvs = [lax.dot_general(p.astype(BF16), v_blk, (((1,), (1,)), ((), ())), preferred_element_type=F32)
                   for p in ps]
            return tuple((m_new, a * st[1] + jnp.sum(p, axis=-1, keepdims=True), a * st[2] + pv)
                         for st, m_new, a, p, pv in zip(carry, m_news, alphas, ps, pvs))

        init = tuple((jnp.full((tq, 1), NEG_INF, F32), jnp.zeros((tq, 1), F32), jnp.zeros((tq, PAIR), F32))
                     for _ in heads)
        carry = lax.fori_loop(
            0, i, lambda kb, c_, block=block: block(pl.multiple_of(kb * tk, tk), c_, False), init)
        outs = [acc / l for _, l, acc in block(q0, carry, True)]
        pairs.append(jnp.where(lane_lo, outs[0], outs[1]))
    o = jnp.concatenate(pairs, axis=1)
    ones_h = _head_ones(o.shape[1])
    o_ref[0] = o * lax.rsqrt(_dot2(o * o, ones_h) * (1.0 / HEAD_DIM) + EPS) * nw_ref[...]


def _fox_prompt(layer, q, kt, vt, cumt, norm_w_tiled, tq):
    b, l, w_b = q.shape
    assert l % tq == 0 and tq % LANES == 0
    return pl.pallas_call(
        functools.partial(_fox_prompt_kernel, n_pairs=w_b // PAIR),
        grid=(b, l // tq),
        in_specs=[pl.BlockSpec((1, tq, w_b), lambda i, t: (i, t, 0)),
                  pl.BlockSpec((1, 1, w_b, l), lambda i, t: (layer, i, 0, 0)),
                  pl.BlockSpec((1, 1, w_b, l), lambda i, t: (layer, i, 0, 0)),
                  pl.BlockSpec((SUBLANES, l), lambda i, t: (0, i)),
                  pl.BlockSpec(norm_w_tiled.shape, lambda i, t: (0, 0))],
        out_specs=pl.BlockSpec((1, tq, w_b), lambda i, t: (i, t, 0)),
        out_shape=jax.ShapeDtypeStruct((b, l, w_b), F32),
        compiler_params=_cparams(("arbitrary", "arbitrary")),
        name="fox_prompt",
    )(q, kt, vt, cumt, norm_w_tiled)


def _page_suffix_kernel(x_ref, m_ref, o_ref):
    n_heads = x_ref.shape[1]
    for h in range(SUBLANES):
        if h < n_heads:
            o_ref[0, :, h, :] = _dot01_right(x_ref[0, h], m_ref[...])
        else:
            o_ref[0, :, h, :] = jnp.zeros((o_ref.shape[1], o_ref.shape[3]), F32)


def _page_suffix(logf_t, tp):
    depth, n_heads, n_phys, page = logf_t.shape
    assert page == LANES and n_phys % tp == 0
    idx = jnp.arange(page)
    m = (idx[:, None] >= idx[None, :]).astype(BF16)
    return pl.pallas_call(
        _page_suffix_kernel,
        grid=(depth, n_phys // tp),
        in_specs=[pl.BlockSpec((1, n_heads, tp, page), lambda d, i: (d, 0, i, 0)),
                  pl.BlockSpec(m.shape, lambda d, i: (0, 0))],
        out_specs=pl.BlockSpec((1, tp, SUBLANES, page), lambda d, i: (d, i, 0, 0)),
        out_shape=jax.ShapeDtypeStruct((depth, n_phys, SUBLANES, page), F32),
        compiler_params=_cparams(("arbitrary", "arbitrary")),
        name="page_suffix",
    )(logf_t, m)


def _run_if(cond, fn):
    if isinstance(cond, bool):
        if cond:
            fn()
    else:
        pl.when(cond)(fn)


def _decode_round_scores(step, n_steps, refs, *, layer, n_new, pps, sub_pages, nj, j_static, maybe_first,
                         maybe_last):
    (pt_ref, q_ref, kn_ref, vn_ref, cn_ref, nw_ref, kt_hbm, vt_hbm, sfx_hbm, o_ref,
     kbuf, vbuf, sbuf, ksem, vsem, ssem, qbd_ref, m_ref, l_ref, acc_ref, carry_ref) = refs
    bb = step // nj
    j = step % nj if j_static is None else j_static
    n_tok = q_ref.shape[1]
    w_b = q_ref.shape[2]
    rows = n_tok * SUBLANES
    page = kbuf.shape[3]
    n_pages = nj * pps
    scale = HEAD_DIM ** -0.5
    slot = step % 2

    def page_copies(step_x, slot_x, i):
        pg = pt_ref[step_x // nj, n_pages - 1 - ((step_x % nj) * pps + i)]
        return (pltpu.make_async_copy(kt_hbm.at[layer, pg], kbuf.at[slot_x, i], ksem.at[slot_x]),
                pltpu.make_async_copy(vt_hbm.at[layer, pg], vbuf.at[slot_x, i], vsem.at[slot_x]),
                pltpu.make_async_copy(sfx_hbm.at[layer, pg], sbuf.at[slot_x, i], ssem.at[slot_x]))

    def request(step_x, slot_x):
        for i in range(pps):
            for cp in page_copies(step_x, slot_x, i):
                cp.start()

    if maybe_first:
        def prime():
            request(step, slot)
            if n_steps > 1:
                request(step + 1, 1 - slot)
        _run_if(step == 0, prime)

    for i in range(pps):
        for cp in page_copies(step, slot, i):
            cp.wait()

    def start_sequence():
        q = q_ref[bb] * scale
        headmask = (_iota((SUBLANES, w_b), 1) // HEAD_DIM == _iota((SUBLANES, w_b), 0)).astype(F32)
        qbd = jnp.concatenate([q[t:t + 1, :] * headmask for t in range(n_tok)], axis=0)
        qbd_ref[...] = qbd.astype(BF16)
        kn = kn_ref[bb]
        s = _dot_nt(qbd, kn)
        cn = cn_ref[bb]
        s = s - jnp.concatenate([cn] * n_tok, axis=0)
        kidx = _iota((rows, SUBLANES), 1)
        tidx = _iota((rows, SUBLANES), 0) // SUBLANES
        s = jnp.where((kidx <= tidx) & (kidx < n_new), s, NEG_INF)
        m = jnp.max(s, axis=-1, keepdims=True)
        p = jnp.exp(s - m)
        m_ref[...] = m
        l_ref[...] = jnp.sum(p, axis=-1, keepdims=True)
        acc_ref[...] = _dot(p, vn_ref[bb])
        carry_ref[...] = jnp.zeros_like(carry_ref)

    _run_if(j == 0, start_sequence)

    lane = _iota((SUBLANES, page), 1)
    carry = carry_ref[...]
    biases = []
    for i in range(pps):
        incl = sbuf[slot, i]
        excl = jnp.where(lane < page - 1, pltpu.roll(incl, page - 1, axis=1), 0.0)
        biases.append(excl + carry)
        carry = carry + incl[:, 0:1]
    carry_ref[...] = carry
    qbd = qbd_ref[...]
    n_sub = pps // sub_pages
    subs = [range(u * sub_pages, (u + 1) * sub_pages) for u in range(n_sub)]
    kts = [jnp.concatenate([kbuf[slot, i].astype(BF16) for i in sb], axis=1) for sb in subs]
    ss = [jnp.dot(qbd, kt, preferred_element_type=F32) for kt in kts]
    ss = [s + jnp.concatenate([jnp.concatenate([biases[i] for i in sb], axis=1)] * n_tok, axis=0)
          for s, sb in zip(ss, subs)]
    ms = [jnp.max(s, axis=-1, keepdims=True) for s in ss]
    ps = [jnp.exp(s - m) for s, m in zip(ss, ms)]
    ls = [jnp.sum(p, axis=-1, keepdims=True) for p in ps]
    refill = functools.partial(_run_if, step + 2 < n_steps if maybe_last else True,
                               lambda: request(step + 2, slot))
    return dict(refs=refs, slot=slot, subs=subs, ms=ms, ps=ps, ls=ls, bb=bb, last_of_sequence=j == nj - 1,
                refill=refill)


def _decode_round_values(state):
    (_, q_ref, _, _, _, nw_ref, _, _, _, o_ref,
     _, vbuf, _, _, _, _, _, m_ref, l_ref, acc_ref, _) = state['refs']
    slot, subs, ms, ps, ls, bb = (state[k] for k in ('slot', 'subs', 'ms', 'ps', 'ls', 'bb'))
    n_tok = q_ref.shape[1]
    w_b = q_ref.shape[2]
    rows = n_tok * SUBLANES
    vts = [jnp.concatenate([vbuf[slot, i].astype(BF16) for i in sb], axis=1) for sb in subs]
    pvs = [lax.dot_general(p.astype(BF16), vt, (((1,), (1,)), ((), ())), preferred_element_type=F32)
           for p, vt in zip(ps, vts)]
    state['refill']()
    m_old = m_ref[...]
    m_new = m_old
    for m in ms:
        m_new = jnp.maximum(m_new, m)
    alpha = jnp.exp(m_old - m_new)
    l_new = alpha * l_ref[...]
    acc = alpha * acc_ref[...]
    for m, l_u, pv in zip(ms, ls, pvs):
        w_u = jnp.exp(m - m_new)
        l_new = l_new + w_u * l_u
        acc = acc + w_u * pv
    m_ref[...] = m_new
    l_ref[...] = l_new
    acc_ref[...] = acc

    def finish_sequence():
        o = acc_ref[...] / l_ref[...]
        diag = (_iota((rows, w_b), 1) // HEAD_DIM) == (_iota((rows, w_b), 0) % SUBLANES)
        o = jnp.where(diag, o, 0.0)
        group = (_iota((SUBLANES, rows), 1) // SUBLANES == _iota((SUBLANES, rows), 0)).astype(F32)
        o_tok = jnp.dot(group, o, precision=HIGHEST, preferred_element_type=F32)
        ones_h = _head_ones(w_b)
        ss = jnp.dot(o_tok * o_tok, ones_h, precision=HIGHEST, preferred_element_type=F32)
        on = o_tok * lax.rsqrt(ss * (1.0 / HEAD_DIM) + EPS) * nw_ref[...]
        o_ref[bb] = on[0:n_tok, :]

    _run_if(state['last_of_sequence'], finish_sequence)


N_DECODE_INPUTS = 9
N_DECODE_SCRATCH = 11


def _decode_plumbing(layer, page_table, q, k_new8, v_new8, cn8, norm_w_tiled, cache_kt, cache_vt, suffix, n_new,
                     n_hosts):
    b, n_tok, w_b = q.shape
    page = cache_kt.shape[3]
    n_pages = page_table.shape[1]
    pps = math.gcd(n_pages, FOX_PAGES_PER_STEP)
    sub_pages = math.gcd(pps, FOX_PAGES_PER_SUBBLOCK)
    nj = n_pages // pps
    assert (b * nj) % n_hosts == 0
    rows = n_tok * SUBLANES
    n_slots = 2

    def whole(a):
        nd = a.ndim
        return pl.BlockSpec(a.shape, lambda *_: (0,) * nd)

    args = [page_table, q, k_new8, v_new8, cn8, norm_w_tiled, cache_kt, cache_vt, suffix]
    in_specs = [pl.BlockSpec(memory_space=pltpu.SMEM), whole(q), whole(k_new8), whole(v_new8), whole(cn8),
                whole(norm_w_tiled), pl.BlockSpec(memory_space=pl.ANY), pl.BlockSpec(memory_space=pl.ANY),
                pl.BlockSpec(memory_space=pl.ANY)]
    assert len(args) == N_DECODE_INPUTS
    out_shape = jax.ShapeDtypeStruct((b, n_tok, w_b), F32)
    out_spec = pl.BlockSpec((b, n_tok, w_b), lambda *_: (0, 0, 0))
    scratch = [pltpu.VMEM((n_slots, pps, w_b, page), F32), pltpu.VMEM((n_slots, pps, w_b, page), F32),
               pltpu.VMEM((n_slots, pps, SUBLANES, page), F32),
               pltpu.SemaphoreType.DMA((n_slots,)), pltpu.SemaphoreType.DMA((n_slots,)),
               pltpu.SemaphoreType.DMA((n_slots,)),
               pltpu.VMEM((rows, w_b), BF16), pltpu.VMEM((rows, 1), F32), pltpu.VMEM((rows, 1), F32),
               pltpu.VMEM((rows, w_b), F32), pltpu.VMEM((SUBLANES, page), F32)]
    assert len(scratch) == N_DECODE_SCRATCH
    config = dict(layer=layer, n_new=n_new, pps=pps, sub_pages=sub_pages, nj=nj,
                  rounds_per_host=b * nj // n_hosts, n_steps=b * nj)
    return args, in_specs, out_shape, out_spec, scratch, config


def _rms(x, w):
    return x * lax.rsqrt(jnp.mean(x * x, axis=-1, keepdims=True) + EPS) * w


def _gelu_tanh(x):
    return x * (0.5 * (1.0 + jnp.tanh(math.sqrt(2.0 / math.pi) * (x + 0.044715 * (x * x * x)))))


def _mix_ffn_kernel(x_ref, oa_ref, ob_ref, cuv_ref, cvw_ref, wm_ref, bse_ref, wo_ref, nf_ref,
                    wg_ref, wu_ref, cw_ref, wd_ref, pm_ref, nfin_ref, *refs,
                    seq_len, dff_chunk, final_norm, multi_seq, decode):
    if decode is None:
        y_ref, tail_ref, cv_ref, carry_ref = refs
        decode_refs = None
    else:
        dec_in, rest = refs[:N_DECODE_INPUTS], refs[N_DECODE_INPUTS:]
        y_ref, tail_ref, cv_ref, dec_out, carry_ref = rest[:5]
        decode_refs = tuple(dec_in) + (dec_out,) + tuple(rest[5:])
    t = pl.program_id(1)
    tm = x_ref.shape[0]
    w_a, w_b = oa_ref.shape[1], ob_ref.shape[1]
    w_c = cuv_ref.shape[1] // 2
    d_ff = wg_ref.shape[1]
    tail_rows = tail_ref.shape[1]

    gel = _gelu_tanh(cuv_ref[...])
    cu, cv = gel[:, :w_c], gel[:, w_c:]
    ones_h = _head_ones(w_c)
    cvn = cv * lax.rsqrt(_dot2(cv * cv, ones_h) * (1.0 / HEAD_DIM) + EPS) * cvw_ref[...]
    cv_ref[...] = cvn
    cvn_b = cvn.astype(BF16)
    lane_lo = _iota((LANES, PAIR), 1) < HEAD_DIM
    row_blocks = []
    for n in range(tm // LANES):
        rs = slice(n * LANES, (n + 1) * LANES)
        pair_blocks = []
        for j in range(w_c // PAIR):
            ls = slice(j * PAIR, (j + 1) * PAIR)
            v_p = cvn_b[rs, ls]
            mix_a = jnp.dot(wm_ref[2 * j], v_p, preferred_element_type=F32)
            mix_b = jnp.dot(wm_ref[2 * j + 1], v_p, preferred_element_type=F32)
            pair_blocks.append(cu[rs, ls] * (jnp.where(lane_lo, mix_a, mix_b) + bse_ref[:, ls]))
        row_blocks.append(jnp.concatenate(pair_blocks, axis=1))
    o_c = jnp.concatenate(row_blocks, axis=0)

    mix = jnp.concatenate([oa_ref[...].astype(BF16), ob_ref[...].astype(BF16), o_c.astype(BF16)], axis=1)
    x1 = x_ref[...] + jnp.dot(mix, wo_ref[...], preferred_element_type=F32)
    h2 = _rms(x1, nf_ref[...]).astype(BF16)

    row = _iota((tm, 1), 0)
    if multi_seq:
        pos = row % seq_len
    else:
        pos = row

        @pl.when(t == 0)
        def _():
            carry_ref[...] = pm_ref[0]

    def project(c0):
        cs = slice(c0, c0 + dff_chunk)
        return (jnp.dot(h2, wg_ref[:, cs], preferred_element_type=F32),
                jnp.dot(h2, wu_ref[:, cs], preferred_element_type=F32))

    n_chunks = d_ff // dff_chunk
    rounds_at = [[] for _ in range(n_chunks)]
    if decode is not None:
        rph = decode['rounds_per_host']
        first_round = (pl.program_id(0) * pl.num_programs(1) + t) * rph
        for r in range(rph):
            rounds_at[r * n_chunks // rph].append(r)
        round_cfg = {k: decode[k] for k in ('layer', 'n_new', 'pps', 'sub_pages', 'nj')}
        whole_sequences = rph % decode['nj'] == 0

    acc = jnp.zeros((tm, x_ref.shape[1]), F32)
    nxt = project(0)
    for ci, c0 in enumerate(range(0, d_ff, dff_chunk)):
        cs = slice(c0, c0 + dff_chunk)
        gp, up = nxt
        def scores(r):
            return _decode_round_scores(first_round + r, decode['n_steps'], decode_refs,
                                        j_static=r % decode['nj'] if whole_sequences else None,
                                        maybe_first=r == 0, maybe_last=r >= rph - 2, **round_cfg)

        hosted = rounds_at[ci]
        state = scores(hosted[0]) if hosted else None
        if c0 + dff_chunk < d_ff:
            nxt = project(c0 + dff_chunk)
        if hosted:
            _decode_round_values(state)
        for r in hosted[1:]:
            _decode_round_values(scores(r))
        if multi_seq:
            first1, first2 = pm_ref[0, :, cs], pm_ref[1, :, cs]
        else:
            c6, c7 = carry_ref[SUBLANES - 2:SUBLANES - 1, cs], carry_ref[SUBLANES - 1:SUBLANES, cs]
            first1 = c7
            first2 = jnp.where(row == 0, c6, c7)
            carry_ref[:, cs] = gp[tm - SUBLANES:tm, :]
        gm1 = jnp.where(pos >= 1, pltpu.roll(gp, 1, axis=0), first1)
        gm2 = jnp.where(pos >= 2, pltpu.roll(gp, 2, axis=0), first2)
        gate = cw_ref[0:1, cs] * gm2 + cw_ref[1:2, cs] * gm1 + cw_ref[2:3, cs] * gp
        act = (_silu(gate) * up).astype(BF16)
        acc = acc + jnp.dot(act, wd_ref[cs, :], preferred_element_type=F32)
        tail_ref[0, :, cs] = gp[tm - tail_rows:tm, :]
    x2 = x1 + acc
    if final_norm:
        x2 = _rms(x2, nfin_ref[...])
    y_ref[...] = x2


def _mix_ffn(x, oa, ob, cuv, cvw, wm, bse, wo, nf, wg, wu, cw, wd, pm, nfin, seq_len, tm, final_norm,
             decode_operands=None):
    n, d = x.shape
    multi_seq = seq_len < tm
    assert n % tm == 0 and (tm % seq_len == 0 if multi_seq else seq_len % tm == 0)
    assert tm % LANES == 0 and wm.shape[1] == LANES
    n_seq = 1 if multi_seq else n // seq_len
    seq_tiles = 1 if multi_seq else seq_len // tm
    n_tiles = n // tm
    d_ff = wg.shape[1]
    dff_chunk = 256
    assert d_ff % dff_chunk == 0
    tail_rows = tm if multi_seq else SUBLANES
    w_c = cuv.shape[1] // 2

    def rows(w):
        return pl.BlockSpec((tm, w), lambda s, t: (s * seq_tiles + t, 0))

    def const(a):
        nd = a.ndim
        return pl.BlockSpec(a.shape, lambda s, t: (0,) * nd, pipeline_mode=pl.Buffered(1))

    if multi_seq:
        pm_spec = pl.BlockSpec((2, tm, d_ff), lambda s, t: (0, s * seq_tiles + t, 0))
        grid = (n_tiles, 1)
    else:
        pm_spec = pl.BlockSpec((1, SUBLANES, d_ff), lambda s, t: (s, 0, 0))
        grid = (n_seq, seq_tiles)
    n_tail = n_tiles if multi_seq else n_seq
    args = [x, oa, ob, cuv, cvw, wm, bse, wo, nf, wg, wu, cw, wd, pm, nfin]
    in_specs = [rows(d), rows(oa.shape[1]), rows(ob.shape[1]), rows(cuv.shape[1]),
                const(cvw), const(wm), const(bse), const(wo), const(nf), const(wg), const(wu), const(cw),
                const(wd), pm_spec, const(nfin)]
    out_specs = [rows(d), pl.BlockSpec((1, tail_rows, d_ff), lambda s, t: (s, 0, 0)), rows(w_c)]
    out_shape = [jax.ShapeDtypeStruct((n, d), F32),
                 jax.ShapeDtypeStruct((n_tail, tail_rows, d_ff), F32),
                 jax.ShapeDtypeStruct((n, w_c), F32)]
    scratch = [pltpu.VMEM((SUBLANES, d_ff), F32)]
    decode = None
    if decode_operands is not None:
        d_args, d_in_specs, d_out_shape, d_out_spec, d_scratch, decode = _decode_plumbing(
            *decode_operands, n_hosts=n_tiles)
        args += d_args
        in_specs += d_in_specs
        out_specs.append(d_out_spec)
        out_shape.append(d_out_shape)
        scratch += d_scratch
    return pl.pallas_call(
        functools.partial(_mix_ffn_kernel, seq_len=seq_len, dff_chunk=dff_chunk, final_norm=final_norm,
                          multi_seq=multi_seq, decode=decode),
        grid=grid,
        in_specs=in_specs,
        out_specs=out_specs,
        out_shape=out_shape,
        scratch_shapes=scratch,
        compiler_params=_cparams(("arbitrary", "arbitrary")),
        name="mix_ffn",
    )(*args)


def _pad_axis(a, axis, before, after):
    pads = [(0, 0)] * a.ndim
    pads[axis] = (before, after)
    return jnp.pad(a, pads)


def _lanes_block(parts):
    lead = parts[0][1].shape[:-1]
    out = jnp.zeros(lead + (LANES,), F32)
    for off, a in parts:
        out = lax.dynamic_update_slice_in_dim(out, a.astype(F32), off, axis=a.ndim - 1)
    return out


def kernel(x_prompt, x_sample, cache_fox_k, cache_fox_v, cache_fox_logf, page_table, state_gdn, state_gdn_conv,
           state_ffn_conv, norm_mix, w_in, gdn_conv_w, gdn_a_log, gdn_dt_bias, gdn_norm_w, fox_f_bias,
           fox_out_norm, chunk_v_norm, chunk_ws, chunk_bs, w_out, norm_ffn, ffn_w_gate, ffn_w_up, ffn_conv_w,
           ffn_w_down, norm_final):
    depth, d_model, _ = w_in.shape
    b_p, seq, _ = x_prompt.shape
    b_s, t_s, _ = x_sample.shape
    gh, fh, cg = gdn_a_log.shape[1], fox_f_bias.shape[1], chunk_ws.shape[1]
    assert gdn_norm_w.shape[1] == HEAD_DIM and gh % 2 == 0 and fh % 2 == 0 and cg % 2 == 0
    w_a, w_b, w_c = gh * HEAD_DIM, fh * HEAD_DIM, cg * HEAD_DIM
    d_ff = ffn_w_gate.shape[2]
    conv_k = gdn_conv_w.shape[1]
    ffn_k = ffn_conv_w.shape[1]
    chunk_len = chunk_ws.shape[2]
    n_phys, page = cache_fox_k.shape[1], cache_fox_k.shape[2]
    assert gh <= SUBLANES and fh <= SUBLANES and conv_k - 1 <= t_s <= GDN_CHUNK and ffn_k == 3
    assert chunk_len == LANES and seq % chunk_len == 0 and t_s <= chunk_len

    sizes = [3 * w_a, w_a, gh, gh, 3 * w_b, fh, 2 * w_c]
    starts = [0]
    for s in sizes:
        starts.append(starts[-1] + s)
    i_aqkv, i_az, i_beta, i_alpha, i_bqkv, i_bf, i_cuv = starts[:-1]
    offs, pos = {}, 0
    for name, width in [('a_qkv', 3 * w_a), ('a_z', w_a), ('b_q', w_b), ('b_k', w_b), ('b_v', w_b),
                        ('c_uv', 2 * w_c), ('small', LANES)]:
        offs[name] = (pos, pos + width)
        pos += width

    xp = x_prompt.reshape(b_p * seq, d_model)
    xs = x_sample.reshape(b_s * t_s, d_model)
    cache_kt = cache_fox_k.transpose(0, 1, 3, 4, 2).reshape(depth, n_phys, w_b, page)
    cache_vt = cache_fox_v.transpose(0, 1, 3, 4, 2).reshape(depth, n_phys, w_b, page)
    suffix = _page_suffix(cache_fox_logf.transpose(0, 3, 1, 2), math.gcd(n_phys, 512))

    tri = jnp.tril(jnp.ones((chunk_len, chunk_len), bool))
    tri_s = jnp.tril(jnp.ones((t_s, t_s), bool))
    n_rep = chunk_len // t_s
    assert (b_s * t_s) % chunk_len == 0 and chunk_len % t_s == 0

    prompt_rows, sample_rows = [], []
    w_in_t = w_in.transpose(2, 0, 1)
    kt_all = vt_all = None
    logft_rows = []
    for l in range(depth):
        w = w_in_t[:, l, :]

        def zrows(n):
            return jnp.zeros((n, d_model), F32)

        small_w = jnp.concatenate([
            w[i_beta:i_beta + gh], zrows(SM_G - SM_BETA - gh),
            w[i_alpha:i_alpha + gh], zrows(SM_F - SM_G - gh),
            w[i_bf:i_bf + fh], zrows(LANES - SM_F - fh)], axis=0)
        w_r = jnp.concatenate([w[i_aqkv:i_aqkv + 3 * w_a], w[i_az:i_az + w_a],
                               w[i_bqkv:i_bqkv + 3 * w_b], w[i_cuv:i_cuv + 2 * w_c], small_w],
                              axis=0).astype(BF16)
        par = jnp.concatenate([
            _lanes_block([(SM_G, gdn_dt_bias[l][None, :]), (SM_F, fox_f_bias[l][None, :])]),
            _lanes_block([(SM_G, gdn_a_log[l][None, :])]),
            jnp.zeros((SUBLANES - 2, LANES), F32)], axis=0)
        nm = norm_mix[l][None, :]
        gnw = jnp.tile(gdn_norm_w[l], gh)[None, :]
        fnw = jnp.tile(fox_out_norm[l], fh)[None, :]
        cvw = jnp.tile(chunk_v_norm[l], cg)[None, :]
        wo = w_out[l].astype(BF16)
        wg, wu, wd = ffn_w_gate[l].astype(BF16), ffn_w_up[l].astype(BF16), ffn_w_down[l].astype(BF16)
        nf = norm_ffn[l][None, :]
        nfin = norm_final[None, :]
        last = l == depth - 1

        aqkv, az, bq, kt_all, vt_all, cuv, small, cumt, logft = _inproj(
            xp, nm, w_r, par, offs, seq, 512, kv_stack=(l, depth, kt_all, vt_all))
        logft_rows.append(logft)
        o_a, s_new = _gdn(aqkv.reshape(b_p, seq, 3 * w_a), az.reshape(b_p, seq, w_a),
                          small.reshape(b_p, seq, LANES), jnp.zeros((b_p, SUBLANES, 3 * w_a), F32),
                          jnp.zeros((b_p, gh, HEAD_DIM, HEAD_DIM), F32), gdn_conv_w[l], gnw, seq, 512, 1)
        o_b = _fox_prompt(l, bq.reshape(b_p, seq, w_b), kt_all, vt_all, cumt, fnw, 512)
        wm = jnp.where(tri, chunk_ws[l], 0.0).astype(BF16)
        bse = jnp.repeat(chunk_bs[l].T, HEAD_DIM, axis=1)
        n_s = b_s * t_s
        aqkv_p, cuv_p = aqkv, cuv
        aqkv, az, bq, bk, bv, cuv, small, cumt, _ = _inproj(xs, nm, w_r, par, offs, t_s, n_s)
        cn8 = _pad_axis(cumt.reshape(SUBLANES, b_s, t_s).transpose(1, 0, 2), 2, 0, SUBLANES - t_s)
        decode_operands = (l, page_table, bq.reshape(b_s, t_s, w_b),
                           _pad_axis(bk.reshape(b_s, t_s, w_b), 1, 0, SUBLANES - t_s),
                           _pad_axis(bv.reshape(b_s, t_s, w_b), 1, 0, SUBLANES - t_s),
                           cn8, fnw, cache_kt, cache_vt, suffix, t_s)
        xp, tail, _, o_b = _mix_ffn(xp, o_a.reshape(b_p * seq, w_a), o_b.reshape(b_p * seq, w_b), cuv_p, cvw, wm,
                                    bse, wo, nf, wg, wu, ffn_conv_w[l], wd,
                                    jnp.zeros((b_p, SUBLANES, d_ff), F32), nfin, seq, 512, last,
                                    decode_operands=decode_operands)
        prompt_rows.append((None, None, None, s_new,
                            aqkv_p.reshape(b_p, seq, 3 * w_a)[:, seq - (conv_k - 1):, :],
                            tail[:, SUBLANES - (ffn_k - 1):, :]))

        pad_t = GDN_CHUNK - t_s
        o_a, s_new = _gdn(_pad_axis(aqkv.reshape(b_s, t_s, 3 * w_a), 1, 0, pad_t),
                          _pad_axis(az.reshape(b_s, t_s, w_a), 1, 0, pad_t),
                          _pad_axis(small.reshape(b_s, t_s, LANES), 1, 0, pad_t),
                          _pad_axis(state_gdn_conv[l], 1, SUBLANES - (conv_k - 1), 0),
                          state_gdn[l], gdn_conv_w[l], gnw, t_s, GDN_CHUNK, math.gcd(b_s, 4))
        o_a = o_a[:, :t_s, :]
        eye_rep = jnp.eye(n_rep, dtype=F32)
        wm_s = jnp.stack([jnp.kron(eye_rep, jnp.where(tri_s, chunk_ws[l, g, :t_s, :t_s], 0.0))
                          for g in range(cg)]).astype(BF16)
        bse_s = jnp.repeat(jnp.tile(chunk_bs[l, :, :t_s], (1, n_rep)).T, HEAD_DIM, axis=1)
        buf = state_ffn_conv[l]
        zrow = jnp.zeros((b_s, 1, d_ff), F32)
        pm1 = jnp.concatenate([buf[:, 1:2]] + [zrow] * (t_s - 1), axis=1).reshape(n_s, d_ff)
        pm2 = jnp.concatenate([buf[:, 0:1], buf[:, 1:2]] + [zrow] * (t_s - 2), axis=1).reshape(n_s, d_ff)
        xs, tail, cvn = _mix_ffn(xs, o_a.reshape(n_s, w_a), o_b.reshape(n_s, w_b), cuv, cvw, wm_s, bse_s,
                                 wo, nf, wg, wu, ffn_conv_w[l], wd, jnp.stack([pm1, pm2]), nfin,
                                 t_s, chunk_len, last)
        sample_rows.append((bk.reshape(b_s, t_s, fh, HEAD_DIM), bv.reshape(b_s, t_s, fh, HEAD_DIM),
                            small[:, SM_F:SM_F + fh].reshape(b_s, t_s, fh), s_new,
                            aqkv.reshape(b_s, t_s, 3 * w_a)[:, t_s - (conv_k - 1):, :],
                            tail.reshape(b_s, t_s, d_ff)[:, t_s - (ffn_k - 1):, :],
                            cvn.reshape(b_s, t_s, w_c)))

    def stack(rows_, i):
        return jnp.stack([r[i] for r in rows_])

    y_prompt = xp.reshape(b_p, seq, d_model)
    y_sample = xs.reshape(b_s, t_s, d_model)
    fox_k_prompt = kt_all.reshape(depth, b_p, fh, HEAD_DIM, seq).transpose(0, 1, 4, 2, 3)
    fox_v_prompt = vt_all.reshape(depth, b_p, fh, HEAD_DIM, seq).transpose(0, 1, 4, 2, 3)
    fox_logf_prompt = jnp.stack(logft_rows)[:, :fh, :].reshape(depth, fh, b_p, seq).transpose(0, 2, 3, 1)
    return (y_prompt, y_sample,
            fox_k_prompt, fox_v_prompt, fox_logf_prompt, stack(prompt_rows, 3),
            stack(prompt_rows, 4), stack(prompt_rows, 5),
            stack(sample_rows, 0), stack(sample_rows, 1), stack(sample_rows, 2), stack(sample_rows, 3),
            stack(sample_rows, 4), stack(sample_rows, 5), stack(sample_rows, 6))
```

```python
import functools
import math

import jax
import jax.numpy as jnp
from jax import lax
from jax.experimental import pallas as pl
from jax.experimental.pallas import tpu as pltpu

F32 = jnp.float32
BF16 = jnp.bfloat16
EPS = 1e-6
LANES = 128
SUBLANES = 8
HEAD_DIM = 64
PAIR = 2 * HEAD_DIM
GDN_CHUNK = 64
FOX_PAGES_PER_STEP = 16
FOX_PAGES_PER_SUBBLOCK = 4
VMEM_LIMIT = 62 * 1024 * 1024
HIGHEST = lax.Precision.HIGHEST
NEG_INF = -1e30

SM_BETA, SM_G, SM_F = 0, 8, 16


def _cparams(sem):
    return pltpu.CompilerParams(dimension_semantics=sem, vmem_limit_bytes=VMEM_LIMIT)


def _dot(a, b):
    return jnp.dot(a.astype(BF16), b.astype(BF16), preferred_element_type=F32)


def _dot_nt(a, b):
    return lax.dot_general(a.astype(BF16), b.astype(BF16), (((1,), (1,)), ((), ())),
                           preferred_element_type=F32)


def _dot_tn(a, b):
    return lax.dot_general(a.astype(BF16), b.astype(BF16), (((0,), (0,)), ((), ())),
                           preferred_element_type=F32)


def _dot2(a, b01):
    hi = a.astype(BF16)
    lo = (a - hi.astype(F32)).astype(BF16)
    b = b01.astype(BF16)
    return (jnp.dot(hi, b, preferred_element_type=F32) + jnp.dot(lo, b, preferred_element_type=F32))


def _split3(a):
    a1 = a.astype(BF16)
    r1 = a - a1.astype(F32)
    a2 = r1.astype(BF16)
    a3 = (r1 - a2.astype(F32)).astype(BF16)
    return a1, a2, a3


def _dot01_left(m01, a):
    m = m01.astype(BF16)
    t1, t2, t3 = _split3(a)
    return (jnp.dot(m, t1, preferred_element_type=F32) + jnp.dot(m, t2, preferred_element_type=F32)
            + jnp.dot(m, t3, preferred_element_type=F32))


def _dot01_right(a, m01):
    m = m01.astype(BF16)
    t1, t2, t3 = _split3(a)
    return (jnp.dot(t1, m, preferred_element_type=F32) + jnp.dot(t2, m, preferred_element_type=F32)
            + jnp.dot(t3, m, preferred_element_type=F32))


def _iota(shape, dim):
    return lax.broadcasted_iota(jnp.int32, shape, dim)


def _head_ones(n):
    return (_iota((n, n), 0) // HEAD_DIM == _iota((n, n), 1) // HEAD_DIM).astype(F32)


def _silu(x):
    return x * jax.nn.sigmoid(x)


def _inproj_kernel(x_ref, nw_ref, wt_ref, par_ref, *refs, offs, seq_tiles, seq_len, kv_transposed, kv_slot,
                   n_alias):
    (aqkv_ref, az_ref, bq_ref, bk_ref, bv_ref, cuv_ref, small_ref, cumt_ref, logft_ref,
     carry_ref) = refs[n_alias:]
    i = pl.program_id(0)
    tm = x_ref.shape[0]
    x = x_ref[...]
    h = (x * lax.rsqrt(jnp.mean(x * x, axis=-1, keepdims=True) + EPS) * nw_ref[...]).astype(BF16)

    y = lax.dot_general(h, wt_ref[...], (((1,), (1,)), ((), ())), preferred_element_type=F32)

    def mm(name):
        lo, hi = offs[name]
        return y[:, lo:hi]

    aqkv_ref[...] = mm('a_qkv')
    az_ref[...] = mm('a_z')
    bq_ref[...] = mm('b_q')
    if kv_transposed:
        bk_ref[kv_slot, 0] = mm('b_k').T
        bv_ref[kv_slot, 0] = mm('b_v').T
        for d in range(bk_ref.shape[0]):
            if d != kv_slot:
                bk_ref[d, 0] = jnp.zeros(bk_ref.shape[2:], F32)
                bv_ref[d, 0] = jnp.zeros(bv_ref.shape[2:], F32)
    else:
        bk_ref[...] = mm('b_k')
        bv_ref[...] = mm('b_v')
    cuv_ref[...] = mm('c_uv')

    z = mm('small') + par_ref[0:1, :]
    lane = _iota(z.shape, 1)
    t = jnp.log1p(jnp.exp(-jnp.abs(z)))
    softplus = jnp.maximum(z, 0.0) + t
    log_sig = jnp.minimum(z, 0.0) - t
    sm = jnp.where(lane < SM_G, jax.nn.sigmoid(z),
                   jnp.where(lane < SM_F, -jnp.exp(par_ref[1:2, :]) * softplus, log_sig))
    small_ref[...] = sm

    lb = min(seq_len, tm)
    row, col = _iota((tm, tm), 0), _iota((tm, tm), 1)
    tri = jnp.where((col <= row) & (row // lb == col // lb), 1.0, 0.0)
    cum = _dot01_left(tri, sm)
    if seq_tiles > 1:
        @pl.when(i % seq_tiles == 0)
        def _():
            carry_ref[...] = jnp.zeros_like(carry_ref)
        cum = cum + carry_ref[0:1, :]
        carry_ref[0:1, :] = cum[tm - 1:tm, :]
    cumt_ref[...] = cum.T[SM_F:SM_F + SUBLANES, :]
    logft_ref[...] = sm.T[SM_F:SM_F + SUBLANES, :]


def _inproj(x, nw, wt, par, offs, seq_len, tm, kv_stack=None):
    n, d = x.shape
    assert n % tm == 0 and (seq_len % tm == 0 or tm % seq_len == 0)
    seq_tiles = max(seq_len // tm, 1)
    widths = {k: hi - lo for k, (lo, hi) in offs.items()}
    names = ['a_qkv', 'a_z', 'b_q', 'b_k', 'b_v', 'c_uv', 'small']
    out_shape = [jax.ShapeDtypeStruct((n, widths[k]), F32) for k in names]
    out_specs = [pl.BlockSpec((tm, widths[k]), lambda i: (i, 0)) for k in names]
    out_shape += [jax.ShapeDtypeStruct((SUBLANES, n), F32)] * 2
    out_specs += [pl.BlockSpec((SUBLANES, tm), lambda i: (0, i))] * 2
    args = [x, nw, wt, par]
    in_specs = [pl.BlockSpec((tm, d), lambda i: (i, 0)),
                pl.BlockSpec((1, d), lambda i: (0, 0)),
                pl.BlockSpec(wt.shape, lambda i: (0, 0)),
                pl.BlockSpec(par.shape, lambda i: (0, 0))]
    aliases = {}
    kv_slot = 0
    if kv_stack is not None:
        layer, depth, kt_prev, vt_prev = kv_stack
        assert seq_len % tm == 0
        w_b = widths['b_k']
        if kt_prev is None:
            kv_slot, blk, first = layer, depth, 0
        else:
            kv_slot, blk, first = 0, 1, layer
            args += [kt_prev, vt_prev]
            in_specs += [pl.BlockSpec(memory_space=pl.ANY)] * 2
            aliases = {4: 3, 5: 4}
        for pos in (3, 4):
            out_shape[pos] = jax.ShapeDtypeStruct((depth, n // seq_len, w_b, seq_len), F32)
            out_specs[pos] = pl.BlockSpec((blk, 1, w_b, tm), lambda i: (first, i // seq_tiles, 0, i % seq_tiles))
    return pl.pallas_call(
        functools.partial(_inproj_kernel, offs=offs, seq_tiles=seq_tiles, seq_len=seq_len,
                          kv_transposed=kv_stack is not None, kv_slot=kv_slot, n_alias=len(aliases)),
        grid=(n // tm,),
        in_specs=in_specs,
        out_specs=out_specs,
        out_shape=out_shape,
        input_output_aliases=aliases,
        scratch_shapes=[pltpu.VMEM((SUBLANES, LANES), F32)],
        compiler_params=_cparams(("arbitrary",)),
        name="inproj",
    )(*args)


def _unit_lower_inverses(lmats):
    n = lmats[0].shape[0]
    row, col = _iota((n, n), 0), _iota((n, n), 1)
    eye = (row == col).astype(F32)
    base = SUBLANES
    in_base = row // base == col // base
    lds = [jnp.where(in_base, lm, 0.0) for lm in lmats]
    xs = [eye - ld for ld in lds]
    ps = [_dot(ld, ld) for ld in lds]
    xs = [x + _dot(x, p) for x, p in zip(xs, ps)]
    ps = [_dot(p, p) for p in ps]
    xs = [x + _dot(x, p) for x, p in zip(xs, ps)]
    s = base
    while s < HEAD_DIM:
        off_mask = (row // (2 * s) == col // (2 * s)) & (row // s != col // s)
        ts_ = [_dot(x, jnp.where(off_mask, lm, 0.0)) for x, lm in zip(xs, lmats)]
        xs = [x - _dot(t_, x) for x, t_ in zip(xs, ts_)]
        s *= 2
    return xs


def _gdn_kernel(aqkv_ref, az_ref, small_ref, buf_ref, s0_ref, cw_ref, nw_ref,
                o_ref, sout_ref,
                s_ref, prev_ref, qs_ref, qg_ref, k_ref, kb_ref, kbg_ref, vb_ref, gc_ref, oacc_ref,
                conv_ref, uw_ref, at_ref, *, n_pairs, valid_len, group):
    t = pl.program_id(1)
    nt = pl.num_programs(1)
    nb, ts, w3 = aqkv_ref.shape
    w_a = n_pairs * PAIR
    c = GDN_CHUNK
    rows_all = nb * ts
    cps = ts // c

    @pl.when(t == 0)
    def _():
        for bi in range(nb):
            for j in range(n_pairs):
                s_a = s0_ref[bi, 2 * j]
                s_b = s0_ref[bi, 2 * j + 1]
                top = jnp.concatenate([s_a, jnp.zeros_like(s_a)], axis=1)
                bot = jnp.concatenate([jnp.zeros_like(s_b), s_b], axis=1)
                s_ref[bi, j] = jnp.concatenate([top, bot], axis=0)
        prev_ref[...] = buf_ref[...]

    x = aqkv_ref[...].reshape(rows_all, w3)
    kw = cw_ref.shape[0]
    conv = cw_ref[kw - 1:kw, :] * x
    for i in range(1, kw):
        conv = conv + cw_ref[kw - 1 - i:kw - i, :] * pltpu.roll(x, i, axis=0)
    conv_ref[...] = conv
    for bi in range(nb):
        head = jnp.concatenate([prev_ref[bi], x[bi * ts:bi * ts + SUBLANES]], axis=0)
        fixed = cw_ref[0:1, :] * head[SUBLANES - kw + 1:2 * SUBLANES - kw + 1]
        for i in range(1, kw):
            fixed = fixed + cw_ref[i:i + 1, :] * head[SUBLANES - kw + 1 + i:2 * SUBLANES - kw + 1 + i]
        conv_ref[bi * ts:bi * ts + SUBLANES, :] = fixed
        prev_ref[bi] = x[(bi + 1) * ts - SUBLANES:(bi + 1) * ts]
    a = _silu(conv_ref[...])
    rowid = _iota((rows_all, 1), 0) % ts + t * ts
    valid = (rowid < valid_len).astype(F32)
    ones_h = _head_ones(w_a)
    q = a[:, 0:w_a]
    k = a[:, w_a:2 * w_a]
    v = a[:, 2 * w_a:3 * w_a] * valid
    qn = q * lax.rsqrt(_dot2(q * q, ones_h) + EPS) * valid
    kn = k * lax.rsqrt(_dot2(k * k, ones_h) + EPS) * valid

    sm = small_ref[...].reshape(rows_all, LANES) * valid
    row, col = _iota((rows_all, rows_all), 0), _iota((rows_all, rows_all), 1)
    tri = jnp.where((col <= row) & (row // c == col // c), 1.0, 0.0)
    gcum = _dot01_left(tri, sm)
    gc_ref[...] = gcum
    lane, hrow = _iota((LANES, w_a), 1), _iota((LANES, w_a), 0)
    exp_b = (hrow == lane // HEAD_DIM + SM_BETA).astype(F32)
    exp_g = (hrow == lane // HEAD_DIM + SM_G).astype(F32)
    beta_e = _dot2(sm, exp_b)
    gcum_e = _dot01_right(gcum, exp_g)
    eg = jnp.exp(gcum_e)
    scale = HEAD_DIM ** -0.5
    qs_ref[...] = qn * scale
    qg_ref[...] = qn * (scale * eg)
    k_ref[...] = kn
    kb = kn * beta_e
    kb_ref[...] = kb
    kbg_ref[...] = kb * eg
    vb_ref[...] = v * beta_e

    r2, c2 = _iota((PAIR, PAIR), 0), _iota((PAIR, PAIR), 1)
    same_head = (r2 // HEAD_DIM) == (c2 // HEAD_DIM)
    strict = same_head & (c2 < r2)
    rw, cw_ = _iota((c, PAIR), 0), _iota((c, PAIR), 1)
    incl_wide = (cw_ % HEAD_DIM) <= rw
    lane_lo = _iota((c, PAIR), 1) < HEAD_DIM

    def stack(m):
        return jnp.concatenate([jnp.where(lane_lo, m, 0.0), jnp.where(lane_lo, 0.0, m)], axis=0)

    def gate_cols(g_c, j):
        return g_c[:, SM_G + 2 * j:SM_G + 2 * j + 1], g_c[:, SM_G + 2 * j + 1:SM_G + 2 * j + 2]

    def prepare(gi, carry):
        items = []
        for i in range(group):
            ci = gi * group + i
            r0 = pl.multiple_of(ci * c, c)
            g_c = gc_ref[pl.ds(r0, c), :]
            g_t = g_c.T
            for j in range(n_pairs):
                ls = slice(j * PAIR, (j + 1) * PAIR)
                ga_col, gb_col = gate_cols(g_c, j)
                g_row = jnp.concatenate([g_t[SM_G + 2 * j:SM_G + 2 * j + 1, :],
                                         g_t[SM_G + 2 * j + 1:SM_G + 2 * j + 2, :]], axis=1)
                g_col2 = jnp.concatenate([ga_col, gb_col], axis=0)
                g_wide = jnp.where(lane_lo, ga_col, gb_col)
                items.append(dict(
                    ci=ci, j=j,
                    k2=stack(k_ref[pl.ds(r0, c), ls]),
                    kb2=stack(kb_ref[pl.ds(r0, c), ls]),
                    rhs=jnp.concatenate([stack(vb_ref[pl.ds(r0, c), ls]), stack(kbg_ref[pl.ds(r0, c), ls])],
                                        axis=1),
                    qs=qs_ref[pl.ds(r0, c), ls],
                    decay2=jnp.exp(jnp.minimum(g_col2 - g_row, 0.0)),
                    decay_w=jnp.exp(jnp.minimum(g_wide - g_row, 0.0))))
        kk = [_dot_nt(it['kb2'], it['k2']) for it in items]
        qk = [_dot_nt(it['qs'], it['k2']) for it in items]
        lmats = [jnp.where(strict, m * it['decay2'], 0.0) for m, it in zip(kk, items)]
        for m, it in zip(qk, items):
            at_ref[it['ci'], it['j']] = jnp.where(incl_wide, m * it['decay_w'], 0.0)
        tinvs = _unit_lower_inverses(lmats)
        sols = [_dot(tinv, it['rhs']) for tinv, it in zip(tinvs, items)]
        for sol, it in zip(sols, items):
            uw_ref[it['ci'], it['j']] = sol
        return carry

    lax.fori_loop(0, rows_all // c // group, prepare, 0)

    def recur(n, carry):
        items = []
        for bi in range(nb):
            ci = bi * cps + n
            r0 = pl.multiple_of(ci * c, c)
            g_c = gc_ref[pl.ds(r0, c), :]
            for j in range(n_pairs):
                ls = slice(j * PAIR, (j + 1) * PAIR)
                ga_col, gb_col = gate_cols(g_c, j)
                g_wide = jnp.where(lane_lo, ga_col, gb_col)
                g_last = jnp.where(lane_lo, ga_col[c - 1:c, :], gb_col[c - 1:c, :])
                g_last_col = jnp.where(_iota((PAIR, 1), 0) < HEAD_DIM, ga_col[c - 1:c, :], gb_col[c - 1:c, :])
                items.append(dict(
                    bi=bi, r0=r0, ls=ls, j=j,
                    u2=uw_ref[ci, j, :, 0:PAIR], w2=uw_ref[ci, j, :, PAIR:2 * PAIR], attn=at_ref[ci, j],
                    s2=s_ref[bi, j], qg=qg_ref[pl.ds(r0, c), ls],
                    kdec=k_ref[pl.ds(r0, c), ls] * jnp.exp(g_last - g_wide),
                    s_decay=jnp.exp(g_last_col)))
        ws = [_dot(it['w2'], it['s2']) for it in items]
        qss = [_dot(it['qg'], it['s2']) for it in items]
        vnew2 = [it['u2'] - w for it, w in zip(items, ws)]
        avs = [_dot(it['attn'], v2) for it, v2 in zip(items, vnew2)]
        kvs = [_dot_tn(it['kdec'], v2[:c, :] + v2[c:, :]) for it, v2 in zip(items, vnew2)]
        for it, q_s, av, kv in zip(items, qss, avs, kvs):
            oacc_ref[pl.ds(it['r0'], c), it['ls']] = q_s + av
            s_ref[it['bi'], it['j']] = it['s2'] * it['s_decay'] + jnp.where(same_head, kv, 0.0)
        return carry

    lax.fori_loop(0, cps, recur, 0)

    o = oacc_ref[...]
    on = o * lax.rsqrt(_dot2(o * o, ones_h) * (1.0 / HEAD_DIM) + EPS) * nw_ref[...]
    o_ref[...] = (on * _silu(az_ref[...].reshape(rows_all, w_a))).reshape(nb, ts, w_a)

    @pl.when(t == nt - 1)
    def _():
        for bi in range(nb):
            for j in range(n_pairs):
                s2 = s_ref[bi, j]
                sout_ref[bi, 2 * j] = s2[:HEAD_DIM, :HEAD_DIM]
                sout_ref[bi, 2 * j + 1] = s2[HEAD_DIM:, HEAD_DIM:]


def _gdn(aqkv, az, small, bufpad, s0, conv_w, norm_w_tiled, valid_len, ts, nb):
    b, l, w3 = aqkv.shape
    w_a = w3 // 3
    n_pairs = w_a // PAIR
    heads = s0.shape[1]
    n_chunks = nb * ts // GDN_CHUNK
    group = math.gcd(n_chunks, 4)
    assert l % ts == 0 and ts % GDN_CHUNK == 0 and heads == 2 * n_pairs and b % nb == 0
    rows_all = nb * ts
    scratch = [pltpu.VMEM((nb, n_pairs, PAIR, PAIR), F32), pltpu.VMEM((nb, SUBLANES, w3), F32)]
    scratch += [pltpu.VMEM((rows_all, w_a), F32) for _ in range(6)]
    scratch += [pltpu.VMEM((rows_all, LANES), F32), pltpu.VMEM((rows_all, w_a), F32),
                pltpu.VMEM((rows_all, w3), F32),
                pltpu.VMEM((n_chunks, n_pairs, PAIR, 2 * PAIR), F32),
                pltpu.VMEM((n_chunks, n_pairs, GDN_CHUNK, PAIR), F32)]
    return pl.pallas_call(
        functools.partial(_gdn_kernel, n_pairs=n_pairs, valid_len=valid_len, group=group),
        grid=(b // nb, l // ts),
        in_specs=[pl.BlockSpec((nb, ts, w3), lambda i, t: (i, t, 0)),
                  pl.BlockSpec((nb, ts, w_a), lambda i, t: (i, t, 0)),
                  pl.BlockSpec((nb, ts, LANES), lambda i, t: (i, t, 0)),
                  pl.BlockSpec((nb, SUBLANES, w3), lambda i, t: (i, 0, 0)),
                  pl.BlockSpec((nb, heads, HEAD_DIM, HEAD_DIM), lambda i, t: (i, 0, 0, 0)),
                  pl.BlockSpec(conv_w.shape, lambda i, t: (0, 0)),
                  pl.BlockSpec(norm_w_tiled.shape, lambda i, t: (0, 0))],
        out_specs=[pl.BlockSpec((nb, ts, w_a), lambda i, t: (i, t, 0)),
                   pl.BlockSpec((nb, heads, HEAD_DIM, HEAD_DIM), lambda i, t: (i, 0, 0, 0))],
        out_shape=[jax.ShapeDtypeStruct((b, l, w_a), F32),
                   jax.ShapeDtypeStruct((b, heads, HEAD_DIM, HEAD_DIM), F32)],
        scratch_shapes=scratch,
        compiler_params=_cparams(("arbitrary", "arbitrary")),
        name="gdn",
    )(aqkv, az, small, bufpad, s0, conv_w, norm_w_tiled)


def _fox_prompt_kernel(q_ref, kt_ref, vt_ref, cumt_ref, nw_ref, o_ref, *, n_pairs):
    i = pl.program_id(1)
    tq = q_ref.shape[1]
    tk = tq
    scale = HEAD_DIM ** -0.5
    lane_lo = _iota((tq, PAIR), 1) < HEAD_DIM
    causal = _iota((tq, tk), 1) <= _iota((tq, tk), 0)
    q0 = pl.multiple_of(i * tq, tq)
    pairs = []
    for j in range(n_pairs):
        ls = slice(j * PAIR, (j + 1) * PAIR)
        q_p = q_ref[0, :, ls] * scale
        heads = (2 * j, 2 * j + 1)
        q_hs = [jnp.where(lane_lo, q_p, 0.0).astype(BF16), jnp.where(lane_lo, 0.0, q_p).astype(BF16)]
        c_refs = [cumt_ref[h:h + 1, pl.ds(q0, LANES)][:, 0:1] for h in heads]

        def block(k0, carry, masked, q_hs=q_hs, c_refs=c_refs, heads=heads, ls=ls):
            k_blk = kt_ref[0, 0, ls, pl.ds(k0, tk)].astype(BF16)
            v_blk = vt_ref[0, 0, ls, pl.ds(k0, tk)].astype(BF16)
            ss = [jnp.dot(q_h, k_blk, preferred_element_type=F32) for q_h in q_hs]
            ss = [s + (c_ref - cumt_ref[h:h + 1, pl.ds(k0, tk)]) for s, c_ref, h in zip(ss, c_refs, heads)]
            if masked:
                ss = [jnp.where(causal, s, NEG_INF) for s in ss]
            m_news = [jnp.maximum(st[0], jnp.max(s, axis=-1, keepdims=True)) for st, s in zip(carry, ss)]
            ps = [jnp.exp(s - m_new) for s, m_new in zip(ss, m_news)]
            alphas = [jnp.exp(st[0] - m_new) for st, m_new in zip(carry, m_news)]
            pvs = [lax.dot_general(p.astype(BF16), v_blk, (((1,), (1,)), ((), ())), preferred_element_type=F32)
                   for p in ps]
            return tuple((m_new, a * st[1] + jnp.sum(p, axis=-1, keepdims=True), a * st[2] + pv)
                         for st, m_new, a, p, pv in zip(carry, m_news, alphas, ps, pvs))

        init = tuple((jnp.full((tq, 1), NEG_INF, F32), jnp.zeros((tq, 1), F32), jnp.zeros((tq, PAIR), F32))
                     for _ in heads)
        carry = lax.fori_loop(
            0, i, lambda kb, c_, block=block: block(pl.multiple_of(kb * tk, tk), c_, False), init)
        outs = [acc / l for _, l, acc in block(q0, carry, True)]
        pairs.append(jnp.where(lane_lo, outs[0], outs[1]))
    o = jnp.concatenate(pairs, axis=1)
    ones_h = _head_ones(o.shape[1])
    o_ref[0] = o * lax.rsqrt(_dot2(o * o, ones_h) * (1.0 / HEAD_DIM) + EPS) * nw_ref[...]


def _fox_prompt(layer, q, kt, vt, cumt, norm_w_tiled, tq):
    b, l, w_b = q.shape
    assert l % tq == 0 and tq % LANES == 0
    return pl.pallas_call(
        functools.partial(_fox_prompt_kernel, n_pairs=w_b // PAIR),
        grid=(b, l // tq),
        in_specs=[pl.BlockSpec((1, tq, w_b), lambda i, t: (i, t, 0)),
                  pl.BlockSpec((1, 1, w_b, l), lambda i, t: (layer, i, 0, 0)),
                  pl.BlockSpec((1, 1, w_b, l), lambda i, t: (layer, i, 0, 0)),
                  pl.BlockSpec((SUBLANES, l), lambda i, t: (0, i)),
                  pl.BlockSpec(norm_w_tiled.shape, lambda i, t: (0, 0))],
        out_specs=pl.BlockSpec((1, tq, w_b), lambda i, t: (i, t, 0)),
        out_shape=jax.ShapeDtypeStruct((b, l, w_b), F32),
        compiler_params=_cparams(("arbitrary", "arbitrary")),
        name="fox_prompt",
    )(q, kt, vt, cumt, norm_w_tiled)


def _page_suffix_kernel(x_ref, m_ref, o_ref):
    n_heads = x_ref.shape[1]
    for h in range(SUBLANES):
        if h < n_heads:
            o_ref[0, :, h, :] = _dot01_right(x_ref[0, h], m_ref[...])
        else:
            o_ref[0, :, h, :] = jnp.zeros((o_ref.shape[1], o_ref.shape[3]), F32)


def _page_suffix(logf_t, tp):
    depth, n_heads, n_phys, page = logf_t.shape
    assert page == LANES and n_phys % tp == 0
    idx = jnp.arange(page)
    m = (idx[:, None] >= idx[None, :]).astype(BF16)
    return pl.pallas_call(
        _page_suffix_kernel,
        grid=(depth, n_phys // tp),
        in_specs=[pl.BlockSpec((1, n_heads, tp, page), lambda d, i: (d, 0, i, 0)),
                  pl.BlockSpec(m.shape, lambda d, i: (0, 0))],
        out_specs=pl.BlockSpec((1, tp, SUBLANES, page), lambda d, i: (d, i, 0, 0)),
        out_shape=jax.ShapeDtypeStruct((depth, n_phys, SUBLANES, page), F32),
        compiler_params=_cparams(("arbitrary", "arbitrary")),
        name="page_suffix",
    )(logf_t, m)


def _run_if(cond, fn):
    if isinstance(cond, bool):
        if cond:
            fn()
    else:
        pl.when(cond)(fn)


def _decode_round_scores(step, n_steps, refs, *, layer, n_new, pps, sub_pages, nj, j_static, maybe_first,
                         maybe_last):
    (pt_ref, q_ref, kn_ref, vn_ref, cn_ref, nw_ref, kt_hbm, vt_hbm, sfx_hbm, o_ref,
     kbuf, vbuf, sbuf, ksem, vsem, ssem, qbd_ref, m_ref, l_ref, acc_ref, carry_ref) = refs
    bb = step // nj
    j = step % nj if j_static is None else j_static
    n_tok = q_ref.shape[1]
    w_b = q_ref.shape[2]
    rows = n_tok * SUBLANES
    page = kbuf.shape[3]
    n_pages = nj * pps
    scale = HEAD_DIM ** -0.5
    slot = step % 2

    def page_copies(step_x, slot_x, i):
        pg = pt_ref[step_x // nj, n_pages - 1 - ((step_x % nj) * pps + i)]
        return (pltpu.make_async_copy(kt_hbm.at[layer, pg], kbuf.at[slot_x, i], ksem.at[slot_x]),
                pltpu.make_async_copy(vt_hbm.at[layer, pg], vbuf.at[slot_x, i], vsem.at[slot_x]),
                pltpu.make_async_copy(sfx_hbm.at[layer, pg], sbuf.at[slot_x, i], ssem.at[slot_x]))

    def request(step_x, slot_x):
        for i in range(pps):
            for cp in page_copies(step_x, slot_x, i):
                cp.start()

    if maybe_first:
        def prime():
            request(step, slot)
            if n_steps > 1:
                request(step + 1, 1 - slot)
        _run_if(step == 0, prime)

    for i in range(pps):
        for cp in page_copies(step, slot, i):
            cp.wait()

    def start_sequence():
        q = q_ref[bb] * scale
        headmask = (_iota((SUBLANES, w_b), 1) // HEAD_DIM == _iota((SUBLANES, w_b), 0)).astype(F32)
        qbd = jnp.concatenate([q[t:t + 1, :] * headmask for t in range(n_tok)], axis=0)
        qbd_ref[...] = qbd.astype(BF16)
        kn = kn_ref[bb]
        s = _dot_nt(qbd, kn)
        cn = cn_ref[bb]
        s = s - jnp.concatenate([cn] * n_tok, axis=0)
        kidx = _iota((rows, SUBLANES), 1)
        tidx = _iota((rows, SUBLANES), 0) // SUBLANES
        s = jnp.where((kidx <= tidx) & (kidx < n_new), s, NEG_INF)
        m = jnp.max(s, axis=-1, keepdims=True)
        p = jnp.exp(s - m)
        m_ref[...] = m
        l_ref[...] = jnp.sum(p, axis=-1, keepdims=True)
        acc_ref[...] = _dot(p, vn_ref[bb])
        carry_ref[...] = jnp.zeros_like(carry_ref)

    _run_if(j == 0, start_sequence)

    lane = _iota((SUBLANES, page), 1)
    carry = carry_ref[...]
    biases = []
    for i in range(pps):
        incl = sbuf[slot, i]
        excl = jnp.where(lane < page - 1, pltpu.roll(incl, page - 1, axis=1), 0.0)
        biases.append(excl + carry)
        carry = carry + incl[:, 0:1]
    carry_ref[...] = carry
    qbd = qbd_ref[...]
    n_sub = pps // sub_pages
    subs = [range(u * sub_pages, (u + 1) * sub_pages) for u in range(n_sub)]
    kts = [jnp.concatenate([kbuf[slot, i].astype(BF16) for i in sb], axis=1) for sb in subs]
    ss = [jnp.dot(qbd, kt, preferred_element_type=F32) for kt in kts]
    ss = [s + jnp.concatenate([jnp.concatenate([biases[i] for i in sb], axis=1)] * n_tok, axis=0)
          for s, sb in zip(ss, subs)]
    ms = [jnp.max(s, axis=-1, keepdims=True) for s in ss]
    ps = [jnp.exp(s - m) for s, m in zip(ss, ms)]
    ls = [jnp.sum(p, axis=-1, keepdims=True) for p in ps]
    refill = functools.partial(_run_if, step + 2 < n_steps if maybe_last else True,
                               lambda: request(step + 2, slot))
    return dict(refs=refs, slot=slot, subs=subs, ms=ms, ps=ps, ls=ls, bb=bb, last_of_sequence=j == nj - 1,
                refill=refill)


def _decode_round_values(state):
    (_, q_ref, _, _, _, nw_ref, _, _, _, o_ref,
     _, vbuf, _, _, _, _, _, m_ref, l_ref, acc_ref, _) = state['refs']
    slot, subs, ms, ps, ls, bb = (state[k] for k in ('slot', 'subs', 'ms', 'ps', 'ls', 'bb'))
    n_tok = q_ref.shape[1]
    w_b = q_ref.shape[2]
    rows = n_tok * SUBLANES
    vts = [jnp.concatenate([vbuf[slot, i].astype(BF16) for i in sb], axis=1) for sb in subs]
    pvs = [lax.dot_general(p.astype(BF16), vt, (((1,), (1,)), ((), ())), preferred_element_type=F32)
           for p, vt in zip(ps, vts)]
    state['refill']()
    m_old = m_ref[...]
    m_new = m_old
    for m in ms:
        m_new = jnp.maximum(m_new, m)
    alpha = jnp.exp(m_old - m_new)
    l_new = alpha * l_ref[...]
    acc = alpha * acc_ref[...]
    for m, l_u, pv in zip(ms, ls, pvs):
        w_u = jnp.exp(m - m_new)
        l_new = l_new + w_u * l_u
        acc = acc + w_u * pv
    m_ref[...] = m_new
    l_ref[...] = l_new
    acc_ref[...] = acc

    def finish_sequence():
        o = acc_ref[...] / l_ref[...]
        diag = (_iota((rows, w_b), 1) // HEAD_DIM) == (_iota((rows, w_b), 0) % SUBLANES)
        o = jnp.where(diag, o, 0.0)
        group = (_iota((SUBLANES, rows), 1) // SUBLANES == _iota((SUBLANES, rows), 0)).astype(F32)
        o_tok = jnp.dot(group, o, precision=HIGHEST, preferred_element_type=F32)
        ones_h = _head_ones(w_b)
        ss = jnp.dot(o_tok * o_tok, ones_h, precision=HIGHEST, preferred_element_type=F32)
        on = o_tok * lax.rsqrt(ss * (1.0 / HEAD_DIM) + EPS) * nw_ref[...]
        o_ref[bb] = on[0:n_tok, :]

    _run_if(state['last_of_sequence'], finish_sequence)


N_DECODE_INPUTS = 9
N_DECODE_SCRATCH = 11


def _decode_plumbing(layer, page_table, q, k_new8, v_new8, cn8, norm_w_tiled, cache_kt, cache_vt, suffix, n_new,
                     n_hosts):
    b, n_tok, w_b = q.shape
    page = cache_kt.shape[3]
    n_pages = page_table.shape[1]
    pps = math.gcd(n_pages, FOX_PAGES_PER_STEP)
    sub_pages = math.gcd(pps, FOX_PAGES_PER_SUBBLOCK)
    nj = n_pages // pps
    assert (b * nj) % n_hosts == 0
    rows = n_tok * SUBLANES
    n_slots = 2

    def whole(a):
        nd = a.ndim
        return pl.BlockSpec(a.shape, lambda *_: (0,) * nd)

    args = [page_table, q, k_new8, v_new8, cn8, norm_w_tiled, cache_kt, cache_vt, suffix]
    in_specs = [pl.BlockSpec(memory_space=pltpu.SMEM), whole(q), whole(k_new8), whole(v_new8), whole(cn8),
                whole(norm_w_tiled), pl.BlockSpec(memory_space=pl.ANY), pl.BlockSpec(memory_space=pl.ANY),
                pl.BlockSpec(memory_space=pl.ANY)]
    assert len(args) == N_DECODE_INPUTS
    out_shape = jax.ShapeDtypeStruct((b, n_tok, w_b), F32)
    out_spec = pl.BlockSpec((b, n_tok, w_b), lambda *_: (0, 0, 0))
    scratch = [pltpu.VMEM((n_slots, pps, w_b, page), F32), pltpu.VMEM((n_slots, pps, w_b, page), F32),
               pltpu.VMEM((n_slots, pps, SUBLANES, page), F32),
               pltpu.SemaphoreType.DMA((n_slots,)), pltpu.SemaphoreType.DMA((n_slots,)),
               pltpu.SemaphoreType.DMA((n_slots,)),
               pltpu.VMEM((rows, w_b), BF16), pltpu.VMEM((rows, 1), F32), pltpu.VMEM((rows, 1), F32),
               pltpu.VMEM((rows, w_b), F32), pltpu.VMEM((SUBLANES, page), F32)]
    assert len(scratch) == N_DECODE_SCRATCH
    config = dict(layer=layer, n_new=n_new, pps=pps, sub_pages=sub_pages, nj=nj,
                  rounds_per_host=b * nj // n_hosts, n_steps=b * nj)
    return args, in_specs, out_shape, out_spec, scratch, config


def _rms(x, w):
    return x * lax.rsqrt(jnp.mean(x * x, axis=-1, keepdims=True) + EPS) * w


def _gelu_tanh(x):
    return x * (0.5 * (1.0 + jnp.tanh(math.sqrt(2.0 / math.pi) * (x + 0.044715 * (x * x * x)))))


def _mix_ffn_kernel(x_ref, oa_ref, ob_ref, cuv_ref, cvw_ref, wm_ref, bse_ref, wo_ref, nf_ref,
                    wg_ref, wu_ref, cw_ref, wd_ref, pm_ref, nfin_ref, *refs,
                    seq_len, dff_chunk, final_norm, multi_seq, decode):
    if decode is None:
        y_ref, tail_ref, cv_ref, carry_ref = refs
        decode_refs = None
    else:
        dec_in, rest = refs[:N_DECODE_INPUTS], refs[N_DECODE_INPUTS:]
        y_ref, tail_ref, dec_out, carry_ref = rest[:4]
        cv_ref = None
        decode_refs = tuple(dec_in) + (dec_out,) + tuple(rest[4:])
    t = pl.program_id(1)
    tm = x_ref.shape[0]
    w_a, w_b = oa_ref.shape[1], ob_ref.shape[1]
    w_c = cuv_ref.shape[1] // 2
    d_ff = wg_ref.shape[1]
    tail_rows = tail_ref.shape[1]

    gel = _gelu_tanh(cuv_ref[...])
    cu, cv = gel[:, :w_c], gel[:, w_c:]
    ones_h = _head_ones(w_c)
    cvn = cv * lax.rsqrt(_dot2(cv * cv, ones_h) * (1.0 / HEAD_DIM) + EPS) * cvw_ref[...]
    if cv_ref is not None:
        cv_ref[...] = cvn
    cvn_b = cvn.astype(BF16)
    lane_lo = _iota((LANES, PAIR), 1) < HEAD_DIM
    row_blocks = []
    for n in range(tm // LANES):
        rs = slice(n * LANES, (n + 1) * LANES)
        pair_blocks = []
        for j in range(w_c // PAIR):
            ls = slice(j * PAIR, (j + 1) * PAIR)
            v_p = cvn_b[rs, ls]
            mix_a = jnp.dot(wm_ref[2 * j], v_p, preferred_element_type=F32)
            mix_b = jnp.dot(wm_ref[2 * j + 1], v_p, preferred_element_type=F32)
            pair_blocks.append(cu[rs, ls] * (jnp.where(lane_lo, mix_a, mix_b) + bse_ref[:, ls]))
        row_blocks.append(jnp.concatenate(pair_blocks, axis=1))
    o_c = jnp.concatenate(row_blocks, axis=0)

    mix = jnp.concatenate([oa_ref[...].astype(BF16), ob_ref[...].astype(BF16), o_c.astype(BF16)], axis=1)
    x1 = x_ref[...] + jnp.dot(mix, wo_ref[...], preferred_element_type=F32)
    h2 = _rms(x1, nf_ref[...]).astype(BF16)

    row = _iota((tm, 1), 0)
    if multi_seq:
        pos = row % seq_len
    else:
        pos = row

        @pl.when(t == 0)
        def _():
            carry_ref[...] = pm_ref[0]

    def project(c0):
        cs = slice(c0, c0 + dff_chunk)
        return (jnp.dot(h2, wg_ref[:, cs], preferred_element_type=F32),
                jnp.dot(h2, wu_ref[:, cs], preferred_element_type=F32))

    n_chunks = d_ff // dff_chunk
    rounds_at = [[] for _ in range(n_chunks)]
    if decode is not None:
        rph = decode['rounds_per_host']
        first_round = (pl.program_id(0) * pl.num_programs(1) + t) * rph
        for r in range(rph):
            rounds_at[r * n_chunks // rph].append(r)
        round_cfg = {k: decode[k] for k in ('layer', 'n_new', 'pps', 'sub_pages', 'nj')}
        whole_sequences = rph % decode['nj'] == 0

    acc = jnp.zeros((tm, x_ref.shape[1]), F32)
    nxt = project(0)
    for ci, c0 in enumerate(range(0, d_ff, dff_chunk)):
        cs = slice(c0, c0 + dff_chunk)
        gp, up = nxt
        def scores(r):
            return _decode_round_scores(first_round + r, decode['n_steps'], decode_refs,
                                        j_static=r % decode['nj'] if whole_sequences else None,
                                        maybe_first=r == 0, maybe_last=r >= rph - 2, **round_cfg)

        hosted = rounds_at[ci]
        state = scores(hosted[0]) if hosted else None
        if c0 + dff_chunk < d_ff:
            nxt = project(c0 + dff_chunk)
        if hosted:
            _decode_round_values(state)
        for r in hosted[1:]:
            _decode_round_values(scores(r))
        if multi_seq:
            first1, first2 = pm_ref[0, :, cs], pm_ref[1, :, cs]
        else:
            c6, c7 = carry_ref[SUBLANES - 2:SUBLANES - 1, cs], carry_ref[SUBLANES - 1:SUBLANES, cs]
            first1 = c7
            first2 = jnp.where(row == 0, c6, c7)
            carry_ref[:, cs] = gp[tm - SUBLANES:tm, :]
        gm1 = jnp.where(pos >= 1, pltpu.roll(gp, 1, axis=0), first1)
        gm2 = jnp.where(pos >= 2, pltpu.roll(gp, 2, axis=0), first2)
        gate = cw_ref[0:1, cs] * gm2 + cw_ref[1:2, cs] * gm1 + cw_ref[2:3, cs] * gp
        act = (_silu(gate) * up).astype(BF16)
        acc = acc + jnp.dot(act, wd_ref[cs, :], preferred_element_type=F32)
        tail_ref[0, :, cs] = gp[tm - tail_rows:tm, :]
    x2 = x1 + acc
    if final_norm:
        x2 = _rms(x2, nfin_ref[...])
    y_ref[...] = x2


def _mix_ffn(x, oa, ob, cuv, cvw, wm, bse, wo, nf, wg, wu, cw, wd, pm, nfin, seq_len, tm, final_norm,
             decode_operands=None):
    n, d = x.shape
    multi_seq = seq_len < tm
    assert n % tm == 0 and (tm % seq_len == 0 if multi_seq else seq_len % tm == 0)
    assert tm % LANES == 0 and wm.shape[1] == LANES
    n_seq = 1 if multi_seq else n // seq_len
    seq_tiles = 1 if multi_seq else seq_len // tm
    n_tiles = n // tm
    d_ff = wg.shape[1]
    dff_chunk = 256
    assert d_ff % dff_chunk == 0
    tail_rows = tm if multi_seq else SUBLANES
    w_c = cuv.shape[1] // 2

    def rows(w):
        return pl.BlockSpec((tm, w), lambda s, t: (s * seq_tiles + t, 0))

    def const(a):
        nd = a.ndim
        return pl.BlockSpec(a.shape, lambda s, t: (0,) * nd, pipeline_mode=pl.Buffered(1))

    if multi_seq:
        pm_spec = pl.BlockSpec((2, tm, d_ff), lambda s, t: (0, s * seq_tiles + t, 0))
        grid = (n_tiles, 1)
    else:
        pm_spec = pl.BlockSpec((1, SUBLANES, d_ff), lambda s, t: (s, 0, 0))
        grid = (n_seq, seq_tiles)
    n_tail = n_tiles if multi_seq else n_seq
    args = [x, oa, ob, cuv, cvw, wm, bse, wo, nf, wg, wu, cw, wd, pm, nfin]
    in_specs = [rows(d), rows(oa.shape[1]), rows(ob.shape[1]), rows(cuv.shape[1]),
                const(cvw), const(wm), const(bse), const(wo), const(nf), const(wg), const(wu), const(cw),
                const(wd), pm_spec, const(nfin)]
    out_specs = [rows(d), pl.BlockSpec((1, tail_rows, d_ff), lambda s, t: (s, 0, 0)), rows(w_c)]
    out_shape = [jax.ShapeDtypeStruct((n, d), F32),
                 jax.ShapeDtypeStruct((n_tail, tail_rows, d_ff), F32),
                 jax.ShapeDtypeStruct((n, w_c), F32)]
    scratch = [pltpu.VMEM((SUBLANES, d_ff), F32)]
    decode = None
    if decode_operands is not None:
        d_args, d_in_specs, d_out_shape, d_out_spec, d_scratch, decode = _decode_plumbing(
            *decode_operands, n_hosts=n_tiles)
        args += d_args
        in_specs += d_in_specs
        out_specs[2] = d_out_spec
        out_shape[2] = d_out_shape
        scratch += d_scratch
    return pl.pallas_call(
        functools.partial(_mix_ffn_kernel, seq_len=seq_len, dff_chunk=dff_chunk, final_norm=final_norm,
                          multi_seq=multi_seq, decode=decode),
        grid=grid,
        in_specs=in_specs,
        out_specs=out_specs,
        out_shape=out_shape,
        scratch_shapes=scratch,
        compiler_params=_cparams(("arbitrary", "arbitrary")),
        name="mix_ffn",
    )(*args)


def _pad_axis(a, axis, before, after):
    pads = [(0, 0)] * a.ndim
    pads[axis] = (before, after)
    return jnp.pad(a, pads)


def _lanes_block(parts):
    lead = parts[0][1].shape[:-1]
    out = jnp.zeros(lead + (LANES,), F32)
    for off, a in parts:
        out = lax.dynamic_update_slice_in_dim(out, a.astype(F32), off, axis=a.ndim - 1)
    return out


def kernel(x_prompt, x_sample, cache_fox_k, cache_fox_v, cache_fox_logf, page_table, state_gdn, state_gdn_conv,
           state_ffn_conv, norm_mix, w_in, gdn_conv_w, gdn_a_log, gdn_dt_bias, gdn_norm_w, fox_f_bias,
           fox_out_norm, chunk_v_norm, chunk_ws, chunk_bs, w_out, norm_ffn, ffn_w_gate, ffn_w_up, ffn_conv_w,
           ffn_w_down, norm_final):
    depth, d_model, _ = w_in.shape
    b_p, seq, _ = x_prompt.shape
    b_s, t_s, _ = x_sample.shape
    gh, fh, cg = gdn_a_log.shape[1], fox_f_bias.shape[1], chunk_ws.shape[1]
    assert gdn_norm_w.shape[1] == HEAD_DIM and gh % 2 == 0 and fh % 2 == 0 and cg % 2 == 0
    w_a, w_b, w_c = gh * HEAD_DIM, fh * HEAD_DIM, cg * HEAD_DIM
    d_ff = ffn_w_gate.shape[2]
    conv_k = gdn_conv_w.shape[1]
    ffn_k = ffn_conv_w.shape[1]
    chunk_len = chunk_ws.shape[2]
    n_phys, page = cache_fox_k.shape[1], cache_fox_k.shape[2]
    assert gh <= SUBLANES and fh <= SUBLANES and conv_k - 1 <= t_s <= GDN_CHUNK and ffn_k == 3
    assert chunk_len == LANES and seq % chunk_len == 0 and t_s <= chunk_len

    sizes = [3 * w_a, w_a, gh, gh, 3 * w_b, fh, 2 * w_c]
    starts = [0]
    for s in sizes:
        starts.append(starts[-1] + s)
    i_aqkv, i_az, i_beta, i_alpha, i_bqkv, i_bf, i_cuv = starts[:-1]
    offs, pos = {}, 0
    for name, width in [('a_qkv', 3 * w_a), ('a_z', w_a), ('b_q', w_b), ('b_k', w_b), ('b_v', w_b),
                        ('c_uv', 2 * w_c), ('small', LANES)]:
        offs[name] = (pos, pos + width)
        pos += width

    xp = x_prompt.reshape(b_p * seq, d_model)
    xs = x_sample.reshape(b_s * t_s, d_model)
    cache_kt = cache_fox_k.transpose(0, 1, 3, 4, 2).reshape(depth, n_phys, w_b, page)
    cache_vt = cache_fox_v.transpose(0, 1, 3, 4, 2).reshape(depth, n_phys, w_b, page)
    suffix = _page_suffix(cache_fox_logf.transpose(0, 3, 1, 2), math.gcd(n_phys, 512))

    tri = jnp.tril(jnp.ones((chunk_len, chunk_len), bool))
    tri_s = jnp.tril(jnp.ones((t_s, t_s), bool))
    n_rep = chunk_len // t_s
    assert (b_s * t_s) % chunk_len == 0 and chunk_len % t_s == 0

    prompt_rows, sample_rows = [], []
    w_in_t = w_in.transpose(2, 0, 1)
    kt_all = vt_all = None
    logft_rows = []
    for l in range(depth):
        w = w_in_t[:, l, :]

        def zrows(n):
            return jnp.zeros((n, d_model), F32)

        small_w = jnp.concatenate([
            w[i_beta:i_beta + gh], zrows(SM_G - SM_BETA - gh),
            w[i_alpha:i_alpha + gh], zrows(SM_F - SM_G - gh),
            w[i_bf:i_bf + fh], zrows(LANES - SM_F - fh)], axis=0)
        w_r = jnp.concatenate([w[i_aqkv:i_aqkv + 3 * w_a], w[i_az:i_az + w_a],
                               w[i_bqkv:i_bqkv + 3 * w_b], w[i_cuv:i_cuv + 2 * w_c], small_w],
                              axis=0).astype(BF16)
        par = jnp.concatenate([
            _lanes_block([(SM_G, gdn_dt_bias[l][None, :]), (SM_F, fox_f_bias[l][None, :])]),
            _lanes_block([(SM_G, gdn_a_log[l][None, :])]),
            jnp.zeros((SUBLANES - 2, LANES), F32)], axis=0)
        nm = norm_mix[l][None, :]
        gnw = jnp.tile(gdn_norm_w[l], gh)[None, :]
        fnw = jnp.tile(fox_out_norm[l], fh)[None, :]
        cvw = jnp.tile(chunk_v_norm[l], cg)[None, :]
        wo = w_out[l].astype(BF16)
        wg, wu, wd = ffn_w_gate[l].astype(BF16), ffn_w_up[l].astype(BF16), ffn_w_down[l].astype(BF16)
        nf = norm_ffn[l][None, :]
        nfin = norm_final[None, :]
        last = l == depth - 1

        aqkv, az, bq, kt_all, vt_all, cuv, small, cumt, logft = _inproj(
            xp, nm, w_r, par, offs, seq, 512, kv_stack=(l, depth, kt_all, vt_all))
        logft_rows.append(logft)
        o_a, s_new = _gdn(aqkv.reshape(b_p, seq, 3 * w_a), az.reshape(b_p, seq, w_a),
                          small.reshape(b_p, seq, LANES), jnp.zeros((b_p, SUBLANES, 3 * w_a), F32),
                          jnp.zeros((b_p, gh, HEAD_DIM, HEAD_DIM), F32), gdn_conv_w[l], gnw, seq, 512, 1)
        o_b = _fox_prompt(l, bq.reshape(b_p, seq, w_b), kt_all, vt_all, cumt, fnw, 512)
        wm = jnp.where(tri, chunk_ws[l], 0.0).astype(BF16)
        bse = jnp.repeat(chunk_bs[l].T, HEAD_DIM, axis=1)
        n_s = b_s * t_s
        aqkv_p, cuv_p = aqkv, cuv
        aqkv, az, bq, bk, bv, cuv, small, cumt, _ = _inproj(xs, nm, w_r, par, offs, t_s, n_s)
        cn8 = _pad_axis(cumt.reshape(SUBLANES, b_s, t_s).transpose(1, 0, 2), 2, 0, SUBLANES - t_s)
        decode_operands = (l, page_table, bq.reshape(b_s, t_s, w_b),
                           _pad_axis(bk.reshape(b_s, t_s, w_b), 1, 0, SUBLANES - t_s),
                           _pad_axis(bv.reshape(b_s, t_s, w_b), 1, 0, SUBLANES - t_s),
                           cn8, fnw, cache_kt, cache_vt, suffix, t_s)
        xp, tail, o_b = _mix_ffn(xp, o_a.reshape(b_p * seq, w_a), o_b.reshape(b_p * seq, w_b), cuv_p, cvw, wm,
                                    bse, wo, nf, wg, wu, ffn_conv_w[l], wd,
                                    jnp.zeros((b_p, SUBLANES, d_ff), F32), nfin, seq, 512, last,
                                    decode_operands=decode_operands)
        prompt_rows.append((None, None, None, s_new,
                            aqkv_p.reshape(b_p, seq, 3 * w_a)[:, seq - (conv_k - 1):, :],
                            tail[:, SUBLANES - (ffn_k - 1):, :]))

        pad_t = GDN_CHUNK - t_s
        o_a, s_new = _gdn(_pad_axis(aqkv.reshape(b_s, t_s, 3 * w_a), 1, 0, pad_t),
                          _pad_axis(az.reshape(b_s, t_s, w_a), 1, 0, pad_t),
                          _pad_axis(small.reshape(b_s, t_s, LANES), 1, 0, pad_t),
                          _pad_axis(state_gdn_conv[l], 1, SUBLANES - (conv_k - 1), 0),
                          state_gdn[l], gdn_conv_w[l], gnw, t_s, GDN_CHUNK, math.gcd(b_s, 4))
        o_a = o_a[:, :t_s, :]
        eye_rep = jnp.eye(n_rep, dtype=F32)
        wm_s = jnp.stack([jnp.kron(eye_rep, jnp.where(tri_s, chunk_ws[l, g, :t_s, :t_s], 0.0))
                          for g in range(cg)]).astype(BF16)
        bse_s = jnp.repeat(jnp.tile(chunk_bs[l, :, :t_s], (1, n_rep)).T, HEAD_DIM, axis=1)
        buf = state_ffn_conv[l]
        zrow = jnp.zeros((b_s, 1, d_ff), F32)
        pm1 = jnp.concatenate([buf[:, 1:2]] + [zrow] * (t_s - 1), axis=1).reshape(n_s, d_ff)
        pm2 = jnp.concatenate([buf[:, 0:1], buf[:, 1:2]] + [zrow] * (t_s - 2), axis=1).reshape(n_s, d_ff)
        xs, tail, cvn = _mix_ffn(xs, o_a.reshape(n_s, w_a), o_b.reshape(n_s, w_b), cuv, cvw, wm_s, bse_s,
                                 wo, nf, wg, wu, ffn_conv_w[l], wd, jnp.stack([pm1, pm2]), nfin,
                                 t_s, chunk_len, last)
        sample_rows.append((bk.reshape(b_s, t_s, fh, HEAD_DIM), bv.reshape(b_s, t_s, fh, HEAD_DIM),
                            small[:, SM_F:SM_F + fh].reshape(b_s, t_s, fh), s_new,
                            aqkv.reshape(b_s, t_s, 3 * w_a)[:, t_s - (conv_k - 1):, :],
                            tail.reshape(b_s, t_s, d_ff)[:, t_s - (ffn_k - 1):, :],
                            cvn.reshape(b_s, t_s, w_c)))

    def stack(rows_, i):
        return jnp.stack([r[i] for r in rows_])

    y_prompt = xp.reshape(b_p, seq, d_model)
    y_sample = xs.reshape(b_s, t_s, d_model)
    fox_k_prompt = kt_all.reshape(depth, b_p, fh, HEAD_DIM, seq).transpose(0, 1, 4, 2, 3)
    fox_v_prompt = vt_all.reshape(depth, b_p, fh, HEAD_DIM, seq).transpose(0, 1, 4, 2, 3)
    fox_logf_prompt = jnp.stack(logft_rows)[:, :fh, :].reshape(depth, fh, b_p, seq).transpose(0, 2, 3, 1)
    return (y_prompt, y_sample,
            fox_k_prompt, fox_v_prompt, fox_logf_prompt, stack(prompt_rows, 3),
            stack(prompt_rows, 4), stack(prompt_rows, 5),
            stack(sample_rows, 0), stack(sample_rows, 1), stack(sample_rows, 2), stack(sample_rows, 3),
            stack(sample_rows, 4), stack(sample_rows, 5), stack(sample_rows, 6))
```

```python
import functools
import math

import jax
import jax.numpy as jnp
from jax import lax
from jax.experimental import pallas as pl
from jax.experimental.pallas import tpu as pltpu

F32 = jnp.float32
BF16 = jnp.bfloat16
EPS = 1e-6
LANES = 128
SUBLANES = 8
HEAD_DIM = 64
PAIR = 2 * HEAD_DIM
GDN_CHUNK = 64
FOX_PAGES_PER_STEP = 16
FOX_PAGES_PER_SUBBLOCK = 4
VMEM_LIMIT = 62 * 1024 * 1024
HIGHEST = lax.Precision.HIGHEST
NEG_INF = -1e30

SM_BETA, SM_G, SM_F = 0, 8, 16


def _cparams(sem):
    return pltpu.CompilerParams(dimension_semantics=sem, vmem_limit_bytes=VMEM_LIMIT)


def _dot(a, b):
    return jnp.dot(a.astype(BF16), b.astype(BF16), preferred_element_type=F32)


def _dot_nt(a, b):
    return lax.dot_general(a.astype(BF16), b.astype(BF16), (((1,), (1,)), ((), ())),
                           preferred_element_type=F32)


def _dot_tn(a, b):
    return lax.dot_general(a.astype(BF16), b.astype(BF16), (((0,), (0,)), ((), ())),
                           preferred_element_type=F32)


def _dot2(a, b01):
    hi = a.astype(BF16)
    lo = (a - hi.astype(F32)).astype(BF16)
    b = b01.astype(BF16)
    return (jnp.dot(hi, b, preferred_element_type=F32) + jnp.dot(lo, b, preferred_element_type=F32))


def _split3(a):
    a1 = a.astype(BF16)
    r1 = a - a1.astype(F32)
    a2 = r1.astype(BF16)
    a3 = (r1 - a2.astype(F32)).astype(BF16)
    return a1, a2, a3


def _dot01_left(m01, a):
    m = m01.astype(BF16)
    t1, t2, t3 = _split3(a)
    return (jnp.dot(m, t1, preferred_element_type=F32) + jnp.dot(m, t2, preferred_element_type=F32)
            + jnp.dot(m, t3, preferred_element_type=F32))


def _dot01_right(a, m01):
    m = m01.astype(BF16)
    t1, t2, t3 = _split3(a)
    return (jnp.dot(t1, m, preferred_element_type=F32) + jnp.dot(t2, m, preferred_element_type=F32)
            + jnp.dot(t3, m, preferred_element_type=F32))


def _iota(shape, dim):
    return lax.broadcasted_iota(jnp.int32, shape, dim)


def _head_ones(n):
    return (_iota((n, n), 0) // HEAD_DIM == _iota((n, n), 1) // HEAD_DIM).astype(F32)


def _silu(x):
    return x * jax.nn.sigmoid(x)


def _inproj_kernel(x_ref, nw_ref, wt_ref, par_ref, *refs, offs, seq_tiles, seq_len, kv_transposed, kv_slot,
                   n_alias):
    (aqkv_ref, az_ref, bq_ref, bk_ref, bv_ref, cuv_ref, small_ref, cumt_ref, logft_ref,
     carry_ref) = refs[n_alias:]
    i = pl.program_id(0)
    tm = x_ref.shape[0]
    x = x_ref[...]
    h = (x * lax.rsqrt(jnp.mean(x * x, axis=-1, keepdims=True) + EPS) * nw_ref[...]).astype(BF16)

    y = lax.dot_general(h, wt_ref[...], (((1,), (1,)), ((), ())), preferred_element_type=F32)

    def mm(name):
        lo, hi = offs[name]
        return y[:, lo:hi]

    aqkv_ref[...] = mm('a_qkv')
    az_ref[...] = mm('a_z')
    bq_ref[...] = mm('b_q')
    if kv_transposed:
        bk_ref[kv_slot, 0] = mm('b_k').T
        bv_ref[kv_slot, 0] = mm('b_v').T
        for d in range(bk_ref.shape[0]):
            if d != kv_slot:
                bk_ref[d, 0] = jnp.zeros(bk_ref.shape[2:], F32)
                bv_ref[d, 0] = jnp.zeros(bv_ref.shape[2:], F32)
    else:
        bk_ref[...] = mm('b_k')
        bv_ref[...] = mm('b_v')
    cuv_ref[...] = mm('c_uv')

    z = mm('small') + par_ref[0:1, :]
    lane = _iota(z.shape, 1)
    t = jnp.log1p(jnp.exp(-jnp.abs(z)))
    softplus = jnp.maximum(z, 0.0) + t
    log_sig = jnp.minimum(z, 0.0) - t
    sm = jnp.where(lane < SM_G, jax.nn.sigmoid(z),
                   jnp.where(lane < SM_F, -jnp.exp(par_ref[1:2, :]) * softplus, log_sig))
    small_ref[...] = sm

    lb = min(seq_len, tm)
    row, col = _iota((tm, tm), 0), _iota((tm, tm), 1)
    tri = jnp.where((col <= row) & (row // lb == col // lb), 1.0, 0.0)
    cum = _dot01_left(tri, sm)
    if seq_tiles > 1:
        @pl.when(i % seq_tiles == 0)
        def _():
            carry_ref[...] = jnp.zeros_like(carry_ref)
        cum = cum + carry_ref[0:1, :]
        carry_ref[0:1, :] = cum[tm - 1:tm, :]
    cumt_ref[...] = cum.T[SM_F:SM_F + SUBLANES, :]
    logft_ref[...] = sm.T[SM_F:SM_F + SUBLANES, :]


def _inproj(x, nw, wt, par, offs, seq_len, tm, kv_stack=None):
    n, d = x.shape
    assert n % tm == 0 and (seq_len % tm == 0 or tm % seq_len == 0)
    seq_tiles = max(seq_len // tm, 1)
    widths = {k: hi - lo for k, (lo, hi) in offs.items()}
    names = ['a_qkv', 'a_z', 'b_q', 'b_k', 'b_v', 'c_uv', 'small']
    out_shape = [jax.ShapeDtypeStruct((n, widths[k]), F32) for k in names]
    out_specs = [pl.BlockSpec((tm, widths[k]), lambda i: (i, 0)) for k in names]
    out_shape += [jax.ShapeDtypeStruct((SUBLANES, n), F32)] * 2
    out_specs += [pl.BlockSpec((SUBLANES, tm), lambda i: (0, i))] * 2
    args = [x, nw, wt, par]
    in_specs = [pl.BlockSpec((tm, d), lambda i: (i, 0)),
                pl.BlockSpec((1, d), lambda i: (0, 0)),
                pl.BlockSpec(wt.shape, lambda i: (0, 0)),
                pl.BlockSpec(par.shape, lambda i: (0, 0))]
    aliases = {}
    kv_slot = 0
    if kv_stack is not None:
        layer, depth, kt_prev, vt_prev = kv_stack
        assert seq_len % tm == 0
        w_b = widths['b_k']
        if kt_prev is None:
            kv_slot, blk, first = layer, depth, 0
        else:
            kv_slot, blk, first = 0, 1, layer
            args += [kt_prev, vt_prev]
            in_specs += [pl.BlockSpec(memory_space=pl.ANY)] * 2
            aliases = {4: 3, 5: 4}
        for pos in (3, 4):
            out_shape[pos] = jax.ShapeDtypeStruct((depth, n // seq_len, w_b, seq_len), F32)
            out_specs[pos] = pl.BlockSpec((blk, 1, w_b, tm), lambda i: (first, i // seq_tiles, 0, i % seq_tiles))
    return pl.pallas_call(
        functools.partial(_inproj_kernel, offs=offs, seq_tiles=seq_tiles, seq_len=seq_len,
                          kv_transposed=kv_stack is not None, kv_slot=kv_slot, n_alias=len(aliases)),
        grid=(n // tm,),
        in_specs=in_specs,
        out_specs=out_specs,
        out_shape=out_shape,
        input_output_aliases=aliases,
        scratch_shapes=[pltpu.VMEM((SUBLANES, LANES), F32)],
        compiler_params=_cparams(("arbitrary",)),
        name="inproj",
    )(*args)


def _unit_lower_inverses(lmats):
    n = lmats[0].shape[0]
    row, col = _iota((n, n), 0), _iota((n, n), 1)
    eye = (row == col).astype(F32)
    base = SUBLANES
    in_base = row // base == col // base
    lds = [jnp.where(in_base, lm, 0.0) for lm in lmats]
    xs = [eye - ld for ld in lds]
    ps = [_dot(ld, ld) for ld in lds]
    xs = [x + _dot(x, p) for x, p in zip(xs, ps)]
    ps = [_dot(p, p) for p in ps]
    xs = [x + _dot(x, p) for x, p in zip(xs, ps)]
    s = base
    while s < HEAD_DIM:
        off_mask = (row // (2 * s) == col // (2 * s)) & (row // s != col // s)
        ts_ = [_dot(x, jnp.where(off_mask, lm, 0.0)) for x, lm in zip(xs, lmats)]
        xs = [x - _dot(t_, x) for x, t_ in zip(xs, ts_)]
        s *= 2
    return xs


def _gdn_kernel(aqkv_ref, az_ref, small_ref, buf_ref, s0_ref, cw_ref, nw_ref,
                o_ref, sout_ref,
                s_ref, prev_ref, qs_ref, qg_ref, k_ref, kb_ref, kbg_ref, vb_ref, gc_ref, oacc_ref,
                conv_ref, uw_ref, at_ref, *, n_pairs, valid_len, group):
    t = pl.program_id(1)
    nt = pl.num_programs(1)
    nb, ts, w3 = aqkv_ref.shape
    w_a = n_pairs * PAIR
    c = GDN_CHUNK
    rows_all = nb * ts
    cps = ts // c

    @pl.when(t == 0)
    def _():
        for bi in range(nb):
            for j in range(n_pairs):
                s_a = s0_ref[bi, 2 * j]
                s_b = s0_ref[bi, 2 * j + 1]
                top = jnp.concatenate([s_a, jnp.zeros_like(s_a)], axis=1)
                bot = jnp.concatenate([jnp.zeros_like(s_b), s_b], axis=1)
                s_ref[bi, j] = jnp.concatenate([top, bot], axis=0)
        prev_ref[...] = buf_ref[...]

    x = aqkv_ref[...].reshape(rows_all, w3)
    kw = cw_ref.shape[0]
    conv = cw_ref[kw - 1:kw, :] * x
    for i in range(1, kw):
        conv = conv + cw_ref[kw - 1 - i:kw - i, :] * pltpu.roll(x, i, axis=0)
    conv_ref[...] = conv
    for bi in range(nb):
        head = jnp.concatenate([prev_ref[bi], x[bi * ts:bi * ts + SUBLANES]], axis=0)
        fixed = cw_ref[0:1, :] * head[SUBLANES - kw + 1:2 * SUBLANES - kw + 1]
        for i in range(1, kw):
            fixed = fixed + cw_ref[i:i + 1, :] * head[SUBLANES - kw + 1 + i:2 * SUBLANES - kw + 1 + i]
        conv_ref[bi * ts:bi * ts + SUBLANES, :] = fixed
        prev_ref[bi] = x[(bi + 1) * ts - SUBLANES:(bi + 1) * ts]
    a = _silu(conv_ref[...])
    rowid = _iota((rows_all, 1), 0) % ts + t * ts
    valid = (rowid < valid_len).astype(F32)
    ones_h = _head_ones(w_a)
    q = a[:, 0:w_a]
    k = a[:, w_a:2 * w_a]
    v = a[:, 2 * w_a:3 * w_a] * valid
    qn = q * lax.rsqrt(_dot2(q * q, ones_h) + EPS) * valid
    kn = k * lax.rsqrt(_dot2(k * k, ones_h) + EPS) * valid

    sm = small_ref[...].reshape(rows_all, LANES) * valid
    row, col = _iota((rows_all, rows_all), 0), _iota((rows_all, rows_all), 1)
    tri = jnp.where((col <= row) & (row // c == col // c), 1.0, 0.0)
    gcum = _dot01_left(tri, sm)
    gc_ref[...] = gcum
    lane, hrow = _iota((LANES, w_a), 1), _iota((LANES, w_a), 0)
    exp_b = (hrow == lane // HEAD_DIM + SM_BETA).astype(F32)
    exp_g = (hrow == lane // HEAD_DIM + SM_G).astype(F32)
    beta_e = _dot2(sm, exp_b)
    gcum_e = _dot01_right(gcum, exp_g)
    eg = jnp.exp(gcum_e)
    scale = HEAD_DIM ** -0.5
    qs_ref[...] = qn * scale
    qg_ref[...] = qn * (scale * eg)
    k_ref[...] = kn
    kb = kn * beta_e
    kb_ref[...] = kb
    kbg_ref[...] = kb * eg
    vb_ref[...] = v * beta_e

    r2, c2 = _iota((PAIR, PAIR), 0), _iota((PAIR, PAIR), 1)
    same_head = (r2 // HEAD_DIM) == (c2 // HEAD_DIM)
    strict = same_head & (c2 < r2)
    rw, cw_ = _iota((c, PAIR), 0), _iota((c, PAIR), 1)
    incl_wide = (cw_ % HEAD_DIM) <= rw
    lane_lo = _iota((c, PAIR), 1) < HEAD_DIM

    def stack(m):
        return jnp.concatenate([jnp.where(lane_lo, m, 0.0), jnp.where(lane_lo, 0.0, m)], axis=0)

    def gate_cols(g_c, j):
        return g_c[:, SM_G + 2 * j:SM_G + 2 * j + 1], g_c[:, SM_G + 2 * j + 1:SM_G + 2 * j + 2]

    def prepare(gi, carry):
        items = []
        for i in range(group):
            ci = gi * group + i
            r0 = pl.multiple_of(ci * c, c)
            g_c = gc_ref[pl.ds(r0, c), :]
            g_t = g_c.T
            for j in range(n_pairs):
                ls = slice(j * PAIR, (j + 1) * PAIR)
                ga_col, gb_col = gate_cols(g_c, j)
                g_row = jnp.concatenate([g_t[SM_G + 2 * j:SM_G + 2 * j + 1, :],
                                         g_t[SM_G + 2 * j + 1:SM_G + 2 * j + 2, :]], axis=1)
                g_col2 = jnp.concatenate([ga_col, gb_col], axis=0)
                g_wide = jnp.where(lane_lo, ga_col, gb_col)
                items.append(dict(
                    ci=ci, j=j,
                    k2=stack(k_ref[pl.ds(r0, c), ls]),
                    kb2=stack(kb_ref[pl.ds(r0, c), ls]),
                    rhs=jnp.concatenate([stack(vb_ref[pl.ds(r0, c), ls]), stack(kbg_ref[pl.ds(r0, c), ls])],
                                        axis=1),
                    qs=qs_ref[pl.ds(r0, c), ls],
                    decay2=jnp.exp(jnp.minimum(g_col2 - g_row, 0.0)),
                    decay_w=jnp.exp(jnp.minimum(g_wide - g_row, 0.0))))
        kk = [_dot_nt(it['kb2'], it['k2']) for it in items]
        qk = [_dot_nt(it['qs'], it['k2']) for it in items]
        lmats = [jnp.where(strict, m * it['decay2'], 0.0) for m, it in zip(kk, items)]
        for m, it in zip(qk, items):
            at_ref[it['ci'], it['j']] = jnp.where(incl_wide, m * it['decay_w'], 0.0)
        tinvs = _unit_lower_inverses(lmats)
        sols = [_dot(tinv, it['rhs']) for tinv, it in zip(tinvs, items)]
        for sol, it in zip(sols, items):
            uw_ref[it['ci'], it['j']] = sol
        return carry

    lax.fori_loop(0, rows_all // c // group, prepare, 0)

    def recur(n, carry):
        items = []
        for bi in range(nb):
            ci = bi * cps + n
            r0 = pl.multiple_of(ci * c, c)
            g_c = gc_ref[pl.ds(r0, c), :]
            for j in range(n_pairs):
                ls = slice(j * PAIR, (j + 1) * PAIR)
                ga_col, gb_col = gate_cols(g_c, j)
                g_wide = jnp.where(lane_lo, ga_col, gb_col)
                g_last = jnp.where(lane_lo, ga_col[c - 1:c, :], gb_col[c - 1:c, :])
                g_last_col = jnp.where(_iota((PAIR, 1), 0) < HEAD_DIM, ga_col[c - 1:c, :], gb_col[c - 1:c, :])
                items.append(dict(
                    bi=bi, r0=r0, ls=ls, j=j,
                    u2=uw_ref[ci, j, :, 0:PAIR], w2=uw_ref[ci, j, :, PAIR:2 * PAIR], attn=at_ref[ci, j],
                    s2=s_ref[bi, j], qg=qg_ref[pl.ds(r0, c), ls],
                    kdec=k_ref[pl.ds(r0, c), ls] * jnp.exp(g_last - g_wide),
                    s_decay=jnp.exp(g_last_col)))
        ws = [_dot(it['w2'], it['s2']) for it in items]
        qss = [_dot(it['qg'], it['s2']) for it in items]
        vnew2 = [it['u2'] - w for it, w in zip(items, ws)]
        avs = [_dot(it['attn'], v2) for it, v2 in zip(items, vnew2)]
        kvs = [_dot_tn(it['kdec'], v2[:c, :] + v2[c:, :]) for it, v2 in zip(items, vnew2)]
        for it, q_s, av, kv in zip(items, qss, avs, kvs):
            oacc_ref[pl.ds(it['r0'], c), it['ls']] = q_s + av
            s_ref[it['bi'], it['j']] = it['s2'] * it['s_decay'] + jnp.where(same_head, kv, 0.0)
        return carry

    for n in range(cps):
        recur(n, 0)

    o = oacc_ref[...]
    on = o * lax.rsqrt(_dot2(o * o, ones_h) * (1.0 / HEAD_DIM) + EPS) * nw_ref[...]
    o_ref[...] = (on * _silu(az_ref[...].reshape(rows_all, w_a))).reshape(nb, ts, w_a)

    @pl.when(t == nt - 1)
    def _():
        for bi in range(nb):
            for j in range(n_pairs):
                s2 = s_ref[bi, j]
                sout_ref[bi, 2 * j] = s2[:HEAD_DIM, :HEAD_DIM]
                sout_ref[bi, 2 * j + 1] = s2[HEAD_DIM:, HEAD_DIM:]


def _gdn(aqkv, az, small, bufpad, s0, conv_w, norm_w_tiled, valid_len, ts, nb):
    b, l, w3 = aqkv.shape
    w_a = w3 // 3
    n_pairs = w_a // PAIR
    heads = s0.shape[1]
    n_chunks = nb * ts // GDN_CHUNK
    group = math.gcd(n_chunks, 8)
    assert l % ts == 0 and ts % GDN_CHUNK == 0 and heads == 2 * n_pairs and b % nb == 0
    rows_all = nb * ts
    scratch = [pltpu.VMEM((nb, n_pairs, PAIR, PAIR), F32), pltpu.VMEM((nb, SUBLANES, w3), F32)]
    scratch += [pltpu.VMEM((rows_all, w_a), F32) for _ in range(6)]
    scratch += [pltpu.VMEM((rows_all, LANES), F32), pltpu.VMEM((rows_all, w_a), F32),
                pltpu.VMEM((rows_all, w3), F32),
                pltpu.VMEM((n_chunks, n_pairs, PAIR, 2 * PAIR), F32),
                pltpu.VMEM((n_chunks, n_pairs, GDN_CHUNK, PAIR), F32)]
    return pl.pallas_call(
        functools.partial(_gdn_kernel, n_pairs=n_pairs, valid_len=valid_len, group=group),
        grid=(b // nb, l // ts),
        in_specs=[pl.BlockSpec((nb, ts, w3), lambda i, t: (i, t, 0)),
                  pl.BlockSpec((nb, ts, w_a), lambda i, t: (i, t, 0)),
                  pl.BlockSpec((nb, ts, LANES), lambda i, t: (i, t, 0)),
                  pl.BlockSpec((nb, SUBLANES, w3), lambda i, t: (i, 0, 0)),
                  pl.BlockSpec((nb, heads, HEAD_DIM, HEAD_DIM), lambda i, t: (i, 0, 0, 0)),
                  pl.BlockSpec(conv_w.shape, lambda i, t: (0, 0)),
                  pl.BlockSpec(norm_w_tiled.shape, lambda i, t: (0, 0))],
        out_specs=[pl.BlockSpec((nb, ts, w_a), lambda i, t: (i, t, 0)),
                   pl.BlockSpec((nb, heads, HEAD_DIM, HEAD_DIM), lambda i, t: (i, 0, 0, 0))],
        out_shape=[jax.ShapeDtypeStruct((b, l, w_a), F32),
                   jax.ShapeDtypeStruct((b, heads, HEAD_DIM, HEAD_DIM), F32)],
        scratch_shapes=scratch,
        compiler_params=_cparams(("arbitrary", "arbitrary")),
        name="gdn",
    )(aqkv, az, small, bufpad, s0, conv_w, norm_w_tiled)


def _fox_prompt_kernel(q_ref, kt_ref, vt_ref, cumt_ref, nw_ref, o_ref, *, n_pairs):
    i = pl.program_id(1)
    tq = q_ref.shape[1]
    tk = tq
    scale = HEAD_DIM ** -0.5
    lane_lo = _iota((tq, PAIR), 1) < HEAD_DIM
    causal = _iota((tq, tk), 1) <= _iota((tq, tk), 0)
    q0 = pl.multiple_of(i * tq, tq)
    pairs = []
    for j in range(n_pairs):
        ls = slice(j * PAIR, (j + 1) * PAIR)
        q_p = q_ref[0, :, ls] * scale
        heads = (2 * j, 2 * j + 1)
        q_hs = [jnp.where(lane_lo, q_p, 0.0).astype(BF16), jnp.where(lane_lo, 0.0, q_p).astype(BF16)]
        c_refs = [cumt_ref[h:h + 1, pl.ds(q0, LANES)][:, 0:1] for h in heads]

        def block(k0, carry, masked, q_hs=q_hs, c_refs=c_refs, heads=heads, ls=ls):
            k_blk = kt_ref[0, 0, ls, pl.ds(k0, tk)].astype(BF16)
            v_blk = vt_ref[0, 0, ls, pl.ds(k0, tk)].astype(BF16)
            ss = [jnp.dot(q_h, k_blk, preferred_element_type=F32) for q_h in q_hs]
            ss = [s + (c_ref - cumt_ref[h:h + 1, pl.ds(k0, tk)]) for s, c_ref, h in zip(ss, c_refs, heads)]
            if masked:
                ss = [jnp.where(causal, s, NEG_INF) for s in ss]
            m_news = [jnp.maximum(st[0], jnp.max(s, axis=-1, keepdims=True)) for st, s in zip(carry, ss)]
            ps = [jnp.exp(s - m_new) for s, m_new in zip(ss, m_news)]
            alphas = [jnp.exp(st[0] - m_new) for st, m_new in zip(carry, m_news)]
            pvs = [lax.dot_general(p.astype(BF16), v_blk, (((1,), (1,)), ((), ())), preferred_element_type=F32)
                   for p in ps]
            return tuple((m_new, a * st[1] + jnp.sum(p, axis=-1, keepdims=True), a * st[2] + pv)
                         for st, m_new, a, p, pv in zip(carry, m_news, alphas, ps, pvs))

        init = tuple((jnp.full((tq, 1), NEG_INF, F32), jnp.zeros((tq, 1), F32), jnp.zeros((tq, PAIR), F32))
                     for _ in heads)
        carry = lax.fori_loop(
            0, i, lambda kb, c_, block=block: block(pl.multiple_of(kb * tk, tk), c_, False), init)
        outs = [acc / l for _, l, acc in block(q0, carry, True)]
        pairs.append(jnp.where(lane_lo, outs[0], outs[1]))
    o = jnp.concatenate(pairs, axis=1)
    ones_h = _head_ones(o.shape[1])
    o_ref[0] = o * lax.rsqrt(_dot2(o * o, ones_h) * (1.0 / HEAD_DIM) + EPS) * nw_ref[...]


def _fox_prompt(layer, q, kt, vt, cumt, norm_w_tiled, tq):
    b, l, w_b = q.shape
    assert l % tq == 0 and tq % LANES == 0
    return pl.pallas_call(
        functools.partial(_fox_prompt_kernel, n_pairs=w_b // PAIR),
        grid=(b, l // tq),
        in_specs=[pl.BlockSpec((1, tq, w_b), lambda i, t: (i, t, 0)),
                  pl.BlockSpec((1, 1, w_b, l), lambda i, t: (layer, i, 0, 0)),
                  pl.BlockSpec((1, 1, w_b, l), lambda i, t: (layer, i, 0, 0)),
                  pl.BlockSpec((SUBLANES, l), lambda i, t: (0, i)),
                  pl.BlockSpec(norm_w_tiled.shape, lambda i, t: (0, 0))],
        out_specs=pl.BlockSpec((1, tq, w_b), lambda i, t: (i, t, 0)),
        out_shape=jax.ShapeDtypeStruct((b, l, w_b), F32),
        compiler_params=_cparams(("arbitrary", "arbitrary")),
        name="fox_prompt",
    )(q, kt, vt, cumt, norm_w_tiled)


def _page_suffix_kernel(x_ref, m_ref, o_ref):
    n_heads = x_ref.shape[1]
    for h in range(SUBLANES):
        if h < n_heads:
            o_ref[0, :, h, :] = _dot01_right(x_ref[0, h], m_ref[...])
        else:
            o_ref[0, :, h, :] = jnp.zeros((o_ref.shape[1], o_ref.shape[3]), F32)


def _page_suffix(logf_t, tp):
    depth, n_heads, n_phys, page = logf_t.shape
    assert page == LANES and n_phys % tp == 0
    idx = jnp.arange(page)
    m = (idx[:, None] >= idx[None, :]).astype(BF16)
    return pl.pallas_call(
        _page_suffix_kernel,
        grid=(depth, n_phys // tp),
        in_specs=[pl.BlockSpec((1, n_heads, tp, page), lambda d, i: (d, 0, i, 0)),
                  pl.BlockSpec(m.shape, lambda d, i: (0, 0))],
        out_specs=pl.BlockSpec((1, tp, SUBLANES, page), lambda d, i: (d, i, 0, 0)),
        out_shape=jax.ShapeDtypeStruct((depth, n_phys, SUBLANES, page), F32),
        compiler_params=_cparams(("arbitrary", "arbitrary")),
        name="page_suffix",
    )(logf_t, m)


def _run_if(cond, fn):
    if isinstance(cond, bool):
        if cond:
            fn()
    else:
        pl.when(cond)(fn)


def _decode_round_scores(step, n_steps, refs, *, layer, n_new, pps, sub_pages, nj, j_static, maybe_first,
                         maybe_last):
    (pt_ref, q_ref, kn_ref, vn_ref, cn_ref, nw_ref, kt_hbm, vt_hbm, sfx_hbm, o_ref,
     kbuf, vbuf, sbuf, ksem, vsem, ssem, qbd_ref, m_ref, l_ref, acc_ref, carry_ref) = refs
    bb = step // nj
    j = step % nj if j_static is None else j_static
    n_tok = q_ref.shape[1]
    w_b = q_ref.shape[2]
    rows = n_tok * SUBLANES
    page = kbuf.shape[3]
    n_pages = nj * pps
    scale = HEAD_DIM ** -0.5
    slot = step % 2

    def page_copies(step_x, slot_x, i):
        pg = pt_ref[step_x // nj, n_pages - 1 - ((step_x % nj) * pps + i)]
        return (pltpu.make_async_copy(kt_hbm.at[layer, pg], kbuf.at[slot_x, i], ksem.at[slot_x]),
                pltpu.make_async_copy(vt_hbm.at[layer, pg], vbuf.at[slot_x, i], vsem.at[slot_x]),
                pltpu.make_async_copy(sfx_hbm.at[layer, pg], sbuf.at[slot_x, i], ssem.at[slot_x]))

    def request(step_x, slot_x):
        for i in range(pps):
            for cp in page_copies(step_x, slot_x, i):
                cp.start()

    if maybe_first:
        def prime():
            request(step, slot)
            if n_steps > 1:
                request(step + 1, 1 - slot)
        _run_if(step == 0, prime)

    for i in range(pps):
        for cp in page_copies(step, slot, i):
            cp.wait()

    def start_sequence():
        q = q_ref[bb] * scale
        headmask = (_iota((SUBLANES, w_b), 1) // HEAD_DIM == _iota((SUBLANES, w_b), 0)).astype(F32)
        qbd = jnp.concatenate([q[t:t + 1, :] * headmask for t in range(n_tok)], axis=0)
        qbd_ref[...] = qbd.astype(BF16)
        kn = kn_ref[bb]
        s = _dot_nt(qbd, kn)
        cn = cn_ref[bb]
        s = s - jnp.concatenate([cn] * n_tok, axis=0)
        kidx = _iota((rows, SUBLANES), 1)
        tidx = _iota((rows, SUBLANES), 0) // SUBLANES
        s = jnp.where((kidx <= tidx) & (kidx < n_new), s, NEG_INF)
        m = jnp.max(s, axis=-1, keepdims=True)
        p = jnp.exp(s - m)
        m_ref[...] = m
        l_ref[...] = jnp.sum(p, axis=-1, keepdims=True)
        acc_ref[...] = _dot(p, vn_ref[bb])
        carry_ref[...] = jnp.zeros_like(carry_ref)

    _run_if(j == 0, start_sequence)

    lane = _iota((SUBLANES, page), 1)
    carry = carry_ref[...]
    biases = []
    for i in range(pps):
        incl = sbuf[slot, i]
        excl = jnp.where(lane < page - 1, pltpu.roll(incl, page - 1, axis=1), 0.0)
        biases.append(excl + carry)
        carry = carry + incl[:, 0:1]
    carry_ref[...] = carry
    qbd = qbd_ref[...]
    n_sub = pps // sub_pages
    subs = [range(u * sub_pages, (u + 1) * sub_pages) for u in range(n_sub)]
    kts = [jnp.concatenate([kbuf[slot, i].astype(BF16) for i in sb], axis=1) for sb in subs]
    ss = [jnp.dot(qbd, kt, preferred_element_type=F32) for kt in kts]
    ss = [s + jnp.concatenate([jnp.concatenate([biases[i] for i in sb], axis=1)] * n_tok, axis=0)
          for s, sb in zip(ss, subs)]
    ms = [jnp.max(s, axis=-1, keepdims=True) for s in ss]
    ps = [jnp.exp(s - m) for s, m in zip(ss, ms)]
    ls = [jnp.sum(p, axis=-1, keepdims=True) for p in ps]
    refill = functools.partial(_run_if, step + 2 < n_steps if maybe_last else True,
                               lambda: request(step + 2, slot))
    return dict(refs=refs, slot=slot, subs=subs, ms=ms, ps=ps, ls=ls, bb=bb, last_of_sequence=j == nj - 1,
                refill=refill)


def _decode_round_values(state):
    (_, q_ref, _, _, _, nw_ref, _, _, _, o_ref,
     _, vbuf, _, _, _, _, _, m_ref, l_ref, acc_ref, _) = state['refs']
    slot, subs, ms, ps, ls, bb = (state[k] for k in ('slot', 'subs', 'ms', 'ps', 'ls', 'bb'))
    n_tok = q_ref.shape[1]
    w_b = q_ref.shape[2]
    rows = n_tok * SUBLANES
    vts = [jnp.concatenate([vbuf[slot, i].astype(BF16) for i in sb], axis=1) for sb in subs]
    pvs = [lax.dot_general(p.astype(BF16), vt, (((1,), (1,)), ((), ())), preferred_element_type=F32)
           for p, vt in zip(ps, vts)]
    state['refill']()
    m_old = m_ref[...]
    m_new = m_old
    for m in ms:
        m_new = jnp.maximum(m_new, m)
    alpha = jnp.exp(m_old - m_new)
    l_new = alpha * l_ref[...]
    acc = alpha * acc_ref[...]
    for m, l_u, pv in zip(ms, ls, pvs):
        w_u = jnp.exp(m - m_new)
        l_new = l_new + w_u * l_u
        acc = acc + w_u * pv
    m_ref[...] = m_new
    l_ref[...] = l_new
    acc_ref[...] = acc

    def finish_sequence():
        o = acc_ref[...] / l_ref[...]
        diag = (_iota((rows, w_b), 1) // HEAD_DIM) == (_iota((rows, w_b), 0) % SUBLANES)
        o = jnp.where(diag, o, 0.0)
        group = (_iota((SUBLANES, rows), 1) // SUBLANES == _iota((SUBLANES, rows), 0)).astype(F32)
        o_tok = jnp.dot(group, o, precision=HIGHEST, preferred_element_type=F32)
        ones_h = _head_ones(w_b)
        ss = jnp.dot(o_tok * o_tok, ones_h, precision=HIGHEST, preferred_element_type=F32)
        on = o_tok * lax.rsqrt(ss * (1.0 / HEAD_DIM) + EPS) * nw_ref[...]
        o_ref[bb] = on[0:n_tok, :]

    _run_if(state['last_of_sequence'], finish_sequence)


N_DECODE_INPUTS = 9
N_DECODE_SCRATCH = 11


def _decode_plumbing(layer, page_table, q, k_new8, v_new8, cn8, norm_w_tiled, cache_kt, cache_vt, suffix, n_new,
                     n_hosts):
    b, n_tok, w_b = q.shape
    page = cache_kt.shape[3]
    n_pages = page_table.shape[1]
    pps = math.gcd(n_pages, FOX_PAGES_PER_STEP)
    sub_pages = math.gcd(pps, FOX_PAGES_PER_SUBBLOCK)
    nj = n_pages // pps
    assert (b * nj) % n_hosts == 0
    rows = n_tok * SUBLANES
    n_slots = 2

    def whole(a):
        nd = a.ndim
        return pl.BlockSpec(a.shape, lambda *_: (0,) * nd)

    args = [page_table, q, k_new8, v_new8, cn8, norm_w_tiled, cache_kt, cache_vt, suffix]
    in_specs = [pl.BlockSpec(memory_space=pltpu.SMEM), whole(q), whole(k_new8), whole(v_new8), whole(cn8),
                whole(norm_w_tiled), pl.BlockSpec(memory_space=pl.ANY), pl.BlockSpec(memory_space=pl.ANY),
                pl.BlockSpec(memory_space=pl.ANY)]
    assert len(args) == N_DECODE_INPUTS
    out_shape = jax.ShapeDtypeStruct((b, n_tok, w_b), F32)
    out_spec = pl.BlockSpec((b, n_tok, w_b), lambda *_: (0, 0, 0))
    scratch = [pltpu.VMEM((n_slots, pps, w_b, page), F32), pltpu.VMEM((n_slots, pps, w_b, page), F32),
               pltpu.VMEM((n_slots, pps, SUBLANES, page), F32),
               pltpu.SemaphoreType.DMA((n_slots,)), pltpu.SemaphoreType.DMA((n_slots,)),
               pltpu.SemaphoreType.DMA((n_slots,)),
               pltpu.VMEM((rows, w_b), BF16), pltpu.VMEM((rows, 1), F32), pltpu.VMEM((rows, 1), F32),
               pltpu.VMEM((rows, w_b), F32), pltpu.VMEM((SUBLANES, page), F32)]
    assert len(scratch) == N_DECODE_SCRATCH
    config = dict(layer=layer, n_new=n_new, pps=pps, sub_pages=sub_pages, nj=nj,
                  rounds_per_host=b * nj // n_hosts, n_steps=b * nj)
    return args, in_specs, out_shape, out_spec, scratch, config


def _rms(x, w):
    return x * lax.rsqrt(jnp.mean(x * x, axis=-1, keepdims=True) + EPS) * w


def _gelu_tanh(x):
    return x * (0.5 * (1.0 + jnp.tanh(math.sqrt(2.0 / math.pi) * (x + 0.044715 * (x * x * x)))))


def _mix_ffn_kernel(x_ref, oa_ref, ob_ref, cuv_ref, cvw_ref, wm_ref, bse_ref, wo_ref, nf_ref,
                    wg_ref, wu_ref, cw_ref, wd_ref, pm_ref, nfin_ref, *refs,
                    seq_len, dff_chunk, final_norm, multi_seq, decode):
    if decode is None:
        y_ref, tail_ref, cv_ref, carry_ref = refs
        decode_refs = None
    else:
        dec_in, rest = refs[:N_DECODE_INPUTS], refs[N_DECODE_INPUTS:]
        y_ref, tail_ref, dec_out, carry_ref = rest[:4]
        cv_ref = None
        decode_refs = tuple(dec_in) + (dec_out,) + tuple(rest[4:])
    t = pl.program_id(1)
    tm = x_ref.shape[0]
    w_a, w_b = oa_ref.shape[1], ob_ref.shape[1]
    w_c = cuv_ref.shape[1] // 2
    d_ff = wg_ref.shape[1]
    tail_rows = tail_ref.shape[1]

    gel = _gelu_tanh(cuv_ref[...])
    cu, cv = gel[:, :w_c], gel[:, w_c:]
    ones_h = _head_ones(w_c)
    cvn = cv * lax.rsqrt(_dot2(cv * cv, ones_h) * (1.0 / HEAD_DIM) + EPS) * cvw_ref[...]
    if cv_ref is not None:
        cv_ref[...] = cvn
    cvn_b = cvn.astype(BF16)
    lane_lo = _iota((LANES, PAIR), 1) < HEAD_DIM
    row_blocks = []
    for n in range(tm // LANES):
        rs = slice(n * LANES, (n + 1) * LANES)
        pair_blocks = []
        for j in range(w_c // PAIR):
            ls = slice(j * PAIR, (j + 1) * PAIR)
            v_p = cvn_b[rs, ls]
            mix_a = jnp.dot(wm_ref[2 * j], v_p, preferred_element_type=F32)
            mix_b = jnp.dot(wm_ref[2 * j + 1], v_p, preferred_element_type=F32)
            pair_blocks.append(cu[rs, ls] * (jnp.where(lane_lo, mix_a, mix_b) + bse_ref[:, ls]))
        row_blocks.append(jnp.concatenate(pair_blocks, axis=1))
    o_c = jnp.concatenate(row_blocks, axis=0)

    mix = jnp.concatenate([oa_ref[...].astype(BF16), ob_ref[...].astype(BF16), o_c.astype(BF16)], axis=1)
    x1 = x_ref[...] + jnp.dot(mix, wo_ref[...], preferred_element_type=F32)
    h2 = _rms(x1, nf_ref[...]).astype(BF16)

    row = _iota((tm, 1), 0)
    if multi_seq:
        pos = row % seq_len
    else:
        pos = row

        @pl.when(t == 0)
        def _():
            carry_ref[...] = pm_ref[0]

    def project(c0):
        cs = slice(c0, c0 + dff_chunk)
        return (jnp.dot(h2, wg_ref[:, cs], preferred_element_type=F32),
                jnp.dot(h2, wu_ref[:, cs], preferred_element_type=F32))

    n_chunks = d_ff // dff_chunk
    rounds_at = [[] for _ in range(n_chunks)]
    if decode is not None:
        rph = decode['rounds_per_host']
        first_round = (pl.program_id(0) * pl.num_programs(1) + t) * rph
        for r in range(rph):
            rounds_at[r * n_chunks // rph].append(r)
        round_cfg = {k: decode[k] for k in ('layer', 'n_new', 'pps', 'sub_pages', 'nj')}
        whole_sequences = rph % decode['nj'] == 0

    acc = jnp.zeros((tm, x_ref.shape[1]), F32)
    nxt = project(0)
    for ci, c0 in enumerate(range(0, d_ff, dff_chunk)):
        cs = slice(c0, c0 + dff_chunk)
        gp, up = nxt
        def scores(r):
            return _decode_round_scores(first_round + r, decode['n_steps'], decode_refs,
                                        j_static=r % decode['nj'] if whole_sequences else None,
                                        maybe_first=r == 0, maybe_last=r >= rph - 2, **round_cfg)

        hosted = rounds_at[ci]
        state = scores(hosted[0]) if hosted else None
        if c0 + dff_chunk < d_ff:
            nxt = project(c0 + dff_chunk)
        if hosted:
            _decode_round_values(state)
        for r in hosted[1:]:
            _decode_round_values(scores(r))
        if multi_seq:
            first1, first2 = pm_ref[0, :, cs], pm_ref[1, :, cs]
        else:
            c6, c7 = carry_ref[SUBLANES - 2:SUBLANES - 1, cs], carry_ref[SUBLANES - 1:SUBLANES, cs]
            first1 = c7
            first2 = jnp.where(row == 0, c6, c7)
            carry_ref[:, cs] = gp[tm - SUBLANES:tm, :]
        gm1 = jnp.where(pos >= 1, pltpu.roll(gp, 1, axis=0), first1)
        gm2 = jnp.where(pos >= 2, pltpu.roll(gp, 2, axis=0), first2)
        gate = cw_ref[0:1, cs] * gm2 + cw_ref[1:2, cs] * gm1 + cw_ref[2:3, cs] * gp
        act = (_silu(gate) * up).astype(BF16)
        acc = acc + jnp.dot(act, wd_ref[cs, :], preferred_element_type=F32)
        tail_ref[0, :, cs] = gp[tm - tail_rows:tm, :]
    x2 = x1 + acc
    if final_norm:
        x2 = _rms(x2, nfin_ref[...])
    y_ref[...] = x2


def _mix_ffn(x, oa, ob, cuv, cvw, wm, bse, wo, nf, wg, wu, cw, wd, pm, nfin, seq_len, tm, final_norm,
             decode_operands=None):
    n, d = x.shape
    multi_seq = seq_len < tm
    assert n % tm == 0 and (tm % seq_len == 0 if multi_seq else seq_len % tm == 0)
    assert tm % LANES == 0 and wm.shape[1] == LANES
    n_seq = 1 if multi_seq else n // seq_len
    seq_tiles = 1 if multi_seq else seq_len // tm
    n_tiles = n // tm
    d_ff = wg.shape[1]
    dff_chunk = 256
    assert d_ff % dff_chunk == 0
    tail_rows = tm if multi_seq else SUBLANES
    w_c = cuv.shape[1] // 2

    def rows(w):
        return pl.BlockSpec((tm, w), lambda s, t: (s * seq_tiles + t, 0))

    def const(a):
        nd = a.ndim
        return pl.BlockSpec(a.shape, lambda s, t: (0,) * nd, pipeline_mode=pl.Buffered(1))

    if multi_seq:
        pm_spec = pl.BlockSpec((2, tm, d_ff), lambda s, t: (0, s * seq_tiles + t, 0))
        grid = (n_tiles, 1)
    else:
        pm_spec = pl.BlockSpec((1, SUBLANES, d_ff), lambda s, t: (s, 0, 0))
        grid = (n_seq, seq_tiles)
    n_tail = n_tiles if multi_seq else n_seq
    args = [x, oa, ob, cuv, cvw, wm, bse, wo, nf, wg, wu, cw, wd, pm, nfin]
    in_specs = [rows(d), rows(oa.shape[1]), rows(ob.shape[1]), rows(cuv.shape[1]),
                const(cvw), const(wm), const(bse), const(wo), const(nf), const(wg), const(wu), const(cw),
                const(wd), pm_spec, const(nfin)]
    out_specs = [rows(d), pl.BlockSpec((1, tail_rows, d_ff), lambda s, t: (s, 0, 0)), rows(w_c)]
    out_shape = [jax.ShapeDtypeStruct((n, d), F32),
                 jax.ShapeDtypeStruct((n_tail, tail_rows, d_ff), F32),
                 jax.ShapeDtypeStruct((n, w_c), F32)]
    scratch = [pltpu.VMEM((SUBLANES, d_ff), F32)]
    decode = None
    if decode_operands is not None:
        d_args, d_in_specs, d_out_shape, d_out_spec, d_scratch, decode = _decode_plumbing(
            *decode_operands, n_hosts=n_tiles)
        args += d_args
        in_specs += d_in_specs
        out_specs[2] = d_out_spec
        out_shape[2] = d_out_shape
        scratch += d_scratch
    return pl.pallas_call(
        functools.partial(_mix_ffn_kernel, seq_len=seq_len, dff_chunk=dff_chunk, final_norm=final_norm,
                          multi_seq=multi_seq, decode=decode),
        grid=grid,
        in_specs=in_specs,
        out_specs=out_specs,
        out_shape=out_shape,
        scratch_shapes=scratch,
        compiler_params=_cparams(("arbitrary", "arbitrary")),
        name="mix_ffn",
    )(*args)


def _pad_axis(a, axis, before, after):
    pads = [(0, 0)] * a.ndim
    pads[axis] = (before, after)
    return jnp.pad(a, pads)


def _lanes_block(parts):
    lead = parts[0][1].shape[:-1]
    out = jnp.zeros(lead + (LANES,), F32)
    for off, a in parts:
        out = lax.dynamic_update_slice_in_dim(out, a.astype(F32), off, axis=a.ndim - 1)
    return out


def kernel(x_prompt, x_sample, cache_fox_k, cache_fox_v, cache_fox_logf, page_table, state_gdn, state_gdn_conv,
           state_ffn_conv, norm_mix, w_in, gdn_conv_w, gdn_a_log, gdn_dt_bias, gdn_norm_w, fox_f_bias,
           fox_out_norm, chunk_v_norm, chunk_ws, chunk_bs, w_out, norm_ffn, ffn_w_gate, ffn_w_up, ffn_conv_w,
           ffn_w_down, norm_final):
    depth, d_model, _ = w_in.shape
    b_p, seq, _ = x_prompt.shape
    b_s, t_s, _ = x_sample.shape
    gh, fh, cg = gdn_a_log.shape[1], fox_f_bias.shape[1], chunk_ws.shape[1]
    assert gdn_norm_w.shape[1] == HEAD_DIM and gh % 2 == 0 and fh % 2 == 0 and cg % 2 == 0
    w_a, w_b, w_c = gh * HEAD_DIM, fh * HEAD_DIM, cg * HEAD_DIM
    d_ff = ffn_w_gate.shape[2]
    conv_k = gdn_conv_w.shape[1]
    ffn_k = ffn_conv_w.shape[1]
    chunk_len = chunk_ws.shape[2]
    n_phys, page = cache_fox_k.shape[1], cache_fox_k.shape[2]
    assert gh <= SUBLANES and fh <= SUBLANES and conv_k - 1 <= t_s <= GDN_CHUNK and ffn_k == 3
    assert chunk_len == LANES and seq % chunk_len == 0 and t_s <= chunk_len

    sizes = [3 * w_a, w_a, gh, gh, 3 * w_b, fh, 2 * w_c]
    starts = [0]
    for s in sizes:
        starts.append(starts[-1] + s)
    i_aqkv, i_az, i_beta, i_alpha, i_bqkv, i_bf, i_cuv = starts[:-1]
    offs, pos = {}, 0
    for name, width in [('a_qkv', 3 * w_a), ('a_z', w_a), ('b_q', w_b), ('b_k', w_b), ('b_v', w_b),
                        ('c_uv', 2 * w_c), ('small', LANES)]:
        offs[name] = (pos, pos + width)
        pos += width

    xp = x_prompt.reshape(b_p * seq, d_model)
    xs = x_sample.reshape(b_s * t_s, d_model)
    cache_kt = cache_fox_k.transpose(0, 1, 3, 4, 2).reshape(depth, n_phys, w_b, page)
    cache_vt = cache_fox_v.transpose(0, 1, 3, 4, 2).reshape(depth, n_phys, w_b, page)
    suffix = _page_suffix(cache_fox_logf.transpose(0, 3, 1, 2), math.gcd(n_phys, 512))

    tri = jnp.tril(jnp.ones((chunk_len, chunk_len), bool))
    tri_s = jnp.tril(jnp.ones((t_s, t_s), bool))
    n_rep = chunk_len // t_s
    assert (b_s * t_s) % chunk_len == 0 and chunk_len % t_s == 0

    prompt_rows, sample_rows = [], []
    w_in_t = w_in.transpose(2, 0, 1)
    kt_all = vt_all = None
    logft_rows = []
    for l in range(depth):
        w = w_in_t[:, l, :]

        def zrows(n):
            return jnp.zeros((n, d_model), F32)

        small_w = jnp.concatenate([
            w[i_beta:i_beta + gh], zrows(SM_G - SM_BETA - gh),
            w[i_alpha:i_alpha + gh], zrows(SM_F - SM_G - gh),
            w[i_bf:i_bf + fh], zrows(LANES - SM_F - fh)], axis=0)
        w_r = jnp.concatenate([w[i_aqkv:i_aqkv + 3 * w_a], w[i_az:i_az + w_a],
                               w[i_bqkv:i_bqkv + 3 * w_b], w[i_cuv:i_cuv + 2 * w_c], small_w],
                              axis=0).astype(BF16)
        par = jnp.concatenate([
            _lanes_block([(SM_G, gdn_dt_bias[l][None, :]), (SM_F, fox_f_bias[l][None, :])]),
            _lanes_block([(SM_G, gdn_a_log[l][None, :])]),
            jnp.zeros((SUBLANES - 2, LANES), F32)], axis=0)
        nm = norm_mix[l][None, :]
        gnw = jnp.tile(gdn_norm_w[l], gh)[None, :]
        fnw = jnp.tile(fox_out_norm[l], fh)[None, :]
        cvw = jnp.tile(chunk_v_norm[l], cg)[None, :]
        wo = w_out[l].astype(BF16)
        wg, wu, wd = ffn_w_gate[l].astype(BF16), ffn_w_up[l].astype(BF16), ffn_w_down[l].astype(BF16)
        nf = norm_ffn[l][None, :]
        nfin = norm_final[None, :]
        last = l == depth - 1

        aqkv, az, bq, kt_all, vt_all, cuv, small, cumt, logft = _inproj(
            xp, nm, w_r, par, offs, seq, 512, kv_stack=(l, depth, kt_all, vt_all))
        logft_rows.append(logft)
        o_a, s_new = _gdn(aqkv.reshape(b_p, seq, 3 * w_a), az.reshape(b_p, seq, w_a),
                          small.reshape(b_p, seq, LANES), jnp.zeros((b_p, SUBLANES, 3 * w_a), F32),
                          jnp.zeros((b_p, gh, HEAD_DIM, HEAD_DIM), F32), gdn_conv_w[l], gnw, seq, 512, 1)
        o_b = _fox_prompt(l, bq.reshape(b_p, seq, w_b), kt_all, vt_all, cumt, fnw, 512)
        wm = jnp.where(tri, chunk_ws[l], 0.0).astype(BF16)
        bse = jnp.repeat(chunk_bs[l].T, HEAD_DIM, axis=1)
        n_s = b_s * t_s
        aqkv_p, cuv_p = aqkv, cuv
        aqkv, az, bq, bk, bv, cuv, small, cumt, _ = _inproj(xs, nm, w_r, par, offs, t_s, n_s)
        cn8 = _pad_axis(cumt.reshape(SUBLANES, b_s, t_s).transpose(1, 0, 2), 2, 0, SUBLANES - t_s)
        decode_operands = (l, page_table, bq.reshape(b_s, t_s, w_b),
                           _pad_axis(bk.reshape(b_s, t_s, w_b), 1, 0, SUBLANES - t_s),
                           _pad_axis(bv.reshape(b_s, t_s, w_b), 1, 0, SUBLANES - t_s),
                           cn8, fnw, cache_kt, cache_vt, suffix, t_s)
        xp, tail, o_b = _mix_ffn(xp, o_a.reshape(b_p * seq, w_a), o_b.reshape(b_p * seq, w_b), cuv_p, cvw, wm,
                                    bse, wo, nf, wg, wu, ffn_conv_w[l], wd,
                                    jnp.zeros((b_p, SUBLANES, d_ff), F32), nfin, seq, 512, last,
                                    decode_operands=decode_operands)
        prompt_rows.append((None, None, None, s_new,
                            aqkv_p.reshape(b_p, seq, 3 * w_a)[:, seq - (conv_k - 1):, :],
                            tail[:, SUBLANES - (ffn_k - 1):, :]))

        pad_t = GDN_CHUNK - t_s
        o_a, s_new = _gdn(_pad_axis(aqkv.reshape(b_s, t_s, 3 * w_a), 1, 0, pad_t),
                          _pad_axis(az.reshape(b_s, t_s, w_a), 1, 0, pad_t),
                          _pad_axis(small.reshape(b_s, t_s, LANES), 1, 0, pad_t),
                          _pad_axis(state_gdn_conv[l], 1, SUBLANES - (conv_k - 1), 0),
                          state_gdn[l], gdn_conv_w[l], gnw, t_s, GDN_CHUNK, math.gcd(b_s, 4))
        o_a = o_a[:, :t_s, :]
        eye_rep = jnp.eye(n_rep, dtype=F32)
        wm_s = jnp.stack([jnp.kron(eye_rep, jnp.where(tri_s, chunk_ws[l, g, :t_s, :t_s], 0.0))
                          for g in range(cg)]).astype(BF16)
        bse_s = jnp.repeat(jnp.tile(chunk_bs[l, :, :t_s], (1, n_rep)).T, HEAD_DIM, axis=1)
        buf = state_ffn_conv[l]
        zrow = jnp.zeros((b_s, 1, d_ff), F32)
        pm1 = jnp.concatenate([buf[:, 1:2]] + [zrow] * (t_s - 1), axis=1).reshape(n_s, d_ff)
        pm2 = jnp.concatenate([buf[:, 0:1], buf[:, 1:2]] + [zrow] * (t_s - 2), axis=1).reshape(n_s, d_ff)
        xs, tail, cvn = _mix_ffn(xs, o_a.reshape(n_s, w_a), o_b.reshape(n_s, w_b), cuv, cvw, wm_s, bse_s,
                                 wo, nf, wg, wu, ffn_conv_w[l], wd, jnp.stack([pm1, pm2]), nfin,
                                 t_s, chunk_len, last)
        sample_rows.append((bk.reshape(b_s, t_s, fh, HEAD_DIM), bv.reshape(b_s, t_s, fh, HEAD_DIM),
                            small[:, SM_F:SM_F + fh].reshape(b_s, t_s, fh), s_new,
                            aqkv.reshape(b_s, t_s, 3 * w_a)[:, t_s - (conv_k - 1):, :],
                            tail.reshape(b_s, t_s, d_ff)[:, t_s - (ffn_k - 1):, :],
                            cvn.reshape(b_s, t_s, w_c)))

    def stack(rows_, i):
        return jnp.stack([r[i] for r in rows_])

    y_prompt = xp.reshape(b_p, seq, d_model)
    y_sample = xs.reshape(b_s, t_s, d_model)
    fox_k_prompt = kt_all.reshape(depth, b_p, fh, HEAD_DIM, seq).transpose(0, 1, 4, 2, 3)
    fox_v_prompt = vt_all.reshape(depth, b_p, fh, HEAD_DIM, seq).transpose(0, 1, 4, 2, 3)
    fox_logf_prompt = jnp.stack(logft_rows)[:, :fh, :].reshape(depth, fh, b_p, seq).transpose(0, 2, 3, 1)
    return (y_prompt, y_sample,
            fox_k_prompt, fox_v_prompt, fox_logf_prompt, stack(prompt_rows, 3),
            stack(prompt_rows, 4), stack(prompt_rows, 5),
            stack(sample_rows, 0), stack(sample_rows, 1), stack(sample_rows, 2), stack(sample_rows, 3),
            stack(sample_rows, 4), stack(sample_rows, 5), stack(sample_rows, 6))
```

```python
import functools
import math

import jax
import jax.numpy as jnp
from jax import lax
from jax.experimental import pallas as pl
from jax.experimental.pallas import tpu as pltpu

F32 = jnp.float32
BF16 = jnp.bfloat16
EPS = 1e-6
LANES = 128
SUBLANES = 8
HEAD_DIM = 64
PAIR = 2 * HEAD_DIM
GDN_CHUNK = 64
FOX_PAGES_PER_STEP = 16
FOX_PAGES_PER_SUBBLOCK = 4
VMEM_LIMIT = 62 * 1024 * 1024
HIGHEST = lax.Precision.HIGHEST
NEG_INF = -1e30

SM_BETA, SM_G, SM_F = 0, 8, 16


def _cparams(sem):
    return pltpu.CompilerParams(dimension_semantics=sem, vmem_limit_bytes=VMEM_LIMIT)


def _dot(a, b):
    return jnp.dot(a.astype(BF16), b.astype(BF16), preferred_element_type=F32)


def _dot_nt(a, b):
    return lax.dot_general(a.astype(BF16), b.astype(BF16), (((1,), (1,)), ((), ())),
                           preferred_element_type=F32)


def _dot_tn(a, b):
    return lax.dot_general(a.astype(BF16), b.astype(BF16), (((0,), (0,)), ((), ())),
                           preferred_element_type=F32)


def _dot2(a, b01):
    hi = a.astype(BF16)
    lo = (a - hi.astype(F32)).astype(BF16)
    b = b01.astype(BF16)
    return (jnp.dot(hi, b, preferred_element_type=F32) + jnp.dot(lo, b, preferred_element_type=F32))


def _split3(a):
    a1 = a.astype(BF16)
    r1 = a - a1.astype(F32)
    a2 = r1.astype(BF16)
    a3 = (r1 - a2.astype(F32)).astype(BF16)
    return a1, a2, a3


def _dot01_left(m01, a):
    m = m01.astype(BF16)
    t1, t2, t3 = _split3(a)
    return (jnp.dot(m, t1, preferred_element_type=F32) + jnp.dot(m, t2, preferred_element_type=F32)
            + jnp.dot(m, t3, preferred_element_type=F32))


def _dot01_right(a, m01):
    m = m01.astype(BF16)
    t1, t2, t3 = _split3(a)
    return (jnp.dot(t1, m, preferred_element_type=F32) + jnp.dot(t2, m, preferred_element_type=F32)
            + jnp.dot(t3, m, preferred_element_type=F32))


def _iota(shape, dim):
    return lax.broadcasted_iota(jnp.int32, shape, dim)


def _head_ones(n):
    return (_iota((n, n), 0) // HEAD_DIM == _iota((n, n), 1) // HEAD_DIM).astype(F32)


def _silu(x):
    return x * jax.nn.sigmoid(x)


def _inproj_kernel(x_ref, nw_ref, wt_ref, par_ref, *refs, offs, seq_tiles, seq_len, kv_transposed, kv_slot,
                   n_alias):
    (aqkv_ref, az_ref, bq_ref, bk_ref, bv_ref, cuv_ref, small_ref, cumt_ref, logft_ref,
     carry_ref) = refs[n_alias:]
    i = pl.program_id(0)
    tm = x_ref.shape[0]
    x = x_ref[...]
    h = (x * lax.rsqrt(jnp.mean(x * x, axis=-1, keepdims=True) + EPS) * nw_ref[...]).astype(BF16)

    y = lax.dot_general(h, wt_ref[...], (((1,), (1,)), ((), ())), preferred_element_type=F32)

    def mm(name):
        lo, hi = offs[name]
        return y[:, lo:hi]

    aqkv_ref[...] = mm('a_qkv')
    az_ref[...] = mm('a_z')
    bq_ref[...] = mm('b_q')
    if kv_transposed:
        bk_ref[kv_slot, 0] = mm('b_k').T
        bv_ref[kv_slot, 0] = mm('b_v').T
        for d in range(bk_ref.shape[0]):
            if d != kv_slot:
                bk_ref[d, 0] = jnp.zeros(bk_ref.shape[2:], F32)
                bv_ref[d, 0] = jnp.zeros(bv_ref.shape[2:], F32)
    else:
        bk_ref[...] = mm('b_k')
        bv_ref[...] = mm('b_v')
    cuv_ref[...] = mm('c_uv')

    z = mm('small') + par_ref[0:1, :]
    lane = _iota(z.shape, 1)
    t = jnp.log1p(jnp.exp(-jnp.abs(z)))
    softplus = jnp.maximum(z, 0.0) + t
    log_sig = jnp.minimum(z, 0.0) - t
    sm = jnp.where(lane < SM_G, jax.nn.sigmoid(z),
                   jnp.where(lane < SM_F, -jnp.exp(par_ref[1:2, :]) * softplus, log_sig))
    small_ref[...] = sm

    lb = min(seq_len, tm)
    row, col = _iota((tm, tm), 0), _iota((tm, tm), 1)
    tri = jnp.where((col <= row) & (row // lb == col // lb), 1.0, 0.0)
    cum = _dot01_left(tri, sm)
    if seq_tiles > 1:
        @pl.when(i % seq_tiles == 0)
        def _():
            carry_ref[...] = jnp.zeros_like(carry_ref)
        cum = cum + carry_ref[0:1, :]
        carry_ref[0:1, :] = cum[tm - 1:tm, :]
    cumt_ref[...] = cum.T[SM_F:SM_F + SUBLANES, :]
    logft_ref[...] = sm.T[SM_F:SM_F + SUBLANES, :]


def _inproj(x, nw, wt, par, offs, seq_len, tm, kv_stack=None):
    n, d = x.shape
    assert n % tm == 0 and (seq_len % tm == 0 or tm % seq_len == 0)
    seq_tiles = max(seq_len // tm, 1)
    widths = {k: hi - lo for k, (lo, hi) in offs.items()}
    names = ['a_qkv', 'a_z', 'b_q', 'b_k', 'b_v', 'c_uv', 'small']
    out_shape = [jax.ShapeDtypeStruct((n, widths[k]), F32) for k in names]
    out_specs = [pl.BlockSpec((tm, widths[k]), lambda i: (i, 0)) for k in names]
    out_shape += [jax.ShapeDtypeStruct((SUBLANES, n), F32)] * 2
    out_specs += [pl.BlockSpec((SUBLANES, tm), lambda i: (0, i))] * 2
    args = [x, nw, wt, par]
    in_specs = [pl.BlockSpec((tm, d), lambda i: (i, 0)),
                pl.BlockSpec((1, d), lambda i: (0, 0)),
                pl.BlockSpec(wt.shape, lambda i: (0, 0)),
                pl.BlockSpec(par.shape, lambda i: (0, 0))]
    aliases = {}
    kv_slot = 0
    if kv_stack is not None:
        layer, depth, kt_prev, vt_prev = kv_stack
        assert seq_len % tm == 0
        w_b = widths['b_k']
        if kt_prev is None:
            kv_slot, blk, first = layer, depth, 0
        else:
            kv_slot, blk, first = 0, 1, layer
            args += [kt_prev, vt_prev]
            in_specs += [pl.BlockSpec(memory_space=pl.ANY)] * 2
            aliases = {4: 3, 5: 4}
        for pos in (3, 4):
            out_shape[pos] = jax.ShapeDtypeStruct((depth, n // seq_len, w_b, seq_len), F32)
            out_specs[pos] = pl.BlockSpec((blk, 1, w_b, tm), lambda i: (first, i // seq_tiles, 0, i % seq_tiles))
    return pl.pallas_call(
        functools.partial(_inproj_kernel, offs=offs, seq_tiles=seq_tiles, seq_len=seq_len,
                          kv_transposed=kv_stack is not None, kv_slot=kv_slot, n_alias=len(aliases)),
        grid=(n // tm,),
        in_specs=in_specs,
        out_specs=out_specs,
        out_shape=out_shape,
        input_output_aliases=aliases,
        scratch_shapes=[pltpu.VMEM((SUBLANES, LANES), F32)],
        compiler_params=_cparams(("arbitrary",)),
        name="inproj",
    )(*args)


def _unit_lower_inverses(lmats):
    n = lmats[0].shape[0]
    row, col = _iota((n, n), 0), _iota((n, n), 1)
    eye = (row == col).astype(F32)
    base = SUBLANES
    in_base = row // base == col // base
    lds = [jnp.where(in_base, lm, 0.0) for lm in lmats]
    xs = [eye - ld for ld in lds]
    ps = [_dot(ld, ld) for ld in lds]
    xs = [x + _dot(x, p) for x, p in zip(xs, ps)]
    ps = [_dot(p, p) for p in ps]
    xs = [x + _dot(x, p) for x, p in zip(xs, ps)]
    s = base
    while s < HEAD_DIM:
        off_mask = (row // (2 * s) == col // (2 * s)) & (row // s != col // s)
        ts_ = [_dot(x, jnp.where(off_mask, lm, 0.0)) for x, lm in zip(xs, lmats)]
        xs = [x - _dot(t_, x) for x, t_ in zip(xs, ts_)]
        s *= 2
    return xs


def _gdn_kernel(aqkv_ref, az_ref, small_ref, buf_ref, s0_ref, cw_ref, nw_ref,
                o_ref, sout_ref,
                s_ref, prev_ref, qs_ref, qg_ref, k_ref, kb_ref, kbg_ref, vb_ref, gc_ref, oacc_ref,
                conv_ref, uw_ref, at_ref, *, n_pairs, valid_len, group):
    t = pl.program_id(1)
    nt = pl.num_programs(1)
    nb, ts, w3 = aqkv_ref.shape
    w_a = n_pairs * PAIR
    c = GDN_CHUNK
    rows_all = nb * ts
    cps = ts // c

    @pl.when(t == 0)
    def _():
        for bi in range(nb):
            for j in range(n_pairs):
                s_a = s0_ref[bi, 2 * j]
                s_b = s0_ref[bi, 2 * j + 1]
                top = jnp.concatenate([s_a, jnp.zeros_like(s_a)], axis=1)
                bot = jnp.concatenate([jnp.zeros_like(s_b), s_b], axis=1)
                s_ref[bi, j] = jnp.concatenate([top, bot], axis=0)
        prev_ref[...] = buf_ref[...]

    x = aqkv_ref[...].reshape(rows_all, w3)
    kw = cw_ref.shape[0]
    conv = cw_ref[kw - 1:kw, :] * x
    for i in range(1, kw):
        conv = conv + cw_ref[kw - 1 - i:kw - i, :] * pltpu.roll(x, i, axis=0)
    conv_ref[...] = conv
    for bi in range(nb):
        head = jnp.concatenate([prev_ref[bi], x[bi * ts:bi * ts + SUBLANES]], axis=0)
        fixed = cw_ref[0:1, :] * head[SUBLANES - kw + 1:2 * SUBLANES - kw + 1]
        for i in range(1, kw):
            fixed = fixed + cw_ref[i:i + 1, :] * head[SUBLANES - kw + 1 + i:2 * SUBLANES - kw + 1 + i]
        conv_ref[bi * ts:bi * ts + SUBLANES, :] = fixed
        prev_ref[bi] = x[(bi + 1) * ts - SUBLANES:(bi + 1) * ts]
    a = _silu(conv_ref[...])
    rowid = _iota((rows_all, 1), 0) % ts + t * ts
    valid = (rowid < valid_len).astype(F32)
    ones_h = _head_ones(w_a)
    q = a[:, 0:w_a]
    k = a[:, w_a:2 * w_a]
    v = a[:, 2 * w_a:3 * w_a] * valid
    qn = q * lax.rsqrt(_dot2(q * q, ones_h) + EPS) * valid
    kn = k * lax.rsqrt(_dot2(k * k, ones_h) + EPS) * valid

    sm = small_ref[...].reshape(rows_all, LANES) * valid
    row, col = _iota((rows_all, rows_all), 0), _iota((rows_all, rows_all), 1)
    tri = jnp.where((col <= row) & (row // c == col // c), 1.0, 0.0)
    gcum = _dot01_left(tri, sm)
    gc_ref[...] = gcum
    lane, hrow = _iota((LANES, w_a), 1), _iota((LANES, w_a), 0)
    exp_b = (hrow == lane // HEAD_DIM + SM_BETA).astype(F32)
    exp_g = (hrow == lane // HEAD_DIM + SM_G).astype(F32)
    beta_e = _dot2(sm, exp_b)
    gcum_e = _dot01_right(gcum, exp_g)
    eg = jnp.exp(gcum_e)
    scale = HEAD_DIM ** -0.5
    qs_ref[...] = qn * scale
    qg_ref[...] = qn * (scale * eg)
    k_ref[...] = kn
    kb = kn * beta_e
    kb_ref[...] = kb
    kbg_ref[...] = kb * eg
    vb_ref[...] = v * beta_e

    r2, c2 = _iota((PAIR, PAIR), 0), _iota((PAIR, PAIR), 1)
    same_head = (r2 // HEAD_DIM) == (c2 // HEAD_DIM)
    strict = same_head & (c2 < r2)
    rw, cw_ = _iota((c, PAIR), 0), _iota((c, PAIR), 1)
    incl_wide = (cw_ % HEAD_DIM) <= rw
    lane_lo = _iota((c, PAIR), 1) < HEAD_DIM

    def stack(m):
        return jnp.concatenate([jnp.where(lane_lo, m, 0.0), jnp.where(lane_lo, 0.0, m)], axis=0)

    def gate_cols(g_c, j):
        return g_c[:, SM_G + 2 * j:SM_G + 2 * j + 1], g_c[:, SM_G + 2 * j + 1:SM_G + 2 * j + 2]

    def prepare(gi, carry):
        items = []
        for i in range(group):
            ci = gi * group + i
            r0 = pl.multiple_of(ci * c, c)
            g_c = gc_ref[pl.ds(r0, c), :]
            g_t = g_c.T
            for j in range(n_pairs):
                ls = slice(j * PAIR, (j + 1) * PAIR)
                ga_col, gb_col = gate_cols(g_c, j)
                g_row = jnp.concatenate([g_t[SM_G + 2 * j:SM_G + 2 * j + 1, :],
                                         g_t[SM_G + 2 * j + 1:SM_G + 2 * j + 2, :]], axis=1)
                g_col2 = jnp.concatenate([ga_col, gb_col], axis=0)
                g_wide = jnp.where(lane_lo, ga_col, gb_col)
                items.append(dict(
                    ci=ci, j=j,
                    k2=stack(k_ref[pl.ds(r0, c), ls]),
                    kb2=stack(kb_ref[pl.ds(r0, c), ls]),
                    rhs=jnp.concatenate([stack(vb_ref[pl.ds(r0, c), ls]), stack(kbg_ref[pl.ds(r0, c), ls])],
                                        axis=1),
                    qs=qs_ref[pl.ds(r0, c), ls],
                    decay2=jnp.exp(jnp.minimum(g_col2 - g_row, 0.0)),
                    decay_w=jnp.exp(jnp.minimum(g_wide - g_row, 0.0))))
        kk = [_dot_nt(it['kb2'], it['k2']) for it in items]
        qk = [_dot_nt(it['qs'], it['k2']) for it in items]
        lmats = [jnp.where(strict, m * it['decay2'], 0.0) for m, it in zip(kk, items)]
        for m, it in zip(qk, items):
            at_ref[it['ci'], it['j']] = jnp.where(incl_wide, m * it['decay_w'], 0.0)
        tinvs = _unit_lower_inverses(lmats)
        sols = [_dot(tinv, it['rhs']) for tinv, it in zip(tinvs, items)]
        for sol, it in zip(sols, items):
            uw_ref[it['ci'], it['j']] = sol
        return carry

    lax.fori_loop(0, rows_all // c // group, prepare, 0)

    def recur(n, carry):
        items = []
        for bi in range(nb):
            ci = bi * cps + n
            r0 = pl.multiple_of(ci * c, c)
            g_c = gc_ref[pl.ds(r0, c), :]
            for j in range(n_pairs):
                ls = slice(j * PAIR, (j + 1) * PAIR)
                ga_col, gb_col = gate_cols(g_c, j)
                g_wide = jnp.where(lane_lo, ga_col, gb_col)
                g_last = jnp.where(lane_lo, ga_col[c - 1:c, :], gb_col[c - 1:c, :])
                g_last_col = jnp.where(_iota((PAIR, 1), 0) < HEAD_DIM, ga_col[c - 1:c, :], gb_col[c - 1:c, :])
                items.append(dict(
                    bi=bi, r0=r0, ls=ls, j=j,
                    u2=uw_ref[ci, j, :, 0:PAIR], w2=uw_ref[ci, j, :, PAIR:2 * PAIR], attn=at_ref[ci, j],
                    s2=s_ref[bi, j], qg=qg_ref[pl.ds(r0, c), ls],
                    kdec=k_ref[pl.ds(r0, c), ls] * jnp.exp(g_last - g_wide),
                    s_decay=jnp.exp(g_last_col)))
        ws = [_dot(it['w2'], it['s2']) for it in items]
        qss = [_dot(it['qg'], it['s2']) for it in items]
        vnew2 = [it['u2'] - w for it, w in zip(items, ws)]
        avs = [_dot(it['attn'], v2) for it, v2 in zip(items, vnew2)]
        kvs = [_dot_tn(it['kdec'], v2[:c, :] + v2[c:, :]) for it, v2 in zip(items, vnew2)]
        for it, q_s, av, kv in zip(items, qss, avs, kvs):
            oacc_ref[pl.ds(it['r0'], c), it['ls']] = q_s + av
            s_ref[it['bi'], it['j']] = it['s2'] * it['s_decay'] + jnp.where(same_head, kv, 0.0)
        return carry

    for n in range(cps):
        recur(n, 0)

    o = oacc_ref[...]
    on = o * lax.rsqrt(_dot2(o * o, ones_h) * (1.0 / HEAD_DIM) + EPS) * nw_ref[...]
    o_ref[...] = (on * _silu(az_ref[...].reshape(rows_all, w_a))).reshape(nb, ts, w_a)

    @pl.when(t == nt - 1)
    def _():
        for bi in range(nb):
            for j in range(n_pairs):
                s2 = s_ref[bi, j]
                sout_ref[bi, 2 * j] = s2[:HEAD_DIM, :HEAD_DIM]
                sout_ref[bi, 2 * j + 1] = s2[HEAD_DIM:, HEAD_DIM:]


def _gdn(aqkv, az, small, bufpad, s0, conv_w, norm_w_tiled, valid_len, ts, nb):
    b, l, w3 = aqkv.shape
    w_a = w3 // 3
    n_pairs = w_a // PAIR
    heads = s0.shape[1]
    n_chunks = nb * ts // GDN_CHUNK
    group = math.gcd(n_chunks, 8)
    assert l % ts == 0 and ts % GDN_CHUNK == 0 and heads == 2 * n_pairs and b % nb == 0
    rows_all = nb * ts
    scratch = [pltpu.VMEM((nb, n_pairs, PAIR, PAIR), F32), pltpu.VMEM((nb, SUBLANES, w3), F32)]
    scratch += [pltpu.VMEM((rows_all, w_a), F32) for _ in range(6)]
    scratch += [pltpu.VMEM((rows_all, LANES), F32), pltpu.VMEM((rows_all, w_a), F32),
                pltpu.VMEM((rows_all, w3), F32),
                pltpu.VMEM((n_chunks, n_pairs, PAIR, 2 * PAIR), F32),
                pltpu.VMEM((n_chunks, n_pairs, GDN_CHUNK, PAIR), F32)]
    return pl.pallas_call(
        functools.partial(_gdn_kernel, n_pairs=n_pairs, valid_len=valid_len, group=group),
        grid=(b // nb, l // ts),
        in_specs=[pl.BlockSpec((nb, ts, w3), lambda i, t: (i, t, 0)),
                  pl.BlockSpec((nb, ts, w_a), lambda i, t: (i, t, 0)),
                  pl.BlockSpec((nb, ts, LANES), lambda i, t: (i, t, 0)),
                  pl.BlockSpec((nb, SUBLANES, w3), lambda i, t: (i, 0, 0)),
                  pl.BlockSpec((nb, heads, HEAD_DIM, HEAD_DIM), lambda i, t: (i, 0, 0, 0)),
                  pl.BlockSpec(conv_w.shape, lambda i, t: (0, 0)),
                  pl.BlockSpec(norm_w_tiled.shape, lambda i, t: (0, 0))],
        out_specs=[pl.BlockSpec((nb, ts, w_a), lambda i, t: (i, t, 0)),
                   pl.BlockSpec((nb, heads, HEAD_DIM, HEAD_DIM), lambda i, t: (i, 0, 0, 0))],
        out_shape=[jax.ShapeDtypeStruct((b, l, w_a), F32),
                   jax.ShapeDtypeStruct((b, heads, HEAD_DIM, HEAD_DIM), F32)],
        scratch_shapes=scratch,
        compiler_params=_cparams(("arbitrary", "arbitrary")),
        name="gdn",
    )(aqkv, az, small, bufpad, s0, conv_w, norm_w_tiled)


def _fox_prompt_kernel(q_ref, kt_ref, vt_ref, cumt_ref, nw_ref, *refs, n_pairs, q_tile, fresh):
    o_ref = refs[-1]
    i = q_tile
    tq = q_ref.shape[1]
    tk = tq
    scale = HEAD_DIM ** -0.5
    lane_lo = _iota((tq, PAIR), 1) < HEAD_DIM
    causal = _iota((tq, tk), 1) <= _iota((tq, tk), 0)
    q0 = pl.multiple_of(i * tq, tq)
    pairs = []
    for j in range(n_pairs):
        ls = slice(j * PAIR, (j + 1) * PAIR)
        q_p = q_ref[0, :, ls] * scale
        heads = (2 * j, 2 * j + 1)
        q_hs = [jnp.where(lane_lo, q_p, 0.0).astype(BF16), jnp.where(lane_lo, 0.0, q_p).astype(BF16)]
        c_refs = [cumt_ref[h:h + 1, pl.ds(q0, LANES)][:, 0:1] for h in heads]

        def block(k0, carry, masked, q_hs=q_hs, c_refs=c_refs, heads=heads, ls=ls):
            k_blk = kt_ref[0, 0, ls, pl.ds(k0, tk)].astype(BF16)
            v_blk = vt_ref[0, 0, ls, pl.ds(k0, tk)].astype(BF16)
            ss = [jnp.dot(q_h, k_blk, preferred_element_type=F32) for q_h in q_hs]
            ss = [s + (c_ref - cumt_ref[h:h + 1, pl.ds(k0, tk)]) for s, c_ref, h in zip(ss, c_refs, heads)]
            if masked:
                ss = [jnp.where(causal, s, NEG_INF) for s in ss]
            m_news = [jnp.maximum(st[0], jnp.max(s, axis=-1, keepdims=True)) for st, s in zip(carry, ss)]
            ps = [jnp.exp(s - m_new) for s, m_new in zip(ss, m_news)]
            alphas = [jnp.exp(st[0] - m_new) for st, m_new in zip(carry, m_news)]
            pvs = [lax.dot_general(p.astype(BF16), v_blk, (((1,), (1,)), ((), ())), preferred_element_type=F32)
                   for p in ps]
            return tuple((m_new, a * st[1] + jnp.sum(p, axis=-1, keepdims=True), a * st[2] + pv)
                         for st, m_new, a, p, pv in zip(carry, m_news, alphas, ps, pvs))

        init = tuple((jnp.full((tq, 1), NEG_INF, F32), jnp.zeros((tq, 1), F32), jnp.zeros((tq, PAIR), F32))
                     for _ in heads)
        carry = init
        for kb in range(i):
            carry = block(kb * tk, carry, False)
        outs = [acc / l for _, l, acc in block(q0, carry, True)]
        pairs.append(jnp.where(lane_lo, outs[0], outs[1]))
    o = jnp.concatenate(pairs, axis=1)
    ones_h = _head_ones(o.shape[1])
    res = o * lax.rsqrt(_dot2(o * o, ones_h) * (1.0 / HEAD_DIM) + EPS) * nw_ref[...]
    if fresh:
        for t in range(o_ref.shape[1] // tq):
            o_ref[0, t * tq:(t + 1) * tq, :] = res if t == i else jnp.zeros_like(res)
    else:
        o_ref[0] = res


def _fox_prompt(layer, q, kt, vt, cumt, norm_w_tiled, tq):
    b, l, w_b = q.shape
    assert l % tq == 0 and tq % LANES == 0
    out = None
    for it in range(l // tq):
        fresh = out is None
        args = [q, kt, vt, cumt, norm_w_tiled]
        in_specs = [pl.BlockSpec((1, tq, w_b), lambda i, it=it: (i, it, 0)),
                    pl.BlockSpec((1, 1, w_b, l), lambda i: (layer, i, 0, 0)),
                    pl.BlockSpec((1, 1, w_b, l), lambda i: (layer, i, 0, 0)),
                    pl.BlockSpec((SUBLANES, l), lambda i: (0, i)),
                    pl.BlockSpec(norm_w_tiled.shape, lambda i: (0, 0))]
        if fresh:
            out_spec, aliases = pl.BlockSpec((1, l, w_b), lambda i: (i, 0, 0)), {}
        else:
            args.append(out)
            in_specs.append(pl.BlockSpec(memory_space=pl.ANY))
            out_spec, aliases = pl.BlockSpec((1, tq, w_b), lambda i, it=it: (i, it, 0)), {5: 0}
        out = pl.pallas_call(
            functools.partial(_fox_prompt_kernel, n_pairs=w_b // PAIR, q_tile=it, fresh=fresh),
            grid=(b,),
            in_specs=in_specs,
            out_specs=out_spec,
            out_shape=jax.ShapeDtypeStruct((b, l, w_b), F32),
            input_output_aliases=aliases,
            compiler_params=_cparams(("arbitrary",)),
            name="fox_prompt",
        )(*args)
    return out


def _page_suffix_kernel(x_ref, m_ref, o_ref):
    n_heads = x_ref.shape[1]
    for h in range(SUBLANES):
        if h < n_heads:
            o_ref[0, :, h, :] = _dot01_right(x_ref[0, h], m_ref[...])
        else:
            o_ref[0, :, h, :] = jnp.zeros((o_ref.shape[1], o_ref.shape[3]), F32)


def _page_suffix(logf_t, tp):
    depth, n_heads, n_phys, page = logf_t.shape
    assert page == LANES and n_phys % tp == 0
    idx = jnp.arange(page)
    m = (idx[:, None] >= idx[None, :]).astype(BF16)
    return pl.pallas_call(
        _page_suffix_kernel,
        grid=(depth, n_phys // tp),
        in_specs=[pl.BlockSpec((1, n_heads, tp, page), lambda d, i: (d, 0, i, 0)),
                  pl.BlockSpec(m.shape, lambda d, i: (0, 0))],
        out_specs=pl.BlockSpec((1, tp, SUBLANES, page), lambda d, i: (d, i, 0, 0)),
        out_shape=jax.ShapeDtypeStruct((depth, n_phys, SUBLANES, page), F32),
        compiler_params=_cparams(("arbitrary", "arbitrary")),
        name="page_suffix",
    )(logf_t, m)


def _run_if(cond, fn):
    if isinstance(cond, bool):
        if cond:
            fn()
    else:
        pl.when(cond)(fn)


def _decode_round_scores(step, n_steps, refs, *, layer, n_new, pps, sub_pages, nj, j_static, maybe_first,
                         maybe_last):
    (pt_ref, q_ref, kn_ref, vn_ref, cn_ref, nw_ref, kt_hbm, vt_hbm, sfx_hbm, o_ref,
     kbuf, vbuf, sbuf, ksem, vsem, ssem, qbd_ref, m_ref, l_ref, acc_ref, carry_ref) = refs
    bb = step // nj
    j = step % nj if j_static is None else j_static
    n_tok = q_ref.shape[1]
    w_b = q_ref.shape[2]
    rows = n_tok * SUBLANES
    page = kbuf.shape[3]
    n_pages = nj * pps
    scale = HEAD_DIM ** -0.5
    slot = step % 2

    def page_copies(step_x, slot_x, i):
        pg = pt_ref[step_x // nj, n_pages - 1 - ((step_x % nj) * pps + i)]
        return (pltpu.make_async_copy(kt_hbm.at[layer, pg], kbuf.at[slot_x, i], ksem.at[slot_x]),
                pltpu.make_async_copy(vt_hbm.at[layer, pg], vbuf.at[slot_x, i], vsem.at[slot_x]),
                pltpu.make_async_copy(sfx_hbm.at[layer, pg], sbuf.at[slot_x, i], ssem.at[slot_x]))

    def request(step_x, slot_x):
        for i in range(pps):
            for cp in page_copies(step_x, slot_x, i):
                cp.start()

    if maybe_first:
        def prime():
            request(step, slot)
            if n_steps > 1:
                request(step + 1, 1 - slot)
        _run_if(step == 0, prime)

    for i in range(pps):
        for cp in page_copies(step, slot, i):
            cp.wait()

    def start_sequence():
        q = q_ref[bb] * scale
        headmask = (_iota((SUBLANES, w_b), 1) // HEAD_DIM == _iota((SUBLANES, w_b), 0)).astype(F32)
        qbd = jnp.concatenate([q[t:t + 1, :] * headmask for t in range(n_tok)], axis=0)
        qbd_ref[...] = qbd.astype(BF16)
        kn = kn_ref[bb]
        s = _dot_nt(qbd, kn)
        cn = cn_ref[bb]
        s = s - jnp.concatenate([cn] * n_tok, axis=0)
        kidx = _iota((rows, SUBLANES), 1)
        tidx = _iota((rows, SUBLANES), 0) // SUBLANES
        s = jnp.where((kidx <= tidx) & (kidx < n_new), s, NEG_INF)
        m = jnp.max(s, axis=-1, keepdims=True)
        p = jnp.exp(s - m)
        m_ref[...] = m
        l_ref[...] = jnp.sum(p, axis=-1, keepdims=True)
        acc_ref[...] = _dot(p, vn_ref[bb])
        carry_ref[...] = jnp.zeros_like(carry_ref)

    _run_if(j == 0, start_sequence)

    lane = _iota((SUBLANES, page), 1)
    carry = carry_ref[...]
    biases = []
    for i in range(pps):
        incl = sbuf[slot, i]
        excl = jnp.where(lane < page - 1, pltpu.roll(incl, page - 1, axis=1), 0.0)
        biases.append(excl + carry)
        carry = carry + incl[:, 0:1]
    carry_ref[...] = carry
    qbd = qbd_ref[...]
    n_sub = pps // sub_pages
    subs = [range(u * sub_pages, (u + 1) * sub_pages) for u in range(n_sub)]
    kts = [jnp.concatenate([kbuf[slot, i].astype(BF16) for i in sb], axis=1) for sb in subs]
    ss = [jnp.dot(qbd, kt, preferred_element_type=F32) for kt in kts]
    ss = [s + jnp.concatenate([jnp.concatenate([biases[i] for i in sb], axis=1)] * n_tok, axis=0)
          for s, sb in zip(ss, subs)]
    ms = [jnp.max(s, axis=-1, keepdims=True) for s in ss]
    ps = [jnp.exp(s - m) for s, m in zip(ss, ms)]
    ls = [jnp.sum(p, axis=-1, keepdims=True) for p in ps]
    refill = functools.partial(_run_if, step + 2 < n_steps if maybe_last else True,
                               lambda: request(step + 2, slot))
    return dict(refs=refs, slot=slot, subs=subs, ms=ms, ps=ps, ls=ls, bb=bb, last_of_sequence=j == nj - 1,
                refill=refill)


def _decode_round_values(state):
    (_, q_ref, _, _, _, nw_ref, _, _, _, o_ref,
     _, vbuf, _, _, _, _, _, m_ref, l_ref, acc_ref, _) = state['refs']
    slot, subs, ms, ps, ls, bb = (state[k] for k in ('slot', 'subs', 'ms', 'ps', 'ls', 'bb'))
    n_tok = q_ref.shape[1]
    w_b = q_ref.shape[2]
    rows = n_tok * SUBLANES
    vts = [jnp.concatenate([vbuf[slot, i].astype(BF16) for i in sb], axis=1) for sb in subs]
    pvs = [lax.dot_general(p.astype(BF16), vt, (((1,), (1,)), ((), ())), preferred_element_type=F32)
           for p, vt in zip(ps, vts)]
    state['refill']()
    m_old = m_ref[...]
    m_new = m_old
    for m in ms:
        m_new = jnp.maximum(m_new, m)
    alpha = jnp.exp(m_old - m_new)
    l_new = alpha * l_ref[...]
    acc = alpha * acc_ref[...]
    for m, l_u, pv in zip(ms, ls, pvs):
        w_u = jnp.exp(m - m_new)
        l_new = l_new + w_u * l_u
        acc = acc + w_u * pv
    m_ref[...] = m_new
    l_ref[...] = l_new
    acc_ref[...] = acc

    def finish_sequence():
        o = acc_ref[...] / l_ref[...]
        diag = (_iota((rows, w_b), 1) // HEAD_DIM) == (_iota((rows, w_b), 0) % SUBLANES)
        o = jnp.where(diag, o, 0.0)
        group = (_iota((SUBLANES, rows), 1) // SUBLANES == _iota((SUBLANES, rows), 0)).astype(F32)
        o_tok = jnp.dot(group, o, precision=HIGHEST, preferred_element_type=F32)
        ones_h = _head_ones(w_b)
        ss = jnp.dot(o_tok * o_tok, ones_h, precision=HIGHEST, preferred_element_type=F32)
        on = o_tok * lax.rsqrt(ss * (1.0 / HEAD_DIM) + EPS) * nw_ref[...]
        o_ref[bb] = on[0:n_tok, :]

    _run_if(state['last_of_sequence'], finish_sequence)


N_DECODE_INPUTS = 9
N_DECODE_SCRATCH = 11


def _decode_plumbing(layer, page_table, q, k_new8, v_new8, cn8, norm_w_tiled, cache_kt, cache_vt, suffix, n_new,
                     n_hosts):
    b, n_tok, w_b = q.shape
    page = cache_kt.shape[3]
    n_pages = page_table.shape[1]
    pps = math.gcd(n_pages, FOX_PAGES_PER_STEP)
    sub_pages = math.gcd(pps, FOX_PAGES_PER_SUBBLOCK)
    nj = n_pages // pps
    assert (b * nj) % n_hosts == 0
    rows = n_tok * SUBLANES
    n_slots = 2

    def whole(a):
        nd = a.ndim
        return pl.BlockSpec(a.shape, lambda *_: (0,) * nd)

    args = [page_table, q, k_new8, v_new8, cn8, norm_w_tiled, cache_kt, cache_vt, suffix]
    in_specs = [pl.BlockSpec(memory_space=pltpu.SMEM), whole(q), whole(k_new8), whole(v_new8), whole(cn8),
                whole(norm_w_tiled), pl.BlockSpec(memory_space=pl.ANY), pl.BlockSpec(memory_space=pl.ANY),
                pl.BlockSpec(memory_space=pl.ANY)]
    assert len(args) == N_DECODE_INPUTS
    out_shape = jax.ShapeDtypeStruct((b, n_tok, w_b), F32)
    out_spec = pl.BlockSpec((b, n_tok, w_b), lambda *_: (0, 0, 0))
    scratch = [pltpu.VMEM((n_slots, pps, w_b, page), F32), pltpu.VMEM((n_slots, pps, w_b, page), F32),
               pltpu.VMEM((n_slots, pps, SUBLANES, page), F32),
               pltpu.SemaphoreType.DMA((n_slots,)), pltpu.SemaphoreType.DMA((n_slots,)),
               pltpu.SemaphoreType.DMA((n_slots,)),
               pltpu.VMEM((rows, w_b), BF16), pltpu.VMEM((rows, 1), F32), pltpu.VMEM((rows, 1), F32),
               pltpu.VMEM((rows, w_b), F32), pltpu.VMEM((SUBLANES, page), F32)]
    assert len(scratch) == N_DECODE_SCRATCH
    config = dict(layer=layer, n_new=n_new, pps=pps, sub_pages=sub_pages, nj=nj,
                  rounds_per_host=b * nj // n_hosts, n_steps=b * nj)
    return args, in_specs, out_shape, out_spec, scratch, config


def _rms(x, w):
    return x * lax.rsqrt(jnp.mean(x * x, axis=-1, keepdims=True) + EPS) * w


def _gelu_tanh(x):
    return x * (0.5 * (1.0 + jnp.tanh(math.sqrt(2.0 / math.pi) * (x + 0.044715 * (x * x * x)))))


def _mix_ffn_kernel(x_ref, oa_ref, ob_ref, cuv_ref, cvw_ref, wm_ref, bse_ref, wo_ref, nf_ref,
                    wg_ref, wu_ref, cw_ref, wd_ref, pm_ref, nfin_ref, *refs,
                    seq_len, dff_chunk, final_norm, multi_seq, decode):
    if decode is None:
        y_ref, tail_ref, cv_ref, carry_ref = refs
        decode_refs = None
    else:
        dec_in, rest = refs[:N_DECODE_INPUTS], refs[N_DECODE_INPUTS:]
        y_ref, tail_ref, dec_out, carry_ref = rest[:4]
        cv_ref = None
        decode_refs = tuple(dec_in) + (dec_out,) + tuple(rest[4:])
    t = pl.program_id(1)
    tm = x_ref.shape[0]
    w_a, w_b = oa_ref.shape[1], ob_ref.shape[1]
    w_c = cuv_ref.shape[1] // 2
    d_ff = wg_ref.shape[1]
    tail_rows = tail_ref.shape[1]

    gel = _gelu_tanh(cuv_ref[...])
    cu, cv = gel[:, :w_c], gel[:, w_c:]
    ones_h = _head_ones(w_c)
    cvn = cv * lax.rsqrt(_dot2(cv * cv, ones_h) * (1.0 / HEAD_DIM) + EPS) * cvw_ref[...]
    if cv_ref is not None:
        cv_ref[...] = cvn
    cvn_b = cvn.astype(BF16)
    lane_lo = _iota((LANES, PAIR), 1) < HEAD_DIM
    row_blocks = []
    for n in range(tm // LANES):
        rs = slice(n * LANES, (n + 1) * LANES)
        pair_blocks = []
        for j in range(w_c // PAIR):
            ls = slice(j * PAIR, (j + 1) * PAIR)
            v_p = cvn_b[rs, ls]
            mix_a = jnp.dot(wm_ref[2 * j], v_p, preferred_element_type=F32)
            mix_b = jnp.dot(wm_ref[2 * j + 1], v_p, preferred_element_type=F32)
            pair_blocks.append(cu[rs, ls] * (jnp.where(lane_lo, mix_a, mix_b) + bse_ref[:, ls]))
        row_blocks.append(jnp.concatenate(pair_blocks, axis=1))
    o_c = jnp.concatenate(row_blocks, axis=0)

    mix = jnp.concatenate([oa_ref[...].astype(BF16), ob_ref[...].astype(BF16), o_c.astype(BF16)], axis=1)
    x1 = x_ref[...] + jnp.dot(mix, wo_ref[...], preferred_element_type=F32)
    h2 = _rms(x1, nf_ref[...]).astype(BF16)

    row = _iota((tm, 1), 0)
    if multi_seq:
        pos = row % seq_len
    else:
        pos = row

        @pl.when(t == 0)
        def _():
            carry_ref[...] = pm_ref[0]

    def project(c0):
        cs = slice(c0, c0 + dff_chunk)
        return (jnp.dot(h2, wg_ref[:, cs], preferred_element_type=F32),
                jnp.dot(h2, wu_ref[:, cs], preferred_element_type=F32))

    n_chunks = d_ff // dff_chunk
    rounds_at = [[] for _ in range(n_chunks)]
    if decode is not None:
        rph = decode['rounds_per_host']
        first_round = (pl.program_id(0) * pl.num_programs(1) + t) * rph
        for r in range(rph):
            rounds_at[r * n_chunks // rph].append(r)
        round_cfg = {k: decode[k] for k in ('layer', 'n_new', 'pps', 'sub_pages', 'nj')}
        whole_sequences = rph % decode['nj'] == 0

    acc = jnp.zeros((tm, x_ref.shape[1]), F32)
    nxt = project(0)
    for ci, c0 in enumerate(range(0, d_ff, dff_chunk)):
        cs = slice(c0, c0 + dff_chunk)
        gp, up = nxt
        def scores(r):
            return _decode_round_scores(first_round + r, decode['n_steps'], decode_refs,
                                        j_static=r % decode['nj'] if whole_sequences else None,
                                        maybe_first=r == 0, maybe_last=r >= rph - 2, **round_cfg)

        hosted = rounds_at[ci]
        state = scores(hosted[0]) if hosted else None
        if c0 + dff_chunk < d_ff:
            nxt = project(c0 + dff_chunk)
        if hosted:
            _decode_round_values(state)
        for r in hosted[1:]:
            _decode_round_values(scores(r))
        if multi_seq:
            first1, first2 = pm_ref[0, :, cs], pm_ref[1, :, cs]
        else:
            c6, c7 = carry_ref[SUBLANES - 2:SUBLANES - 1, cs], carry_ref[SUBLANES - 1:SUBLANES, cs]
            first1 = c7
            first2 = jnp.where(row == 0, c6, c7)
            carry_ref[:, cs] = gp[tm - SUBLANES:tm, :]
        gm1 = jnp.where(pos >= 1, pltpu.roll(gp, 1, axis=0), first1)
        gm2 = jnp.where(pos >= 2, pltpu.roll(gp, 2, axis=0), first2)
        gate = cw_ref[0:1, cs] * gm2 + cw_ref[1:2, cs] * gm1 + cw_ref[2:3, cs] * gp
        act = (_silu(gate) * up).astype(BF16)
        acc = acc + jnp.dot(act, wd_ref[cs, :], preferred_element_type=F32)
        tail_ref[0, :, cs] = gp[tm - tail_rows:tm, :]
    x2 = x1 + acc
    if final_norm:
        x2 = _rms(x2, nfin_ref[...])
    y_ref[...] = x2


def _mix_ffn(x, oa, ob, cuv, cvw, wm, bse, wo, nf, wg, wu, cw, wd, pm, nfin, seq_len, tm, final_norm,
             decode_operands=None):
    n, d = x.shape
    multi_seq = seq_len < tm
    assert n % tm == 0 and (tm % seq_len == 0 if multi_seq else seq_len % tm == 0)
    assert tm % LANES == 0 and wm.shape[1] == LANES
    n_seq = 1 if multi_seq else n // seq_len
    seq_tiles = 1 if multi_seq else seq_len // tm
    n_tiles = n // tm
    d_ff = wg.shape[1]
    dff_chunk = 256
    assert d_ff % dff_chunk == 0
    tail_rows = tm if multi_seq else SUBLANES
    w_c = cuv.shape[1] // 2

    def rows(w):
        return pl.BlockSpec((tm, w), lambda s, t: (s * seq_tiles + t, 0))

    def const(a):
        nd = a.ndim
        return pl.BlockSpec(a.shape, lambda s, t: (0,) * nd, pipeline_mode=pl.Buffered(1))

    if multi_seq:
        pm_spec = pl.BlockSpec((2, tm, d_ff), lambda s, t: (0, s * seq_tiles + t, 0))
        grid = (n_tiles, 1)
    else:
        pm_spec = pl.BlockSpec((1, SUBLANES, d_ff), lambda s, t: (s, 0, 0))
        grid = (n_seq, seq_tiles)
    n_tail = n_tiles if multi_seq else n_seq
    args = [x, oa, ob, cuv, cvw, wm, bse, wo, nf, wg, wu, cw, wd, pm, nfin]
    in_specs = [rows(d), rows(oa.shape[1]), rows(ob.shape[1]), rows(cuv.shape[1]),
                const(cvw), const(wm), const(bse), const(wo), const(nf), const(wg), const(wu), const(cw),
                const(wd), pm_spec, const(nfin)]
    out_specs = [rows(d), pl.BlockSpec((1, tail_rows, d_ff), lambda s, t: (s, 0, 0)), rows(w_c)]
    out_shape = [jax.ShapeDtypeStruct((n, d), F32),
                 jax.ShapeDtypeStruct((n_tail, tail_rows, d_ff), F32),
                 jax.ShapeDtypeStruct((n, w_c), F32)]
    scratch = [pltpu.VMEM((SUBLANES, d_ff), F32)]
    decode = None
    if decode_operands is not None:
        d_args, d_in_specs, d_out_shape, d_out_spec, d_scratch, decode = _decode_plumbing(
            *decode_operands, n_hosts=n_tiles)
        args += d_args
        in_specs += d_in_specs
        out_specs[2] = d_out_spec
        out_shape[2] = d_out_shape
        scratch += d_scratch
    return pl.pallas_call(
        functools.partial(_mix_ffn_kernel, seq_len=seq_len, dff_chunk=dff_chunk, final_norm=final_norm,
                          multi_seq=multi_seq, decode=decode),
        grid=grid,
        in_specs=in_specs,
        out_specs=out_specs,
        out_shape=out_shape,
        scratch_shapes=scratch,
        compiler_params=_cparams(("arbitrary", "arbitrary")),
        name="mix_ffn",
    )(*args)


def _pad_axis(a, axis, before, after):
    pads = [(0, 0)] * a.ndim
    pads[axis] = (before, after)
    return jnp.pad(a, pads)


def _lanes_block(parts):
    lead = parts[0][1].shape[:-1]
    out = jnp.zeros(lead + (LANES,), F32)
    for off, a in parts:
        out = lax.dynamic_update_slice_in_dim(out, a.astype(F32), off, axis=a.ndim - 1)
    return out


def kernel(x_prompt, x_sample, cache_fox_k, cache_fox_v, cache_fox_logf, page_table, state_gdn, state_gdn_conv,
           state_ffn_conv, norm_mix, w_in, gdn_conv_w, gdn_a_log, gdn_dt_bias, gdn_norm_w, fox_f_bias,
           fox_out_norm, chunk_v_norm, chunk_ws, chunk_bs, w_out, norm_ffn, ffn_w_gate, ffn_w_up, ffn_conv_w,
           ffn_w_down, norm_final):
    depth, d_model, _ = w_in.shape
    b_p, seq, _ = x_prompt.shape
    b_s, t_s, _ = x_sample.shape
    gh, fh, cg = gdn_a_log.shape[1], fox_f_bias.shape[1], chunk_ws.shape[1]
    assert gdn_norm_w.shape[1] == HEAD_DIM and gh % 2 == 0 and fh % 2 == 0 and cg % 2 == 0
    w_a, w_b, w_c = gh * HEAD_DIM, fh * HEAD_DIM, cg * HEAD_DIM
    d_ff = ffn_w_gate.shape[2]
    conv_k = gdn_conv_w.shape[1]
    ffn_k = ffn_conv_w.shape[1]
    chunk_len = chunk_ws.shape[2]
    n_phys, page = cache_fox_k.shape[1], cache_fox_k.shape[2]
    assert gh <= SUBLANES and fh <= SUBLANES and conv_k - 1 <= t_s <= GDN_CHUNK and ffn_k == 3
    assert chunk_len == LANES and seq % chunk_len == 0 and t_s <= chunk_len

    sizes = [3 * w_a, w_a, gh, gh, 3 * w_b, fh, 2 * w_c]
    starts = [0]
    for s in sizes:
        starts.append(starts[-1] + s)
    i_aqkv, i_az, i_beta, i_alpha, i_bqkv, i_bf, i_cuv = starts[:-1]
    offs, pos = {}, 0
    for name, width in [('a_qkv', 3 * w_a), ('a_z', w_a), ('b_q', w_b), ('b_k', w_b), ('b_v', w_b),
                        ('c_uv', 2 * w_c), ('small', LANES)]:
        offs[name] = (pos, pos + width)
        pos += width

    xp = x_prompt.reshape(b_p * seq, d_model)
    xs = x_sample.reshape(b_s * t_s, d_model)
    cache_kt = cache_fox_k.transpose(0, 1, 3, 4, 2).reshape(depth, n_phys, w_b, page)
    cache_vt = cache_fox_v.transpose(0, 1, 3, 4, 2).reshape(depth, n_phys, w_b, page)
    suffix = _page_suffix(cache_fox_logf.transpose(0, 3, 1, 2), math.gcd(n_phys, 512))

    tri = jnp.tril(jnp.ones((chunk_len, chunk_len), bool))
    tri_s = jnp.tril(jnp.ones((t_s, t_s), bool))
    n_rep = chunk_len // t_s
    assert (b_s * t_s) % chunk_len == 0 and chunk_len % t_s == 0

    prompt_rows, sample_rows = [], []
    w_in_t = w_in.transpose(2, 0, 1)
    kt_all = vt_all = None
    logft_rows = []
    for l in range(depth):
        w = w_in_t[:, l, :]

        def zrows(n):
            return jnp.zeros((n, d_model), F32)

        small_w = jnp.concatenate([
            w[i_beta:i_beta + gh], zrows(SM_G - SM_BETA - gh),
            w[i_alpha:i_alpha + gh], zrows(SM_F - SM_G - gh),
            w[i_bf:i_bf + fh], zrows(LANES - SM_F - fh)], axis=0)
        w_r = jnp.concatenate([w[i_aqkv:i_aqkv + 3 * w_a], w[i_az:i_az + w_a],
                               w[i_bqkv:i_bqkv + 3 * w_b], w[i_cuv:i_cuv + 2 * w_c], small_w],
                              axis=0).astype(BF16)
        par = jnp.concatenate([
            _lanes_block([(SM_G, gdn_dt_bias[l][None, :]), (SM_F, fox_f_bias[l][None, :])]),
            _lanes_block([(SM_G, gdn_a_log[l][None, :])]),
            jnp.zeros((SUBLANES - 2, LANES), F32)], axis=0)
        nm = norm_mix[l][None, :]
        gnw = jnp.tile(gdn_norm_w[l], gh)[None, :]
        fnw = jnp.tile(fox_out_norm[l], fh)[None, :]
        cvw = jnp.tile(chunk_v_norm[l], cg)[None, :]
        wo = w_out[l].astype(BF16)
        wg, wu, wd = ffn_w_gate[l].astype(BF16), ffn_w_up[l].astype(BF16), ffn_w_down[l].astype(BF16)
        nf = norm_ffn[l][None, :]
        nfin = norm_final[None, :]
        last = l == depth - 1

        aqkv, az, bq, kt_all, vt_all, cuv, small, cumt, logft = _inproj(
            xp, nm, w_r, par, offs, seq, 512, kv_stack=(l, depth, kt_all, vt_all))
        logft_rows.append(logft)
        o_a, s_new = _gdn(aqkv.reshape(b_p, seq, 3 * w_a), az.reshape(b_p, seq, w_a),
                          small.reshape(b_p, seq, LANES), jnp.zeros((b_p, SUBLANES, 3 * w_a), F32),
                          jnp.zeros((b_p, gh, HEAD_DIM, HEAD_DIM), F32), gdn_conv_w[l], gnw, seq, 512, 1)
        o_b = _fox_prompt(l, bq.reshape(b_p, seq, w_b), kt_all, vt_all, cumt, fnw, 512)
        wm = jnp.where(tri, chunk_ws[l], 0.0).astype(BF16)
        bse = jnp.repeat(chunk_bs[l].T, HEAD_DIM, axis=1)
        n_s = b_s * t_s
        aqkv_p, cuv_p = aqkv, cuv
        aqkv, az, bq, bk, bv, cuv, small, cumt, _ = _inproj(xs, nm, w_r, par, offs, t_s, n_s)
        cn8 = _pad_axis(cumt.reshape(SUBLANES, b_s, t_s).transpose(1, 0, 2), 2, 0, SUBLANES - t_s)
        decode_operands = (l, page_table, bq.reshape(b_s, t_s, w_b),
                           _pad_axis(bk.reshape(b_s, t_s, w_b), 1, 0, SUBLANES - t_s),
                           _pad_axis(bv.reshape(b_s, t_s, w_b), 1, 0, SUBLANES - t_s),
                           cn8, fnw, cache_kt, cache_vt, suffix, t_s)
        xp, tail, o_b = _mix_ffn(xp, o_a.reshape(b_p * seq, w_a), o_b.reshape(b_p * seq, w_b), cuv_p, cvw, wm,
                                    bse, wo, nf, wg, wu, ffn_conv_w[l], wd,
                                    jnp.zeros((b_p, SUBLANES, d_ff), F32), nfin, seq, 512, last,
                                    decode_operands=decode_operands)
        prompt_rows.append((None, None, None, s_new,
                            aqkv_p.reshape(b_p, seq, 3 * w_a)[:, seq - (conv_k - 1):, :],
                            tail[:, SUBLANES - (ffn_k - 1):, :]))

        pad_t = GDN_CHUNK - t_s
        o_a, s_new = _gdn(_pad_axis(aqkv.reshape(b_s, t_s, 3 * w_a), 1, 0, pad_t),
                          _pad_axis(az.reshape(b_s, t_s, w_a), 1, 0, pad_t),
                          _pad_axis(small.reshape(b_s, t_s, LANES), 1, 0, pad_t),
                          _pad_axis(state_gdn_conv[l], 1, SUBLANES - (conv_k - 1), 0),
                          state_gdn[l], gdn_conv_w[l], gnw, t_s, GDN_CHUNK, math.gcd(b_s, 4))
        o_a = o_a[:, :t_s, :]
        eye_rep = jnp.eye(n_rep, dtype=F32)
        wm_s = jnp.stack([jnp.kron(eye_rep, jnp.where(tri_s, chunk_ws[l, g, :t_s, :t_s], 0.0))
                          for g in range(cg)]).astype(BF16)
        bse_s = jnp.repeat(jnp.tile(chunk_bs[l, :, :t_s], (1, n_rep)).T, HEAD_DIM, axis=1)
        buf = state_ffn_conv[l]
        zrow = jnp.zeros((b_s, 1, d_ff), F32)
        pm1 = jnp.concatenate([buf[:, 1:2]] + [zrow] * (t_s - 1), axis=1).reshape(n_s, d_ff)
        pm2 = jnp.concatenate([buf[:, 0:1], buf[:, 1:2]] + [zrow] * (t_s - 2), axis=1).reshape(n_s, d_ff)
        xs, tail, cvn = _mix_ffn(xs, o_a.reshape(n_s, w_a), o_b.reshape(n_s, w_b), cuv, cvw, wm_s, bse_s,
                                 wo, nf, wg, wu, ffn_conv_w[l], wd, jnp.stack([pm1, pm2]), nfin,
                                 t_s, chunk_len, last)
        sample_rows.append((bk.reshape(b_s, t_s, fh, HEAD_DIM), bv.reshape(b_s, t_s, fh, HEAD_DIM),
                            small[:, SM_F:SM_F + fh].reshape(b_s, t_s, fh), s_new,
                            aqkv.reshape(b_s, t_s, 3 * w_a)[:, t_s - (conv_k - 1):, :],
                            tail.reshape(b_s, t_s, d_ff)[:, t_s - (ffn_k - 1):, :],
                            cvn.reshape(b_s, t_s, w_c)))

    def stack(rows_, i):
        return jnp.stack([r[i] for r in rows_])

    y_prompt = xp.reshape(b_p, seq, d_model)
    y_sample = xs.reshape(b_s, t_s, d_model)
    fox_k_prompt = kt_all.reshape(depth, b_p, fh, HEAD_DIM, seq).transpose(0, 1, 4, 2, 3)
    fox_v_prompt = vt_all.reshape(depth, b_p, fh, HEAD_DIM, seq).transpose(0, 1, 4, 2, 3)
    fox_logf_prompt = jnp.stack(logft_rows)[:, :fh, :].reshape(depth, fh, b_p, seq).transpose(0, 2, 3, 1)
    return (y_prompt, y_sample,
            fox_k_prompt, fox_v_prompt, fox_logf_prompt, stack(prompt_rows, 3),
            stack(prompt_rows, 4), stack(prompt_rows, 5),
            stack(sample_rows, 0), stack(sample_rows, 1), stack(sample_rows, 2), stack(sample_rows, 3),
            stack(sample_rows, 4), stack(sample_rows, 5), stack(sample_rows, 6))
```
